```python
import math
import jax, jax.numpy as jnp
from jax import lax
import numpy as np

D_MODEL = 1024
BATCH = 32
SEQ = 256
DEPTH = 2
DEC_BATCH = 4
DEC_SEQ = 2048
PAST_LEN = 512

GRID_W = 64
N_BRANCHES = 3
NORM_EPS = 1e-6

HY_WIDTH = 384
HY_SHORT = 3
HY_FILTER_HIDDEN = 64
HY_N_BANDS = 8
HY_POS_DIM = 1 + 2 * HY_N_BANDS
HY_FAST_DECAY_PCT = 0.3
HY_SLOW_DECAY_PCT = 1.5
HY_DECAY_TARGET = 1e-2

RW_HEAD_DIM = 64
RW_HEADS = 6
RW_WIDTH = RW_HEADS * RW_HEAD_DIM
RW_SHORT = 3
RW_DECAY_LORA = 64
RW_ICLR_LORA = 64
RW_GATE_LORA = 128
RW_GN_EPS = 64e-5

S5_GROUPS = 16
S5_GROUP_CH = 16
S5_WIDTH = S5_GROUPS * S5_GROUP_CH
S5_STATE = 64
S5_DT_MIN = 1e-3
S5_DT_MAX = 1e-1

N_EXPERTS = 16
N_EXPERT_GROUPS = 4
EXPERTS_PER_GROUP = N_EXPERTS // N_EXPERT_GROUPS
TOP_K = 2
EXPERT_FF = 512

C_GATE = N_BRANCHES * D_MODEL
C_HY = 3 * HY_WIDTH
C_RKV = 3 * RW_WIDTH
C_LW = 2 * RW_DECAY_LORA
C_LA = 2 * RW_ICLR_LORA
C_LG = RW_GATE_LORA
C_S5 = S5_WIDTH
IN_COLS = C_GATE + C_HY + C_RKV + C_LW + C_LA + C_LG + C_S5
IN_SPLITS = (
    C_GATE,
    C_GATE + C_HY,
    C_GATE + C_HY + C_RKV,
    C_GATE + C_HY + C_RKV + C_LW,
    C_GATE + C_HY + C_RKV + C_LW + C_LA,
    C_GATE + C_HY + C_RKV + C_LW + C_LA + C_LG,
)

kernel_name = 'hyena_rwkv7_s5_moe_prefix_diffusion_step'

F32 = jnp.float32


def _rmsnorm(x, g):
    xf = x.astype(F32)
    return xf * lax.rsqrt(jnp.mean(xf * xf, axis=-1, keepdims=True) + NORM_EPS) * g.astype(F32)


def _modulation(cond, w, b):
    m = jax.nn.silu(cond.astype(F32)) @ w + b
    return jnp.split(m[:, None, :], 6, axis=-1)


def _short_conv(x, w, b, rows):
    bsz, L, ch = x.shape
    xs = x if rows is None else x.reshape(bsz, rows, L // rows, ch)
    ax = xs.ndim - 2
    n = xs.shape[ax]
    pad = [(0, 0)] * xs.ndim
    pad[ax] = (1, 1)
    xp = jnp.pad(xs, pad)
    y = (lax.slice_in_dim(xp, 0, n, axis=ax) * w[0] + xs * w[1]
         + lax.slice_in_dim(xp, 2, n + 2, axis=ax) * w[2] + b)
    return y.reshape(bsz, L, ch)


def _hyena_filters(L, w1, b1, f1, w2, b2, f2, w3):
    t = jnp.arange(L, dtype=F32)
    t01 = t / max(L - 1, 1)
    bands = jnp.linspace(1e-4, HY_N_BANDS - 1, HY_N_BANDS, dtype=F32)
    ang = (2.0 * math.pi / L) * t[:, None] * bands[None, :]
    feats = jnp.concatenate([t01[:, None], jnp.cos(ang), -jnp.sin(ang)], axis=-1)
    h = jnp.sin(f1 * (feats @ w1 + b1))
    h = jnp.sin(f2 * (h @ w2 + b2))
    h = h @ w3
    max_decay = math.log(HY_DECAY_TARGET) / HY_FAST_DECAY_PCT
    min_decay = math.log(HY_DECAY_TARGET) / HY_SLOW_DECAY_PCT
    deltas = jnp.abs(jnp.linspace(min_decay, max_decay, HY_WIDTH, dtype=F32))
    h = h * jnp.exp(-t01[:, None] * jnp.concatenate([deltas, deltas])[None, :])
    h_f, h_b = h[:, :HY_WIDTH], h[:, HY_WIDTH:]
    l1 = jnp.sum(jnp.abs(h_f), axis=0) + jnp.sum(jnp.abs(h_b), axis=0) + 1e-6
    return h_f / l1, h_b / l1


def _bidir_long_conv(z, h_f, h_b):
    L = z.shape[1]
    k = jnp.concatenate([h_f, jnp.zeros((1, h_f.shape[1]), h_f.dtype), h_b[:0:-1]], axis=0)
    zf = jnp.fft.rfft(z.astype(F32), n=2 * L, axis=1)
    kf = jnp.fft.rfft(k, n=2 * L, axis=0)
    return jnp.fft.irfft(zf * kf[None], n=2 * L, axis=1)[:, :L]


def _hyena(u, rows, conv_w, conv_b, w1, b1, f1, w2, b2, f2, w3, bias):
    u = _short_conv(u, conv_w, conv_b, rows)
    x0, x1, v = jnp.split(u, 3, axis=-1)
    h_f, h_b = _hyena_filters(u.shape[1], w1, b1, f1, w2, b2, f2, w3)
    v = v * x0
    v = _bidir_long_conv(v, h_f, h_b) + v * bias
    return v * x1


def _wkv_scan(s0, r, w, kk, a, kt, v, reverse):
    def step(S, inp):
        r_t, w_t, kk_t, a_t, k_t, v_t = inp
        sk = jnp.einsum('bhvk,bhk->bhv', S, kk_t)
        S = (S * w_t[:, :, None, :] - sk[..., None] * (kk_t * a_t)[:, :, None, :]
             + v_t[..., None] * k_t[:, :, None, :])
        return S, jnp.einsum('bhvk,bhk->bhv', S, r_t)
    seq = tuple(jnp.moveaxis(t.astype(F32), 1, 0) for t in (r, w, kk, a, kt, v))
    S, o = lax.scan(step, s0, seq, reverse=reverse)
    return S, jnp.moveaxis(o, 0, 1)


def _rwkv7(rkv, lora_w, lora_a, lora_g, s0, rows, conv_w, conv_b, w0, w_up, a0, a_up, g_up,
           k_k, k_a, r_k, gn_w, gn_b):
    bsz, L, _ = rkv.shape
    hs = (bsz, L, RW_HEADS, RW_HEAD_DIM)
    r, k, v = jnp.split(_short_conv(rkv, conv_w, conv_b, rows).astype(F32), 3, axis=-1)
    lw = lora_w.reshape(bsz, L, 2, RW_DECAY_LORA)
    la = lora_a.reshape(bsz, L, 2, RW_ICLR_LORA)
    wraw = w0 + jnp.einsum('bldr,drc->bldc', jnp.tanh(lw), w_up)
    decay = jnp.exp(-jnp.exp(-jax.nn.softplus(-wraw) - 0.5))
    a = jax.nn.sigmoid(a0 + jnp.einsum('bldr,drc->bldc', la, a_up))
    kk = (k * k_k).reshape(hs)
    kk = kk * lax.rsqrt(jnp.maximum(jnp.sum(kk * kk, axis=-1, keepdims=True), 1e-24))
    kt = k[:, :, None, :] * (1.0 + (a - 1.0) * k_a)
    rh, vh = r.reshape(hs), v.reshape(hs)
    s0 = s0.astype(F32)
    outs, finals = [], []
    for d in range(2):
        S, o = _wkv_scan(s0[:, d], rh, decay[:, :, d].reshape(hs), kk, a[:, :, d].reshape(hs),
                         kt[:, :, d].reshape(hs), vh, reverse=(d == 1))
        outs.append(o)
        finals.append(S)
    o = outs[0] + outs[1]
    mu = jnp.mean(o, axis=-1, keepdims=True)
    var = jnp.mean(jnp.square(o - mu), axis=-1, keepdims=True)
    o = ((o - mu) * lax.rsqrt(var + RW_GN_EPS)).reshape(bsz, L, RW_WIDTH) * gn_w + gn_b
    bonus = jnp.sum(rh * (kt[:, :, 0] + kt[:, :, 1]).reshape(hs) * r_k, axis=-1, keepdims=True) * vh
    g = jax.nn.sigmoid(lora_g) @ g_up
    return (o + bonus.reshape(bsz, L, RW_WIDTH)) * g, jnp.stack(finals, axis=1)


def _cmul(ar, ai, br, bi):
    return ar * br - ai * bi, ar * bi + ai * br


def _ssm_combine(e1, e2):
    a1r, a1i, b1r, b1i = e1
    a2r, a2i, b2r, b2i = e2
    ar, ai = _cmul(a2r, a2i, a1r, a1i)
    br, bi = _cmul(a2r, a2i, b1r, b1i)
    return ar, ai, br + b2r, bi + b2i


def _s5(u, x0_re, x0_im, lam_re, lam_im, log_dt, b_re, b_im, c_re, c_im, d_skip, glu_w, glu_b):
    bsz, L, _ = u.shape
    u = u.astype(F32)
    ug = u.reshape(bsz, L, S5_GROUPS, S5_GROUP_CH)
    y = u * d_skip
    fin_re, fin_im = [], []
    for d in range(2):
        lr, li = lam_re[d].astype(F32), lam_im[d].astype(F32)
        dt = jnp.exp(log_dt[d].astype(F32))[:, None]
        mag = jnp.exp(lr * dt)
        ar, ai = mag * jnp.cos(li * dt), mag * jnp.sin(li * dt)
        den = lr * lr + li * li
        qr = ((ar - 1.0) * lr + ai * li) / den
        qi = (ai * lr - (ar - 1.0) * li) / den
        bbr, bbi = _cmul(qr[..., None], qi[..., None], b_re[d], b_im[d])
        bur = jnp.einsum('gnp,blgp->blgn', bbr, ug)
        bui = jnp.einsum('gnp,blgp->blgn', bbi, ug)
        ir, ii = _cmul(ar, ai, x0_re[:, d].astype(F32), x0_im[:, d].astype(F32))
        first = 0 if d == 0 else L - 1
        bur = bur.at[:, first].add(ir)
        bui = bui.at[:, first].add(ii)
        are = jnp.broadcast_to(ar, bur.shape)
        aim = jnp.broadcast_to(ai, bur.shape)
        _, _, xr, xi = lax.associative_scan(_ssm_combine, (are, aim, bur, bui), reverse=(d == 1), axis=1)
        y = y + (jnp.einsum('gpn,blgn->blgp', c_re[d], xr)
                 - jnp.einsum('gpn,blgn->blgp', c_im[d], xi)).reshape(bsz, L, S5_WIDTH)
        last = L - 1 if d == 0 else 0
        fin_re.append(xr[:, last])
        fin_im.append(xi[:, last])
    y = jax.nn.gelu(y)
    y = y * jax.nn.sigmoid(y @ glu_w + glu_b)
    return y, jnp.stack(fin_re, axis=1), jnp.stack(fin_im, axis=1)


def _moe(h, router_w, router_b, wg, wu, wd):
    bsz, L, dm = h.shape
    t = h.reshape(bsz * L, dm)
    scores = jax.nn.sigmoid((t @ router_w).astype(F32))
    sel = scores + router_b
    grp_score = jnp.sum(lax.top_k(sel.reshape(-1, N_EXPERT_GROUPS, EXPERTS_PER_GROUP), TOP_K)[0], axis=-1)
    best = jnp.argmax(grp_score, axis=-1)
    in_group = (jnp.arange(N_EXPERTS) // EXPERTS_PER_GROUP)[None, :] == best[:, None]
    _, idx = lax.top_k(jnp.where(in_group, sel, -1e9), TOP_K)
    w = jnp.take_along_axis(scores, idx, axis=-1)
    w = w / jnp.sum(w, axis=-1, keepdims=True)
    gates = jnp.einsum('tk,tke->te', w, jax.nn.one_hot(idx, N_EXPERTS, dtype=w.dtype))
    out = jnp.zeros((bsz * L, dm), F32)
    for e in range(N_EXPERTS):
        he = jax.nn.silu(t @ wg[e]) * (t @ wu[e])
        out = out + gates[:, e:e + 1] * (he @ wd[e])
    return out.reshape(bsz, L, dm)


def _layer(x, cond, rw_s0, s5_re0, s5_im0, rows, l, P):
    bsz, L, _ = x.shape
    sh1, sc1, gt1, sh2, sc2, gt2 = _modulation(cond, P['ada_w'][l], P['ada_b'][l])
    xn = _rmsnorm(x, P['norm1_g'][l]) * (1.0 + sc1) + sh1
    proj = xn @ P['w_in'][l]
    g_in, hy_in, rkv_in, lw_in, la_in, lg_in, s5_in = jnp.split(proj, IN_SPLITS, axis=-1)
    y_hy = _hyena(hy_in, rows, P['hy_conv_w'][l], P['hy_conv_b'][l], P['hy_f_w1'][l], P['hy_f_b1'][l],
                  P['hy_f_freq1'][l], P['hy_f_w2'][l], P['hy_f_b2'][l], P['hy_f_freq2'][l],
                  P['hy_f_w3'][l], P['hy_bias'][l])
    y_rw, rw_fin = _rwkv7(rkv_in, lw_in, la_in, lg_in, rw_s0, rows, P['rw_conv_w'][l], P['rw_conv_b'][l],
                          P['rw_w0'][l], P['rw_w_up'][l], P['rw_a0'][l], P['rw_a_up'][l], P['rw_g_up'][l],
                          P['rw_k_k'][l], P['rw_k_a'][l], P['rw_r_k'][l], P['rw_gn_w'][l], P['rw_gn_b'][l])
    y_s5, s5r_fin, s5i_fin = _s5(s5_in, s5_re0, s5_im0, P['s5_lam_re'][l], P['s5_lam_im'][l],
                                 P['s5_log_dt'][l], P['s5_b_re'][l], P['s5_b_im'][l], P['s5_c_re'][l],
                                 P['s5_c_im'][l], P['s5_d'][l], P['s5_glu_w'][l], P['s5_glu_b'][l])
    gate = jax.nn.sigmoid(g_in).reshape(bsz, L, N_BRANCHES, D_MODEL)
    merged = (gate[:, :, 0] * (y_hy @ P['wb_hy'][l]) + gate[:, :, 1] * (y_rw @ P['wb_rw'][l])
              + gate[:, :, 2] * (y_s5 @ P['wb_s5'][l]))
    x = x + gt1 * (merged @ P['w_out'][l])
    hn = _rmsnorm(x, P['norm2_g'][l]) * (1.0 + sc2) + sh2
    x = x + gt2 * _moe(hn, P['router_w'], P['router_b'], P['exp_wg'][l], P['exp_wu'][l], P['exp_wd'][l])
    return x, rw_fin, s5r_fin, s5i_fin


def setup_inputs(seed: int = 0) -> dict:
    key = jax.random.key(seed)
    keys = iter(jax.random.split(key, 80))

    def nrm(shape, scale):
        return scale * jax.random.normal(next(keys), shape, F32)

    H = HY_FILTER_HIDDEN
    s5_shape = (DEPTH, 2, S5_GROUPS, S5_STATE)
    lam_im = jnp.broadcast_to(math.pi * jnp.arange(S5_STATE, dtype=F32), s5_shape) + nrm(s5_shape, 0.01)
    log_dt = jax.random.uniform(next(keys), (DEPTH, 2, S5_GROUPS), F32,
                                math.log(S5_DT_MIN), math.log(S5_DT_MAX))
    return {
        'x_prompt': nrm((BATCH, SEQ, D_MODEL), 1.0),
        'x_sample': nrm((DEC_BATCH, DEC_SEQ, D_MODEL), 1.0),
        'state_rwkv': nrm((DEC_BATCH, DEPTH, 2, RW_HEADS, RW_HEAD_DIM, RW_HEAD_DIM), 0.5),
        'state_s5_re': nrm((DEC_BATCH, DEPTH, 2, S5_GROUPS, S5_STATE), 0.5),
        'state_s5_im': nrm((DEC_BATCH, DEPTH, 2, S5_GROUPS, S5_STATE), 0.5),
        'c': nrm((DEC_BATCH, D_MODEL), 1.0),
        'c_ctx': nrm((D_MODEL,), 1.0),
        'ada_w': nrm((DEPTH, D_MODEL, 6 * D_MODEL), 0.5 * D_MODEL ** -0.5),
        'ada_b': nrm((DEPTH, 6 * D_MODEL), 0.02),
        'norm1_g': 1.0 + nrm((DEPTH, D_MODEL), 0.02),
        'norm2_g': 1.0 + nrm((DEPTH, D_MODEL), 0.02),
        'final_g': 1.0 + nrm((D_MODEL,), 0.02),
        'w_in': nrm((DEPTH, D_MODEL, IN_COLS), D_MODEL ** -0.5),
        'hy_conv_w': nrm((DEPTH, HY_SHORT, C_HY), HY_SHORT ** -0.5),
        'hy_conv_b': nrm((DEPTH, C_HY), 0.02),
        'hy_f_w1': nrm((DEPTH, HY_POS_DIM, H), HY_POS_DIM ** -0.5),
        'hy_f_b1': nrm((DEPTH, H), 0.1),
        'hy_f_freq1': 1.0 + nrm((DEPTH, H), 0.02),
        'hy_f_w2': nrm((DEPTH, H, H), H ** -0.5),
        'hy_f_b2': nrm((DEPTH, H), 0.1),
        'hy_f_freq2': 1.0 + nrm((DEPTH, H), 0.02),
        'hy_f_w3': nrm((DEPTH, H, 2 * HY_WIDTH), H ** -0.5),
        'hy_bias': nrm((DEPTH, HY_WIDTH), 0.5),
        'rw_conv_w': nrm((DEPTH, RW_SHORT, C_RKV), RW_SHORT ** -0.5),
        'rw_conv_b': nrm((DEPTH, C_RKV), 0.02),
        'rw_w0': -1.0 + nrm((DEPTH, 2, RW_WIDTH), 0.5),
        'rw_w_up': nrm((DEPTH, 2, RW_DECAY_LORA, RW_WIDTH), 0.1),
        'rw_a0': nrm((DEPTH, 2, RW_WIDTH), 0.5),
        'rw_a_up': nrm((DEPTH, 2, RW_ICLR_LORA, RW_WIDTH), 0.1),
        'rw_g_up': nrm((DEPTH, RW_GATE_LORA, RW_WIDTH), RW_GATE_LORA ** -0.5),
        'rw_k_k': 0.85 + nrm((DEPTH, RW_WIDTH), 0.05),
        'rw_k_a': 1.0 + nrm((DEPTH, RW_WIDTH), 0.05),
        'rw_r_k': nrm((DEPTH, RW_HEADS, RW_HEAD_DIM), 0.1),
        'rw_gn_w': 1.0 + nrm((DEPTH, RW_WIDTH), 0.02),
        'rw_gn_b': nrm((DEPTH, RW_WIDTH), 0.02),
        's5_lam_re': -0.5 + nrm(s5_shape, 0.01),
        's5_lam_im': lam_im,
        's5_log_dt': log_dt,
        's5_b_re': nrm((DEPTH, 2, S5_GROUPS, S5_STATE, S5_GROUP_CH), (2 * S5_GROUP_CH) ** -0.5),
        's5_b_im': nrm((DEPTH, 2, S5_GROUPS, S5_STATE, S5_GROUP_CH), (2 * S5_GROUP_CH) ** -0.5),
        's5_c_re': nrm((DEPTH, 2, S5_GROUPS, S5_GROUP_CH, S5_STATE), (2 * S5_STATE) ** -0.5),
        's5_c_im': nrm((DEPTH, 2, S5_GROUPS, S5_GROUP_CH, S5_STATE), (2 * S5_STATE) ** -0.5),
        's5_d': nrm((DEPTH, S5_WIDTH), 1.0),
        's5_glu_w': nrm((DEPTH, S5_WIDTH, S5_WIDTH), S5_WIDTH ** -0.5),
        's5_glu_b': nrm((DEPTH, S5_WIDTH), 0.02),
        'wb_hy': nrm((DEPTH, HY_WIDTH, D_MODEL), HY_WIDTH ** -0.5),
        'wb_rw': nrm((DEPTH, RW_WIDTH, D_MODEL), RW_WIDTH ** -0.5),
        'wb_s5': nrm((DEPTH, S5_WIDTH, D_MODEL), S5_WIDTH ** -0.5),
        'w_out': nrm((DEPTH, D_MODEL, D_MODEL), D_MODEL ** -0.5),
        'router_w': nrm((D_MODEL, N_EXPERTS), D_MODEL ** -0.5),
        'router_b': nrm((N_EXPERTS,), 0.01),
        'exp_wg': nrm((DEPTH, N_EXPERTS, D_MODEL, EXPERT_FF), D_MODEL ** -0.5),
        'exp_wu': nrm((DEPTH, N_EXPERTS, D_MODEL, EXPERT_FF), D_MODEL ** -0.5),
        'exp_wd': nrm((DEPTH, N_EXPERTS, EXPERT_FF, D_MODEL), EXPERT_FF ** -0.5),
    }


def reference(x_prompt, x_sample, state_rwkv, state_s5_re, state_s5_im, c, c_ctx,
              ada_w, ada_b, norm1_g, norm2_g, final_g, w_in,
              hy_conv_w, hy_conv_b, hy_f_w1, hy_f_b1, hy_f_freq1, hy_f_w2, hy_f_b2, hy_f_freq2, hy_f_w3, hy_bias,
              rw_conv_w, rw_conv_b, rw_w0, rw_w_up, rw_a0, rw_a_up, rw_g_up, rw_k_k, rw_k_a, rw_r_k,
              rw_gn_w, rw_gn_b,
              s5_lam_re, s5_lam_im, s5_log_dt, s5_b_re, s5_b_im, s5_c_re, s5_c_im, s5_d, s5_glu_w, s5_glu_b,
              wb_hy, wb_rw, wb_s5, w_out, router_w, router_b, exp_wg, exp_wu, exp_wd):
    P = dict(ada_w=ada_w, ada_b=ada_b, norm1_g=norm1_g, norm2_g=norm2_g, w_in=w_in,
             hy_conv_w=hy_conv_w, hy_conv_b=hy_conv_b, hy_f_w1=hy_f_w1, hy_f_b1=hy_f_b1,
             hy_f_freq1=hy_f_freq1, hy_f_w2=hy_f_w2, hy_f_b2=hy_f_b2, hy_f_freq2=hy_f_freq2,
             hy_f_w3=hy_f_w3, hy_bias=hy_bias,
             rw_conv_w=rw_conv_w, rw_conv_b=rw_conv_b, rw_w0=rw_w0, rw_w_up=rw_w_up, rw_a0=rw_a0,
             rw_a_up=rw_a_up, rw_g_up=rw_g_up, rw_k_k=rw_k_k, rw_k_a=rw_k_a, rw_r_k=rw_r_k,
             rw_gn_w=rw_gn_w, rw_gn_b=rw_gn_b,
             s5_lam_re=s5_lam_re, s5_lam_im=s5_lam_im, s5_log_dt=s5_log_dt, s5_b_re=s5_b_re,
             s5_b_im=s5_b_im, s5_c_re=s5_c_re, s5_c_im=s5_c_im, s5_d=s5_d, s5_glu_w=s5_glu_w,
             s5_glu_b=s5_glu_b, wb_hy=wb_hy, wb_rw=wb_rw, wb_s5=wb_s5, w_out=w_out,
             router_w=router_w, router_b=router_b, exp_wg=exp_wg, exp_wu=exp_wu, exp_wd=exp_wd)

    bp = x_prompt.shape[0]
    xc = x_prompt.astype(F32)
    zero_rw = jnp.zeros((bp, 2, RW_HEADS, RW_HEAD_DIM, RW_HEAD_DIM), F32)
    zero_s5 = jnp.zeros((bp, 2, S5_GROUPS, S5_STATE), F32)
    cond_ctx = c_ctx[None, :]
    rw_new, s5r_new, s5i_new = [], [], []
    for l in range(DEPTH):
        xc, rw_fin, s5r_fin, s5i_fin = _layer(xc, cond_ctx, zero_rw, zero_s5, zero_s5, None, l, P)
        rw_new.append(rw_fin)
        s5r_new.append(s5r_fin)
        s5i_new.append(s5i_fin)
    y_prompt = _rmsnorm(xc, final_g).astype(x_prompt.dtype)
    new_state_rwkv = jnp.stack(rw_new, axis=1)
    new_state_s5_re = jnp.stack(s5r_new, axis=1)
    new_state_s5_im = jnp.stack(s5i_new, axis=1)

    rows = x_sample.shape[1] // GRID_W
    xs = x_sample.astype(F32)
    for l in range(DEPTH):
        xs, _, _, _ = _layer(xs, c, state_rwkv[:, l], state_s5_re[:, l], state_s5_im[:, l], rows, l, P)
    y_sample = _rmsnorm(xs, final_g).astype(x_sample.dtype)

    return (y_prompt, y_sample, new_state_rwkv, new_state_s5_re, new_state_s5_im)
```

```python
import functools
import math

import jax
import jax.numpy as jnp
import numpy as np
from jax import lax
from jax.experimental import pallas as pl
from jax.experimental.pallas import tpu as pltpu

F32 = jnp.float32
BF16 = jnp.bfloat16
HI = lax.Precision.HIGHEST

D_MODEL = 1024
DEPTH = 2
GRID_W = 64
NORM_EPS = 1e-6

HY_WIDTH = 384
HY_FILTER_HIDDEN = 64
HY_N_BANDS = 8
HY_POS_DIM = 1 + 2 * HY_N_BANDS
HY_FAST_DECAY_PCT = 0.3
HY_SLOW_DECAY_PCT = 1.5
HY_DECAY_TARGET = 1e-2

RW_HEAD_DIM = 64
RW_HEADS = 6
RW_WIDTH = RW_HEADS * RW_HEAD_DIM
RW_GN_EPS = 64e-5
RW_CHUNK = 64

S5_GROUPS = 16
S5_GROUP_CH = 16
S5_WIDTH = S5_GROUPS * S5_GROUP_CH
S5_STATE = 64
S5_NS = S5_GROUPS * S5_STATE

N_EXPERTS = 16
N_EXPERT_GROUPS = 4
EXPERTS_PER_GROUP = N_EXPERTS // N_EXPERT_GROUPS
EXPERT_FF = 512

C_GATE = 3 * D_MODEL
C_HY = 3 * HY_WIDTH
C_RKV = 3 * RW_WIDTH
C_LORA = 384
C_S5 = S5_WIDTH
IN_COLS = C_GATE + C_HY + C_RKV + C_LORA + C_S5

V7X_VMEM_LIMIT = 56 * 1024 * 1024

TM_TOK = 256
TL_SEQ = 256


def _cparams(sem):
    return pltpu.CompilerParams(dimension_semantics=sem, vmem_limit_bytes=V7X_VMEM_LIMIT)


def _dot(a, b, precision=None):
    return jnp.dot(a, b, preferred_element_type=F32, precision=precision)


def _mod_kernel(c_ref, w_ref, b_ref, o_ref):
    c = c_ref[...]
    s = c * jax.nn.sigmoid(c)
    o_ref[...] = _dot(s, w_ref[...], HI) + b_ref[...]


def _modulation(cond8, ada_w, ada_b, l):
    tn = 1536
    return pl.pallas_call(
        _mod_kernel,
        grid=(6 * D_MODEL // tn,),
        in_specs=[
            pl.BlockSpec((8, D_MODEL), lambda j: (0, 0)),
            pl.BlockSpec((None, D_MODEL, tn), lambda j: (l, 0, j)),
            pl.BlockSpec((None, 1, tn), lambda j: (l, 0, j)),
        ],
        out_specs=pl.BlockSpec((8, tn), lambda j: (0, j)),
        out_shape=jax.ShapeDtypeStruct((8, 6 * D_MODEL), F32),
        compiler_params=_cparams(("arbitrary",)),
        name="modulation",
    )(cond8, ada_w, ada_b.reshape(DEPTH, 1, 6 * D_MODEL))


def _inproj_kernel(x_ref, sh_ref, sc_ref, g_ref, w_ref, og, ohy, orkv, olo, os5):
    x = x_ref[...]
    xn = x * lax.rsqrt(jnp.mean(x * x, axis=-1, keepdims=True) + NORM_EPS) * g_ref[...]
    xn = (xn * (1.0 + sc_ref[...]) + sh_ref[...]).astype(BF16)
    off = 0
    for o_ref, width in ((og, C_GATE), (ohy, C_HY), (orkv, C_RKV), (olo, C_LORA), (os5, C_S5)):
        o_ref[...] = _dot(xn, w_ref[:, off:off + width])
        off += width


def _inproj(x, sh, sc, norm_g, w_in_bf, l, tiles_per_stream):
    n_tok = x.shape[0]
    tm = TM_TOK
    stream = lambda i: (i // tiles_per_stream, 0, 0)
    widths = (C_GATE, C_HY, C_RKV, C_LORA, C_S5)
    return pl.pallas_call(
        _inproj_kernel,
        grid=(n_tok // tm,),
        in_specs=[
            pl.BlockSpec((tm, D_MODEL), lambda i: (i, 0)),
            pl.BlockSpec((None, tm, D_MODEL), stream),
            pl.BlockSpec((None, tm, D_MODEL), stream),
            pl.BlockSpec((None, 1, D_MODEL), lambda i: (l, 0, 0)),
            pl.BlockSpec((None, D_MODEL, IN_COLS), lambda i: (l, 0, 0)),
        ],
        out_specs=[pl.BlockSpec((tm, w), lambda i: (i, 0)) for w in widths],
        out_shape=[jax.ShapeDtypeStruct((n_tok, w), F32) for w in widths],
        compiler_params=_cparams(("parallel",)),
        name="inproj",
    )(x, sh, sc, norm_g.reshape(DEPTH, 1, D_MODEL), w_in_bf)


def _conv3(x, w_ref, b_ref, period):
    n = x.shape[0]
    t = lax.broadcasted_iota(jnp.int32, x.shape, 0) % period
    prev = jnp.where(t == 0, 0.0, pltpu.roll(x, 1, axis=0))
    nxt = jnp.where(t == period - 1, 0.0, pltpu.roll(x, n - 1, axis=0))
    return prev * w_ref[0:1, :] + x * w_ref[1:2, :] + nxt * w_ref[2:3, :] + b_ref[...]


def _hy_pre_kernel(hy_ref, cw_ref, cb_ref, bias_ref, vq_ref, e_ref, x1_ref, *, period):
    u = _conv3(hy_ref[...], cw_ref, cb_ref, period)
    x0, x1, v = u[:, :HY_WIDTH], u[:, HY_WIDTH:2 * HY_WIDTH], u[:, 2 * HY_WIDTH:]
    v = v * x0
    vq_ref[...] = v.astype(BF16)
    e_ref[...] = v * bias_ref[...]
    x1_ref[...] = x1


def _hy_pre(hy2d, conv_w, conv_b, bias, l, seq_len, batch, period):
    nt = seq_len // TL_SEQ
    blk = lambda w: pl.BlockSpec((TL_SEQ, w), lambda b, i: (i, b))
    shp = lambda dt: jax.ShapeDtypeStruct((seq_len, batch * HY_WIDTH), dt)
    return pl.pallas_call(
        functools.partial(_hy_pre_kernel, period=period),
        grid=(batch, nt),
        in_specs=[
            blk(C_HY),
            pl.BlockSpec((None, 3, C_HY), lambda b, i: (l, 0, 0)),
            pl.BlockSpec((None, 1, C_HY), lambda b, i: (l, 0, 0)),
            pl.BlockSpec((None, 1, HY_WIDTH), lambda b, i: (l, 0, 0)),
        ],
        out_specs=[blk(HY_WIDTH), blk(HY_WIDTH), blk(HY_WIDTH)],
        out_shape=[shp(BF16), shp(F32), shp(F32)],
        compiler_params=_cparams(("parallel", "parallel")),
        name="hyena_pre",
    )(hy2d, conv_w, conv_b.reshape(DEPTH, 1, C_HY), bias.reshape(DEPTH, 1, HY_WIDTH))


def _hy_filter_kernel(feat_ref, w1_ref, b1_ref, f1_ref, w2_ref, b2_ref, f2_ref, w3_ref, dl_ref,
                      hs_ref, hd_ref):
    feats = feat_ref[...]
    h = jnp.sin(f1_ref[...] * (_dot(feats, w1_ref[...], HI) + b1_ref[...]))
    h = jnp.sin(f2_ref[...] * (_dot(h, w2_ref[...], HI) + b2_ref[...]))
    h = _dot(h, w3_ref[...], HI)
    h = h * jnp.exp(-feats[:, 0:1] * dl_ref[...])
    hf, hb = h[:, :HY_WIDTH], h[:, HY_WIDTH:]
    l1 = (jnp.sum(jnp.abs(hf), axis=0, keepdims=True)
          + jnp.sum(jnp.abs(hb), axis=0, keepdims=True) + 1e-6)
    hf = hf / l1
    hb = hb / l1
    row = lax.broadcasted_iota(jnp.int32, hb.shape, 0)
    hb0 = jnp.where(row == 0, 0.0, hb)
    hs_ref[...] = (hf + hb0).astype(BF16)
    hd_ref[...] = (hb0 - hf).astype(BF16)


def _hy_filter(seq_len, w1, b1, f1, w2, b2, f2, w3):
    t = jnp.arange(seq_len, dtype=F32)
    t01 = t / max(seq_len - 1, 1)
    bands = jnp.linspace(1e-4, HY_N_BANDS - 1, HY_N_BANDS, dtype=F32)
    ang = (2.0 * math.pi / seq_len) * t[:, None] * bands[None, :]
    feats = jnp.concatenate([t01[:, None], jnp.cos(ang), -jnp.sin(ang)], axis=-1)
    feats = jnp.pad(feats, ((0, 0), (0, 128 - HY_POS_DIM)))
    w1p = jnp.pad(w1, ((0, 128 - HY_POS_DIM), (0, 0)))
    max_decay = math.log(HY_DECAY_TARGET) / HY_FAST_DECAY_PCT
    min_decay = math.log(HY_DECAY_TARGET) / HY_SLOW_DECAY_PCT
    deltas = jnp.abs(jnp.linspace(min_decay, max_decay, HY_WIDTH, dtype=F32))
    dl = jnp.concatenate([deltas, deltas])[None, :]
    hid = HY_FILTER_HIDDEN
    full = lambda shape: pl.BlockSpec(shape, lambda: tuple(0 for _ in shape))
    return pl.pallas_call(
        _hy_filter_kernel,
        in_specs=[full((seq_len, 128)), full((128, hid)), full((1, hid)), full((1, hid)),
                  full((hid, hid)), full((1, hid)), full((1, hid)), full((hid, 2 * HY_WIDTH)),
                  full((1, 2 * HY_WIDTH))],
        out_specs=[full((seq_len, HY_WIDTH)), full((seq_len, HY_WIDTH))],
        out_shape=[jax.ShapeDtypeStruct((seq_len, HY_WIDTH), BF16)] * 2,
        compiler_params=pltpu.CompilerParams(vmem_limit_bytes=V7X_VMEM_LIMIT),
        name="hyena_filter",
    )(feats, w1p, b1[None, :], f1[None, :], w2, b2[None, :], f2[None, :], w3, dl)


def _dft_matrices(seq_len):
    n2 = 2 * seq_len
    k = jnp.arange(seq_len, dtype=jnp.int32)
    prod = (k[:, None] * k[None, :]) % n2
    ang = prod.astype(F32) * (2.0 * math.pi / n2)
    cosm, sinm = jnp.cos(ang), jnp.sin(ang)
    alt = jnp.where(k % 2 == 0, 1.0, -1.0).astype(F32)
    first = (k == 0)
    fw_s = jnp.where(first[:, None], alt[None, :], sinm)
    gc = jnp.where(first[None, :], 1.0 / n2, (2.0 / n2) * cosm)
    gs = jnp.where(first[None, :], alt[:, None] / n2, (-2.0 / n2) * sinm)
    return jnp.stack([cosm, fw_s]).astype(BF16), jnp.stack([gc, gs]).astype(BF16)


def _hy_spec_kernel(hs_ref, hd_ref, f_ref, p_ref, q_ref, s_ref, *, tk):
    kre = _dot(f_ref[0], hs_ref[...])
    nyq = _dot(f_ref[1], hs_ref[...])
    kim = _dot(f_ref[1], hd_ref[...])
    row = lax.broadcasted_iota(jnp.int32, kre.shape, 0) + pl.program_id(0) * tk
    first = row == 0
    p_ref[...] = kre
    q_ref[...] = jnp.where(first, 0.0, kim)
    s_ref[...] = jnp.where(first, nyq, -kre)


def _hy_spec(hs, hd, fw, seq_len, tk):
    nf = seq_len // tk
    full = pl.BlockSpec((seq_len, HY_WIDTH), lambda f: (0, 0))
    tile = pl.BlockSpec((tk, HY_WIDTH), lambda f: (f, 0))
    return pl.pallas_call(
        functools.partial(_hy_spec_kernel, tk=tk),
        grid=(nf,),
        in_specs=[full, full, pl.BlockSpec((2, tk, seq_len), lambda f: (0, f, 0))],
        out_specs=[tile, tile, tile],
        out_shape=[jax.ShapeDtypeStruct((seq_len, HY_WIDTH), F32)] * 3,
        compiler_params=_cparams(("parallel",)),
        name="hyena_spectrum",
    )(hs, hd, fw)


def _hy_conv_kernel(v_ref, p_ref, q_ref, s_ref, f_ref, g_ref, e_ref, x1_ref, o_ref, acc_ref):
    f = pl.program_id(1)

    @pl.when(f == 0)
    def _():
        acc_ref[...] = jnp.zeros_like(acc_ref)

    v = v_ref[...]
    c = _dot(f_ref[0], v)
    s = _dot(f_ref[1], v)
    q = q_ref[...]
    yre = (c * p_ref[...] + s * q).astype(BF16)
    yim = (c * q + s * s_ref[...]).astype(BF16)
    acc_ref[...] += _dot(g_ref[0], yre) + _dot(g_ref[1], yim)

    @pl.when(f == pl.num_programs(1) - 1)
    def _():
        o_ref[...] = (acc_ref[...] + e_ref[...]) * x1_ref[...]


def _hy_conv(vq, p, q, s, fw, gi, e, x1, seq_len, batch, tk):
    nf = seq_len // tk
    seq = pl.BlockSpec((seq_len, HY_WIDTH), lambda b, f: (0, b))
    tile = pl.BlockSpec((tk, HY_WIDTH), lambda b, f: (f, 0))
    return pl.pallas_call(
        _hy_conv_kernel,
        grid=(batch, nf),
        in_specs=[seq, tile, tile, tile,
                  pl.BlockSpec((2, tk, seq_len), lambda b, f: (0, f, 0)),
                  pl.BlockSpec((2, seq_len, tk), lambda b, f: (0, 0, f)),
                  seq, seq],
        out_specs=seq,
        out_shape=jax.ShapeDtypeStruct((seq_len, batch * HY_WIDTH), F32),
        scratch_shapes=[pltpu.VMEM((seq_len, HY_WIDTH), F32)],
        compiler_params=_cparams(("parallel", "arbitrary")),
        name="hyena_longconv",
    )(vq, p, q, s, fw, gi, e, x1)


def _rw_pre_kernel(rkv_ref, lo_ref, cw_ref, cb_ref, w0_ref, wup_ref, a0_ref, aup_ref, gup_ref,
                   kk_ref, ka_ref, rk_ref, hs_ref,
                   r_out, v_out, kk_out, bonus_out, g_out, lw_out, lwt_out, bbt_out, ktt_out,
                   *, period):
    u = _conv3(rkv_ref[...], cw_ref, cb_ref, period)
    r, k, v = u[:, :RW_WIDTH], u[:, RW_WIDTH:2 * RW_WIDTH], u[:, 2 * RW_WIDTH:]
    lo = lo_ref[...]
    wraw = w0_ref[...] + _dot(jnp.tanh(lo[:, 0:128]), wup_ref[...], HI)
    z = -wraw
    softplus = jnp.maximum(z, 0.0) + jnp.log(1.0 + jnp.exp(-jnp.abs(z)))
    logw = -jnp.exp(-softplus - 0.5)
    a = jax.nn.sigmoid(a0_ref[...] + _dot(lo[:, 128:256], aup_ref[...], HI))
    g_out[...] = _dot(jax.nn.sigmoid(lo[:, 256:384]), gup_ref[...], HI)
    head_sum = hs_ref[...]
    kk = k * kk_ref[...]
    kk = kk * lax.rsqrt(jnp.maximum(_dot(kk * kk, head_sum, HI), 1e-24))
    r_out[...] = r
    v_out[...] = v
    kk_out[...] = kk
    n_chunk = r.shape[0] // RW_CHUNK
    kt_sum = jnp.zeros_like(k)
    for d in range(2):
        a_d = a[:, d * RW_WIDTH:(d + 1) * RW_WIDTH]
        lw_d = logw[:, d * RW_WIDTH:(d + 1) * RW_WIDTH]
        kt_d = k * (1.0 + (a_d - 1.0) * ka_ref[...])
        kt_sum = kt_sum + kt_d
        lw_out[d] = lw_d
        lwt, bbt, ktt = lw_d.T, (kk * a_d).T, kt_d.T
        for j in range(n_chunk):
            sl = slice(j * RW_CHUNK, (j + 1) * RW_CHUNK)
            lwt_out[d, j] = lwt[:, sl]
            bbt_out[d, j] = bbt[:, sl]
            ktt_out[d, j] = ktt[:, sl]
    bonus_out[...] = _dot(r * kt_sum * rk_ref[...], head_sum, HI) * v


def _rw_pre(rkv2d, lora2d, P, l, seq_len, batch, period):
    nt = seq_len // TL_SEQ
    cpt = TL_SEQ // RW_CHUNK
    nchunk = seq_len // RW_CHUNK
    blk = lambda w: pl.BlockSpec((TL_SEQ, w), lambda b, i: (i, b))
    par = lambda shape: pl.BlockSpec((None,) + shape, lambda b, i: (l,) + tuple(0 for _ in shape))
    row_shape = jax.ShapeDtypeStruct((seq_len, batch * RW_WIDTH), F32)
    dir_shape = jax.ShapeDtypeStruct((2, seq_len, batch * RW_WIDTH), F32)
    tr_shape = jax.ShapeDtypeStruct((2, batch, nchunk, RW_WIDTH, RW_CHUNK), F32)
    tr_blk = pl.BlockSpec((2, None, cpt, RW_WIDTH, RW_CHUNK), lambda b, i: (0, b, i, 0, 0))
    return pl.pallas_call(
        functools.partial(_rw_pre_kernel, period=period),
        grid=(batch, nt),
        in_specs=[
            blk(C_RKV), blk(C_LORA),
            par((3, C_RKV)), par((1, C_RKV)),
            par((1, 2 * RW_WIDTH)), par((128, 2 * RW_WIDTH)),
            par((1, 2 * RW_WIDTH)), par((128, 2 * RW_WIDTH)),
            par((128, RW_WIDTH)),
            par((1, RW_WIDTH)), par((1, RW_WIDTH)), par((1, RW_WIDTH)),
            pl.BlockSpec((RW_WIDTH, RW_WIDTH), lambda b, i: (0, 0)),
        ],
        out_specs=[blk(RW_WIDTH)] * 5
        + [pl.BlockSpec((2, TL_SEQ, RW_WIDTH), lambda b, i: (0, i, b)), tr_blk, tr_blk, tr_blk],
        out_shape=[row_shape] * 5 + [dir_shape, tr_shape, tr_shape, tr_shape],
        compiler_params=_cparams(("parallel", "parallel")),
        name="rwkv_pre",
    )(rkv2d, lora2d, P["rw_conv_w"], P["rw_conv_b3"], P["rw_w0c"], P["rw_wupc"], P["rw_a0c"],
      P["rw_aupc"], P["rw_g_up"], P["rw_k_k3"], P["rw_k_a3"], P["rw_r_k3"], P["head_sum"])


def _rw_scan_kernel(r_ref, v_ref, kk_ref, lw_ref, lwt_ref, bbt_ref, ktt_ref, s0_ref,
                    o_ref, sfin_ref, st_ref):
    d = pl.program_id(0)
    c = pl.program_id(2)
    T = RW_CHUNK
    N = RW_HEAD_DIM

    @pl.when(c == 0)
    def _():
        st_ref[...] = s0_ref[...]

    ti = lax.broadcasted_iota(jnp.int32, (T, T), 0)
    tj = lax.broadcasted_iota(jnp.int32, (T, T), 1)
    sgn = 1 - 2 * d
    before = (tj - ti) * sgn < 0
    before_eq = (tj - ti) * sgn <= 0
    incl = before_eq.astype(F32)
    incl_t = ((ti - tj) * sgn <= 0).astype(F32)
    eye = (ti == tj).astype(F32)

    lw = lw_ref[...]
    cum = _dot(incl, lw, HI)
    w_in = jnp.exp(cum)
    w_ex = jnp.exp(cum - lw)
    lwt = lwt_ref[...]
    cum_t = _dot(lwt, incl_t, HI)
    w_inv_t = jnp.exp(-cum_t)
    w_tot_t = jnp.exp(jnp.sum(lwt, axis=1, keepdims=True))
    a_til = -kk_ref[...] * w_ex
    r_hat = r_ref[...] * w_in
    b_hat_t = bbt_ref[...] * w_inv_t
    k_hat_t = ktt_ref[...] * w_inv_t
    v = v_ref[...]

    for h in range(RW_HEADS):
        cs = slice(h * N, (h + 1) * N)
        a_h, r_h, v_h = a_til[:, cs], r_hat[:, cs], v[:, cs]
        bt_h, kt_h = b_hat_t[cs, :], k_hat_t[cs, :]
        l_ab = jnp.where(before, _dot(a_h, bt_h, HI), 0.0)
        l_ak = jnp.where(before, _dot(a_h, kt_h, HI), 0.0)
        g_rb = jnp.where(before_eq, _dot(r_h, bt_h, HI), 0.0)
        g_rk = jnp.where(before_eq, _dot(r_h, kt_h, HI), 0.0)
        m = eye + l_ab
        lp = l_ab
        for _ in range(int(math.log2(T)) - 1):
            lp = _dot(lp, lp, HI)
            m = m + _dot(m, lp, HI)
        p = _dot(m, a_h, HI)
        q = _dot(m, _dot(l_ak, v_h, HI), HI)
        st = st_ref[h]
        u = _dot(p, st, HI) + q
        o_ref[:, cs] = _dot(r_h, st, HI) + _dot(g_rb, u, HI) + _dot(g_rk, v_h, HI)
        st_ref[h] = (st + _dot(bt_h, u, HI) + _dot(kt_h, v_h, HI)) * w_tot_t[cs, :]

    @pl.when(c == pl.num_programs(2) - 1)
    def _():
        sfin_ref[...] = st_ref[...]


def _rw_scan(r, v, kk, lw, lwt, bbt, ktt, s0t, seq_len, batch):
    nc = seq_len // RW_CHUNK
    chunk = lambda d, c: c + d * (nc - 1 - 2 * c)
    row = pl.BlockSpec((RW_CHUNK, RW_WIDTH), lambda d, b, c: (chunk(d, c), b))
    drow = pl.BlockSpec((None, RW_CHUNK, RW_WIDTH), lambda d, b, c: (d, chunk(d, c), b))
    tr = pl.BlockSpec((None, None, None, RW_WIDTH, RW_CHUNK), lambda d, b, c: (d, b, chunk(d, c), 0, 0))
    st = pl.BlockSpec((None, None, RW_HEADS, RW_HEAD_DIM, RW_HEAD_DIM), lambda d, b, c: (b, d, 0, 0, 0))
    return pl.pallas_call(
        _rw_scan_kernel,
        grid=(2, batch, nc),
        in_specs=[row, row, row, drow, tr, tr, tr, st],
        out_specs=[drow, st],
        out_shape=[jax.ShapeDtypeStruct((2, seq_len, batch * RW_WIDTH), F32),
                   jax.ShapeDtypeStruct((batch, 2, RW_HEADS, RW_HEAD_DIM, RW_HEAD_DIM), F32)],
        scratch_shapes=[pltpu.VMEM((RW_HEADS, RW_HEAD_DIM, RW_HEAD_DIM), F32)],
        compiler_params=_cparams(("parallel", "parallel", "arbitrary")),
        name="rwkv_scan",
    )(r, v, kk, lw, lwt, bbt, ktt, s0t)


def _s5_kernel(u_ref, wb_ref, wc_ref, a_ref, x0_ref, y_ref, fin_ref, bu_ref, st_ref, *, batch, steps):
    d = pl.program_id(0)
    i = pl.program_id(1)
    ns = S5_NS

    @pl.when(i == 0)
    def _():
        st_ref[...] = x0_ref[...]

    bu_ref[...] = _dot(u_ref[...].astype(BF16), wb_ref[...])

    if batch % 8 == 0:
        lanes = 256
        for lc in range(ns // lanes):
            re_sl = slice(lc * lanes, (lc + 1) * lanes)
            im_sl = slice(ns + lc * lanes, ns + (lc + 1) * lanes)
            ar, ai = a_ref[:, re_sl], a_ref[:, im_sl]

            def body(s, carry, re_sl=re_sl, im_sl=im_sl, ar=ar, ai=ai):
                xr, xi = carry
                step = s + d * (steps - 1 - 2 * s)
                rows = pl.ds(pl.multiple_of(step * batch, 8), batch)
                nr = ar * xr - ai * xi + bu_ref[rows, re_sl]
                ni = ar * xi + ai * xr + bu_ref[rows, im_sl]
                bu_ref[rows, re_sl] = nr
                bu_ref[rows, im_sl] = ni
                return nr, ni

            xr, xi = lax.fori_loop(0, steps, body, (st_ref[:, re_sl], st_ref[:, im_sl]))
            st_ref[:, re_sl] = xr
            st_ref[:, im_sl] = xi
    else:
        ar, ai = a_ref[:, :ns], a_ref[:, ns:]
        half = lax.broadcasted_iota(jnp.int32, (8, ns), 0) // 4
        first = half == d

        def body(s, carry):
            xr, xi = carry
            pair = s + d * (steps // 2 - 1 - 2 * s)
            rows = pl.ds(pl.multiple_of(pair * 8, 8), 8)
            br, bi = bu_ref[rows, :ns], bu_ref[rows, ns:]
            r1 = ar * xr - ai * xi + br
            i1 = ar * xi + ai * xr + bi
            r1s, i1s = pltpu.roll(r1, 4, axis=0), pltpu.roll(i1, 4, axis=0)
            r2 = ar * r1s - ai * i1s + br
            i2 = ar * i1s + ai * r1s + bi
            bu_ref[rows, :ns] = jnp.where(first, r1, r2)
            bu_ref[rows, ns:] = jnp.where(first, i1, i2)
            return pltpu.roll(r2, 4, axis=0), pltpu.roll(i2, 4, axis=0)

        xr, xi = lax.fori_loop(0, steps // 2, body, (st_ref[:, :ns], st_ref[:, ns:]))
        st_ref[:, :ns] = xr
        st_ref[:, ns:] = xi

    y_ref[...] = _dot(bu_ref[...].astype(BF16), wc_ref[...])

    @pl.when(i == pl.num_programs(1) - 1)
    def _():
        fin_ref[...] = st_ref[...]


def _s5_scan(u, wb, wc, a_bar, x0, l, seq_len, batch):
    rows_blk = 512
    steps = rows_blk // batch
    nblk = seq_len * batch // rows_blk
    rows = max(batch, 8)
    blk = lambda d, i: i + d * (nblk - 1 - 2 * i)
    return pl.pallas_call(
        functools.partial(_s5_kernel, batch=batch, steps=steps),
        grid=(2, nblk),
        in_specs=[
            pl.BlockSpec((rows_blk, S5_WIDTH), lambda d, i: (blk(d, i), 0)),
            pl.BlockSpec((None, None, S5_WIDTH, 2 * S5_NS), lambda d, i: (l, d, 0, 0)),
            pl.BlockSpec((None, None, 2 * S5_NS, S5_WIDTH), lambda d, i: (l, d, 0, 0)),
            pl.BlockSpec((None, None, 1, 2 * S5_NS), lambda d, i: (l, d, 0, 0)),
            pl.BlockSpec((None, rows, 2 * S5_NS), lambda d, i: (d, 0, 0)),
        ],
        out_specs=[
            pl.BlockSpec((None, rows_blk, S5_WIDTH), lambda d, i: (d, blk(d, i), 0)),
            pl.BlockSpec((None, rows, 2 * S5_NS), lambda d, i: (d, 0, 0)),
        ],
        out_shape=[jax.ShapeDtypeStruct((2, seq_len * batch, S5_WIDTH), F32),
                   jax.ShapeDtypeStruct((2, rows, 2 * S5_NS), F32)],
        scratch_shapes=[pltpu.VMEM((rows_blk, 2 * S5_NS), F32), pltpu.VMEM((rows, 2 * S5_NS), F32)],
        compiler_params=_cparams(("parallel", "arbitrary")),
        name="s5_scan",
    )(u, wb, wc, a_bar, x0)


def _gelu_tanh(x):
    return 0.5 * x * (1.0 + jnp.tanh(math.sqrt(2.0 / math.pi) * (x + 0.044715 * x * x * x)))


def _merge_kernel(x_ref, gate_ref, yhy_ref, of_ref, ob_ref, bonus_ref, g_ref, u5_ref, y5f_ref, y5b_ref,
                  gt1_ref, sh2_ref, sc2_ref,
                  hm_ref, gnw_ref, gnb_ref, s5d_ref, gluw_ref, glub_ref,
                  wbhy_ref, wbrw_ref, wbs5_ref, wout_ref, n2g_ref, rw_ref, rb_ref,
                  h_ref, hn_ref, gates_ref):
    o = of_ref[...] + ob_ref[...]
    head_mean = hm_ref[...]
    mu = _dot(o, head_mean, HI)
    oc = o - mu
    var = _dot(oc * oc, head_mean, HI)
    y_rw = (oc * lax.rsqrt(var + RW_GN_EPS) * gnw_ref[...] + gnb_ref[...] + bonus_ref[...]) * g_ref[...]
    y5 = _gelu_tanh(u5_ref[...] * s5d_ref[...] + y5f_ref[...] + y5b_ref[...])
    y5 = y5 * jax.nn.sigmoid(_dot(y5.astype(BF16), gluw_ref[...]) + glub_ref[...])
    gate = jax.nn.sigmoid(gate_ref[...])
    merged = (gate[:, :D_MODEL] * _dot(yhy_ref[...].astype(BF16), wbhy_ref[...])
              + gate[:, D_MODEL:2 * D_MODEL] * _dot(y_rw.astype(BF16), wbrw_ref[...])
              + gate[:, 2 * D_MODEL:] * _dot(y5.astype(BF16), wbs5_ref[...]))
    h = x_ref[...] + gt1_ref[...] * _dot(merged.astype(BF16), wout_ref[...])
    h_ref[...] = h
    hn = h * lax.rsqrt(jnp.mean(h * h, axis=-1, keepdims=True) + NORM_EPS) * n2g_ref[...]
    hn = hn * (1.0 + sc2_ref[...]) + sh2_ref[...]
    hn_ref[...] = hn.astype(BF16)

    scores = jax.nn.sigmoid(_dot(hn, rw_ref[...], HI))
    sel = scores + rb_ref[...]
    s_col = [sel[:, e:e + 1] for e in range(N_EXPERTS)]
    best_val = None
    best_grp = None
    for grp in range(N_EXPERT_GROUPS):
        a, b, c, dd = s_col[4 * grp:4 * grp + 4]
        hi1, lo1 = jnp.maximum(a, b), jnp.minimum(a, b)
        hi2, lo2 = jnp.maximum(c, dd), jnp.minimum(c, dd)
        m1 = jnp.maximum(hi1, hi2)
        m2 = jnp.maximum(jnp.minimum(hi1, hi2), jnp.where(hi1 >= hi2, lo1, lo2))
        val = m1 + m2
        if grp == 0:
            best_val, best_grp = val, jnp.zeros_like(val, dtype=jnp.int32)
        else:
            better = val > best_val
            best_val = jnp.where(better, val, best_val)
            best_grp = jnp.where(better, grp, best_grp)
    picked = []
    for e in range(N_EXPERTS):
        grp, pos = divmod(e, EXPERTS_PER_GROUP)
        rank = jnp.zeros_like(best_grp)
        for other in range(EXPERTS_PER_GROUP):
            if other == pos:
                continue
            so = s_col[4 * grp + other]
            ahead = (so > s_col[e]) if other > pos else (so >= s_col[e])
            rank = rank + ahead.astype(jnp.int32)
        take = jnp.logical_and(rank < 2, best_grp == grp)
        picked.append(jnp.where(take, scores[:, e:e + 1], 0.0))
    total = picked[0]
    for e in range(1, N_EXPERTS):
        total = total + picked[e]
    lane = lax.broadcasted_iota(jnp.int32, gates_ref.shape, 1)
    gates = jnp.zeros(gates_ref.shape, F32)
    for e in range(N_EXPERTS):
        gates = jnp.where(lane == e, picked[e] / total, gates)
    gates_ref[...] = gates


def _merge(x, gates_in, y_hy, o_f, o_b, bonus, g, u5, y5f, y5b, gt1, sh2, sc2, P, l, tiles_per_stream):
    n_tok = x.shape[0]
    tm = TM_TOK
    tok = lambda w: pl.BlockSpec((tm, w), lambda i: (i, 0))
    stream = pl.BlockSpec((None, tm, D_MODEL), lambda i: (i // tiles_per_stream, 0, 0))
    par = lambda shape: pl.BlockSpec((None,) + shape, lambda i: (l,) + tuple(0 for _ in shape))
    shared = lambda shape: pl.BlockSpec(shape, lambda i: tuple(0 for _ in shape))
    return pl.pallas_call(
        _merge_kernel,
        grid=(n_tok // tm,),
        in_specs=[
            tok(D_MODEL), tok(C_GATE), tok(HY_WIDTH), tok(RW_WIDTH), tok(RW_WIDTH), tok(RW_WIDTH),
            tok(RW_WIDTH), tok(S5_WIDTH), tok(S5_WIDTH), tok(S5_WIDTH),
            stream, stream, stream,
            shared((RW_WIDTH, RW_WIDTH)), par((1, RW_WIDTH)), par((1, RW_WIDTH)),
            par((1, S5_WIDTH)), par((S5_WIDTH, S5_WIDTH)), par((1, S5_WIDTH)),
            par((HY_WIDTH, D_MODEL)), par((RW_WIDTH, D_MODEL)), par((S5_WIDTH, D_MODEL)),
            par((D_MODEL, D_MODEL)), par((1, D_MODEL)),
            shared((D_MODEL, 128)), shared((1, 128)),
        ],
        out_specs=[tok(D_MODEL), tok(D_MODEL), tok(128)],
        out_shape=[jax.ShapeDtypeStruct((n_tok, D_MODEL), F32),
                   jax.ShapeDtypeStruct((n_tok, D_MODEL), BF16),
                   jax.ShapeDtypeStruct((n_tok, 128), F32)],
        compiler_params=_cparams(("parallel",)),
        name="merge_router",
    )(x, gates_in, y_hy, o_f, o_b, bonus, g, u5, y5f, y5b, gt1, sh2, sc2,
      P["head_mean"], P["rw_gn_w3"], P["rw_gn_b3"], P["s5_d3"], P["s5_glu_w_bf"], P["s5_glu_b3"],
      P["wb_hy_bf"], P["wb_rw_bf"], P["wb_s5_bf"], P["w_out_bf"], P["norm2_g3"],
      P["router_w_pad"], P["router_b_pad"])


def _moe_kernel(hn_ref, gates_ref, h_ref, gt2_ref, wg_ref, wu_ref, wd_ref, fg_ref, x_ref, y_ref, acc_ref,
                *, final):
    e = pl.program_id(1)

    @pl.when(e == 0)
    def _():
        acc_ref[...] = jnp.zeros_like(acc_ref)

    hn = hn_ref[...]
    a = _dot(hn, wg_ref[...])
    he = (a * jax.nn.sigmoid(a)) * _dot(hn, wu_ref[...])
    lane = lax.broadcasted_iota(jnp.int32, gates_ref.shape, 1)
    gate = jnp.sum(jnp.where(lane == e, gates_ref[...], 0.0), axis=1, keepdims=True)
    acc_ref[...] += gate * _dot(he.astype(BF16), wd_ref[...])

    @pl.when(e == pl.num_programs(1) - 1)
    def _():
        x = h_ref[...] + gt2_ref[...] * acc_ref[...]
        x_ref[...] = x
        if final:
            y_ref[...] = x * lax.rsqrt(jnp.mean(x * x, axis=-1, keepdims=True) + NORM_EPS) * fg_ref[...]
        else:
            y_ref[...] = jnp.zeros_like(y_ref)


def _moe(hn, gates, h, gt2, wg, wu, wd, final_g, l, tiles_per_stream, final):
    n_tok = hn.shape[0]
    tm = 1024
    per_stream = tiles_per_stream * TM_TOK // tm
    tok = lambda w: pl.BlockSpec((tm, w), lambda i, e: (i, 0))
    y_shape = (n_tok, D_MODEL) if final else (8, D_MODEL)
    y_spec = tok(D_MODEL) if final else pl.BlockSpec((8, D_MODEL), lambda i, e: (0, 0))
    return pl.pallas_call(
        functools.partial(_moe_kernel, final=final),
        grid=(n_tok // tm, N_EXPERTS),
        in_specs=[
            tok(D_MODEL), tok(128), tok(D_MODEL),
            pl.BlockSpec((None, tm, D_MODEL), lambda i, e: (i // per_stream, 0, 0)),
            pl.BlockSpec((None, None, D_MODEL, EXPERT_FF), lambda i, e: (l, e, 0, 0)),
            pl.BlockSpec((None, None, D_MODEL, EXPERT_FF), lambda i, e: (l, e, 0, 0)),
            pl.BlockSpec((None, None, EXPERT_FF, D_MODEL), lambda i, e: (l, e, 0, 0)),
            pl.BlockSpec((1, D_MODEL), lambda i, e: (0, 0)),
        ],
        out_specs=[tok(D_MODEL), y_spec],
        out_shape=[jax.ShapeDtypeStruct((n_tok, D_MODEL), F32), jax.ShapeDtypeStruct(y_shape, F32)],
        scratch_shapes=[pltpu.VMEM((tm, D_MODEL), F32)],
        compiler_params=_cparams(("parallel", "arbitrary")),
        name="moe",
    )(hn, gates, h, gt2, wg, wu, wd, final_g.reshape(1, D_MODEL))


def _block_diag(blocks):
    g, r, c = blocks.shape[-3:]
    eye = jnp.eye(g, dtype=blocks.dtype)
    out = blocks[..., :, :, None, :] * eye[:, None, :, None]
    return out.reshape(blocks.shape[:-3] + (g * r, g * c))


def _s5_params(lam_re, lam_im, log_dt, b_re, b_im, c_re, c_im):
    lr, li = lam_re.astype(F32), lam_im.astype(F32)
    dt = jnp.exp(log_dt.astype(F32))[..., None]
    mag = jnp.exp(lr * dt)
    ar, ai = mag * jnp.cos(li * dt), mag * jnp.sin(li * dt)
    den = lr * lr + li * li
    qr = ((ar - 1.0) * lr + ai * li) / den
    qi = (ai * lr - (ar - 1.0) * li) / den
    bbr = qr[..., None] * b_re - qi[..., None] * b_im
    bbi = qr[..., None] * b_im + qi[..., None] * b_re
    tr = lambda m: jnp.swapaxes(m, -1, -2)
    wb = jnp.concatenate([_block_diag(tr(bbr)), _block_diag(tr(bbi))], axis=-1)
    wc = jnp.concatenate([_block_diag(tr(c_re)), -_block_diag(tr(c_im))], axis=-2)
    a_bar = jnp.concatenate([ar.reshape(DEPTH, 2, 1, S5_NS), ai.reshape(DEPTH, 2, 1, S5_NS)], axis=-1)
    return wb.astype(BF16), wc.astype(BF16), a_bar


def _stream_rows(m_ctx, m_lat, tm):
    ctx = jnp.broadcast_to(m_ctx[None, :], (tm, m_ctx.shape[-1]))
    lat = jnp.tile(m_lat, (tm // m_lat.shape[0], 1))
    return jnp.stack([ctx, lat])


def kernel(x_prompt, x_sample, state_rwkv, state_s5_re, state_s5_im, c, c_ctx, ada_w, ada_b, norm1_g, norm2_g, final_g, w_in, hy_conv_w, hy_conv_b, hy_f_w1, hy_f_b1, hy_f_freq1, hy_f_w2, hy_f_b2, hy_f_freq2, hy_f_w3, hy_bias, rw_conv_w, rw_conv_b, rw_w0, rw_w_up, rw_a0, rw_a_up, rw_g_up, rw_k_k, rw_k_a, rw_r_k, rw_gn_w, rw_gn_b, s5_lam_re, s5_lam_im, s5_log_dt, s5_b_re, s5_b_im, s5_c_re, s5_c_im, s5_d, s5_glu_w, s5_glu_b, wb_hy, wb_rw, wb_s5, w_out, router_w, router_b, exp_wg, exp_wu, exp_wd):
    bc, lc = x_prompt.shape[0], x_prompt.shape[1]
    bl, ll = x_sample.shape[0], x_sample.shape[1]
    n_ctx, n_lat = bc * lc, bl * ll
    assert n_ctx == n_lat and n_ctx % 1024 == 0 and bl == 4 and bc % 8 == 0
    tiles_per_stream = n_ctx // TM_TOK
    streams = ((0, lc, bc, lc), (n_ctx, ll, bl, GRID_W))

    head_id = np.arange(RW_WIDTH) // RW_HEAD_DIM
    head_sum = (head_id[:, None] == head_id[None, :]).astype(np.float32)
    P = dict(
        rw_conv_w=rw_conv_w, rw_conv_b3=rw_conv_b.reshape(DEPTH, 1, C_RKV),
        rw_w0c=rw_w0.reshape(DEPTH, 1, 2 * RW_WIDTH),
        rw_wupc=_block_diag(rw_w_up), rw_a0c=rw_a0.reshape(DEPTH, 1, 2 * RW_WIDTH),
        rw_aupc=_block_diag(rw_a_up), rw_g_up=rw_g_up,
        rw_k_k3=rw_k_k.reshape(DEPTH, 1, RW_WIDTH), rw_k_a3=rw_k_a.reshape(DEPTH, 1, RW_WIDTH),
        rw_r_k3=rw_r_k.reshape(DEPTH, 1, RW_WIDTH),
        head_sum=jnp.asarray(head_sum), head_mean=jnp.asarray(head_sum / RW_HEAD_DIM),
        rw_gn_w3=rw_gn_w.reshape(DEPTH, 1, RW_WIDTH), rw_gn_b3=rw_gn_b.reshape(DEPTH, 1, RW_WIDTH),
        s5_d3=s5_d.reshape(DEPTH, 1, S5_WIDTH), s5_glu_w_bf=s5_glu_w.astype(BF16),
        s5_glu_b3=s5_glu_b.reshape(DEPTH, 1, S5_WIDTH),
        wb_hy_bf=wb_hy.astype(BF16), wb_rw_bf=wb_rw.astype(BF16), wb_s5_bf=wb_s5.astype(BF16),
        w_out_bf=w_out.astype(BF16), norm2_g3=norm2_g.reshape(DEPTH, 1, D_MODEL),
        router_w_pad=jnp.pad(router_w, ((0, 0), (0, 128 - N_EXPERTS))),
        router_b_pad=jnp.pad(router_b, (0, 128 - N_EXPERTS)).reshape(1, 128),
    )
    w_in_bf = w_in.astype(BF16)
    wg_bf, wu_bf, wd_bf = exp_wg.astype(BF16), exp_wu.astype(BF16), exp_wd.astype(BF16)
    s5_wb, s5_wc, s5_abar = _s5_params(s5_lam_re, s5_lam_im, s5_log_dt, s5_b_re, s5_b_im, s5_c_re, s5_c_im)
    dft = {seq_len: _dft_matrices(seq_len) for seq_len in (lc, ll)}

    x = jnp.concatenate([
        jnp.swapaxes(x_prompt.astype(F32), 0, 1).reshape(n_ctx, D_MODEL),
        jnp.swapaxes(x_sample.astype(F32), 0, 1).reshape(n_lat, D_MODEL)])
    cond8 = jnp.zeros((8, D_MODEL), F32).at[:bl].set(c.astype(F32)).at[bl].set(c_ctx.astype(F32))

    rw_s0 = (jnp.zeros((bc, DEPTH, 2, RW_HEADS, RW_HEAD_DIM, RW_HEAD_DIM), F32),
             jnp.swapaxes(state_rwkv.astype(F32), -1, -2))
    s5_lat = jnp.concatenate([state_s5_re.reshape(bl, DEPTH, 2, S5_NS),
                              state_s5_im.reshape(bl, DEPTH, 2, S5_NS)], axis=-1).astype(F32)
    s5_lat = jnp.tile(jnp.moveaxis(s5_lat, 0, 2), (1, 1, 2, 1))
    s5_x0 = (jnp.zeros((DEPTH, 2, bc, 2 * S5_NS), F32), s5_lat)

    rw_new, s5_new = [], []
    y_final = None
    for l in range(DEPTH):
        m = _modulation(cond8, ada_w, ada_b, l)
        parts = [_stream_rows(m[bl, j * D_MODEL:(j + 1) * D_MODEL], m[:bl, j * D_MODEL:(j + 1) * D_MODEL], 1024)
                 for j in range(6)]
        sh1, sc1, gt1, sh2, sc2, gt2 = parts
        tmr = lambda a: a[:, :TM_TOK]
        g_in, hy_in, rkv_in, lo_in, s5_in = _inproj(x, tmr(sh1), tmr(sc1), norm1_g, w_in_bf, l, tiles_per_stream)

        y_hy, o_dir, bonus, g_rw, y5 = [], [], [], [], []
        for si, (off, seq_len, batch, period) in enumerate(streams):
            rows = slice(off, off + seq_len * batch)
            view = lambda a, w: a[rows].reshape(seq_len, batch * w)
            vq, e, x1 = _hy_pre(view(hy_in, C_HY), hy_conv_w, hy_conv_b, hy_bias, l, seq_len, batch, period)
            hs, hd = _hy_filter(seq_len, hy_f_w1[l], hy_f_b1[l], hy_f_freq1[l], hy_f_w2[l], hy_f_b2[l],
                                hy_f_freq2[l], hy_f_w3[l])
            fw, gi = dft[seq_len]
            tk = min(seq_len, 256)
            sp, sq, ss = _hy_spec(hs, hd, fw, seq_len, tk)
            y = _hy_conv(vq, sp, sq, ss, fw, gi, e, x1, seq_len, batch, tk)
            y_hy.append(y.reshape(seq_len * batch, HY_WIDTH))
            r, v, kk, bon, g, lw, lwt, bbt, ktt = _rw_pre(view(rkv_in, C_RKV), view(lo_in, C_LORA), P, l,
                                                          seq_len, batch, period)
            o, s_fin = _rw_scan(r, v, kk, lw, lwt, bbt, ktt, rw_s0[si][:, l], seq_len, batch)
            o_dir.append(o.reshape(2, seq_len * batch, RW_WIDTH))
            bonus.append(bon.reshape(seq_len * batch, RW_WIDTH))
            g_rw.append(g.reshape(seq_len * batch, RW_WIDTH))
            y5d, fin5 = _s5_scan(s5_in[rows], s5_wb, s5_wc, s5_abar, s5_x0[si][l], l, seq_len, batch)
            y5.append(y5d)
            if si == 0:
                rw_new.append(jnp.swapaxes(s_fin, -1, -2))
                s5_new.append(fin5)
        cat = lambda parts, axis=0: jnp.concatenate(parts, axis=axis)
        o_all = cat(o_dir, axis=1)
        y5_all = cat(y5, axis=1)
        h, hn, gates = _merge(x, g_in, cat(y_hy), o_all[0], o_all[1], cat(bonus), cat(g_rw), s5_in,
                              y5_all[0], y5_all[1], tmr(gt1), tmr(sh2), tmr(sc2), P, l, tiles_per_stream)
        x, y_final = _moe(hn, gates, h, gt2, wg_bf, wu_bf, wd_bf, final_g, l, tiles_per_stream,
                          final=(l == DEPTH - 1))

    y_prompt = jnp.swapaxes(y_final[:n_ctx].reshape(lc, bc, D_MODEL), 0, 1).astype(x_prompt.dtype)
    y_sample = jnp.swapaxes(y_final[n_ctx:].reshape(ll, bl, D_MODEL), 0, 1).astype(x_sample.dtype)
    new_state_rwkv = jnp.stack(rw_new, axis=1)
    s5_fin = jnp.stack(s5_new, axis=0)
    s5_fin = jnp.moveaxis(s5_fin, 2, 0)
    new_re = s5_fin[..., :S5_NS].reshape(bc, DEPTH, 2, S5_GROUPS, S5_STATE)
    new_im = s5_fin[..., S5_NS:].reshape(bc, DEPTH, 2, S5_GROUPS, S5_STATE)
    return (y_prompt, y_sample, new_state_rwkv, new_re, new_im)
```

```python
import functools
import math

import jax
import jax.numpy as jnp
import numpy as np
from jax import lax
from jax.experimental import pallas as pl
from jax.experimental.pallas import tpu as pltpu

F32 = jnp.float32
BF16 = jnp.bfloat16
HI = lax.Precision.HIGHEST

D_MODEL = 1024
DEPTH = 2
GRID_W = 64
NORM_EPS = 1e-6

HY_WIDTH = 384
HY_FILTER_HIDDEN = 64
HY_N_BANDS = 8
HY_POS_DIM = 1 + 2 * HY_N_BANDS
HY_FAST_DECAY_PCT = 0.3
HY_SLOW_DECAY_PCT = 1.5
HY_DECAY_TARGET = 1e-2

RW_HEAD_DIM = 64
RW_HEADS = 6
RW_WIDTH = RW_HEADS * RW_HEAD_DIM
RW_GN_EPS = 64e-5
RW_CHUNK = 64

S5_GROUPS = 16
S5_GROUP_CH = 16
S5_WIDTH = S5_GROUPS * S5_GROUP_CH
S5_STATE = 64
S5_NS = S5_GROUPS * S5_STATE

N_EXPERTS = 16
N_EXPERT_GROUPS = 4
EXPERTS_PER_GROUP = N_EXPERTS // N_EXPERT_GROUPS
EXPERT_FF = 512

C_GATE = 3 * D_MODEL
C_HY = 3 * HY_WIDTH
C_RKV = 3 * RW_WIDTH
C_LORA = 384
C_S5 = S5_WIDTH
IN_COLS = C_GATE + C_HY + C_RKV + C_LORA + C_S5

V7X_VMEM_LIMIT = 56 * 1024 * 1024

TM_TOK = 256
TL_SEQ = 256


def _cparams(sem):
    return pltpu.CompilerParams(dimension_semantics=sem, vmem_limit_bytes=V7X_VMEM_LIMIT)


def _dot(a, b, precision=None):
    return jnp.dot(a, b, preferred_element_type=F32, precision=precision)


def _mod_kernel(c_ref, w_ref, b_ref, o_ref):
    c = c_ref[...]
    s = c * jax.nn.sigmoid(c)
    o_ref[...] = _dot(s, w_ref[...], HI) + b_ref[...]


def _modulation(cond8, ada_w, ada_b, l):
    tn = 1536
    return pl.pallas_call(
        _mod_kernel,
        grid=(6 * D_MODEL // tn,),
        in_specs=[
            pl.BlockSpec((8, D_MODEL), lambda j: (0, 0)),
            pl.BlockSpec((None, D_MODEL, tn), lambda j: (l, 0, j)),
            pl.BlockSpec((None, 1, tn), lambda j: (l, 0, j)),
        ],
        out_specs=pl.BlockSpec((8, tn), lambda j: (0, j)),
        out_shape=jax.ShapeDtypeStruct((8, 6 * D_MODEL), F32),
        compiler_params=_cparams(("arbitrary",)),
        name="modulation",
    )(cond8, ada_w, ada_b.reshape(DEPTH, 1, 6 * D_MODEL))


def _inproj_kernel(x_ref, sh_ref, sc_ref, g_ref, w_ref, og, ohy, orkv, olo, os5):
    x = x_ref[...]
    xn = x * lax.rsqrt(jnp.mean(x * x, axis=-1, keepdims=True) + NORM_EPS) * g_ref[...]
    xn = (xn * (1.0 + sc_ref[...]) + sh_ref[...]).astype(BF16)
    off = 0
    for o_ref, width in ((og, C_GATE), (ohy, C_HY), (orkv, C_RKV), (olo, C_LORA), (os5, C_S5)):
        o_ref[...] = _dot(xn, w_ref[:, off:off + width])
        off += width


def _mod_spec(which, tm, seq_len, n_mod):
    if n_mod == 1:
        return pl.BlockSpec((None, None, 1, D_MODEL), lambda i, *_: (0, which, 0, 0))
    return pl.BlockSpec((None, None, 1, D_MODEL), lambda i, *_: (i * tm // seq_len, which, 0, 0))


def _inproj(x, mod, norm_g, w_in_bf, l, seq_len):
    n_tok = x.shape[0]
    tm = TM_TOK
    widths = (C_GATE, C_HY, C_RKV, C_LORA, C_S5)
    return pl.pallas_call(
        _inproj_kernel,
        grid=(n_tok // tm,),
        in_specs=[
            pl.BlockSpec((tm, D_MODEL), lambda i: (i, 0)),
            _mod_spec(0, tm, seq_len, mod.shape[0]),
            _mod_spec(1, tm, seq_len, mod.shape[0]),
            pl.BlockSpec((None, 1, D_MODEL), lambda i: (l, 0, 0)),
            pl.BlockSpec((None, D_MODEL, IN_COLS), lambda i: (l, 0, 0)),
        ],
        out_specs=[pl.BlockSpec((tm, w), lambda i: (i, 0)) for w in widths],
        out_shape=[jax.ShapeDtypeStruct((n_tok, w), F32) for w in widths],
        compiler_params=_cparams(("parallel",)),
        name="inproj",
    )(x, mod, mod, norm_g.reshape(DEPTH, 1, D_MODEL), w_in_bf)


def _conv3(x, w_ref, b_ref, period):
    n = x.shape[0]
    t = lax.broadcasted_iota(jnp.int32, x.shape, 0) % period
    prev = jnp.where(t == 0, 0.0, pltpu.roll(x, 1, axis=0))
    nxt = jnp.where(t == period - 1, 0.0, pltpu.roll(x, n - 1, axis=0))
    return prev * w_ref[0:1, :] + x * w_ref[1:2, :] + nxt * w_ref[2:3, :] + b_ref[...]


def _hy_pre_kernel(hy_ref, cw_ref, cb_ref, bias_ref, vq_ref, e_ref, x1_ref, *, period):
    u = _conv3(hy_ref[...], cw_ref, cb_ref, period)
    x0, x1, v = u[:, :HY_WIDTH], u[:, HY_WIDTH:2 * HY_WIDTH], u[:, 2 * HY_WIDTH:]
    v = v * x0
    vq_ref[...] = v.astype(BF16)
    e_ref[...] = v * bias_ref[...]
    x1_ref[...] = x1


def _hy_pre(hy, conv_w, conv_b, bias, l, seq_len, batch, period):
    nt = seq_len // TL_SEQ
    blk = lambda w: pl.BlockSpec((None, TL_SEQ, w), lambda b, i: (b, i, 0))
    shp = lambda dt: jax.ShapeDtypeStruct((batch, seq_len, HY_WIDTH), dt)
    return pl.pallas_call(
        functools.partial(_hy_pre_kernel, period=period),
        grid=(batch, nt),
        in_specs=[
            blk(C_HY),
            pl.BlockSpec((None, 3, C_HY), lambda b, i: (l, 0, 0)),
            pl.BlockSpec((None, 1, C_HY), lambda b, i: (l, 0, 0)),
            pl.BlockSpec((None, 1, HY_WIDTH), lambda b, i: (l, 0, 0)),
        ],
        out_specs=[blk(HY_WIDTH), blk(HY_WIDTH), blk(HY_WIDTH)],
        out_shape=[shp(BF16), shp(F32), shp(F32)],
        compiler_params=_cparams(("parallel", "parallel")),
        name="hyena_pre",
    )(hy, conv_w, conv_b.reshape(DEPTH, 1, C_HY), bias.reshape(DEPTH, 1, HY_WIDTH))


def _hy_filter_kernel(feat_ref, w1_ref, b1_ref, f1_ref, w2_ref, b2_ref, f2_ref, w3_ref, dl_ref,
                      hs_ref, hd_ref):
    feats = feat_ref[...]
    h = jnp.sin(f1_ref[...] * (_dot(feats, w1_ref[...], HI) + b1_ref[...]))
    h = jnp.sin(f2_ref[...] * (_dot(h, w2_ref[...], HI) + b2_ref[...]))
    h = _dot(h, w3_ref[...], HI)
    h = h * jnp.exp(-feats[:, 0:1] * dl_ref[...])
    hf, hb = h[:, :HY_WIDTH], h[:, HY_WIDTH:]
    l1 = (jnp.sum(jnp.abs(hf), axis=0, keepdims=True)
          + jnp.sum(jnp.abs(hb), axis=0, keepdims=True) + 1e-6)
    hf = hf / l1
    hb = hb / l1
    row = lax.broadcasted_iota(jnp.int32, hb.shape, 0)
    hb0 = jnp.where(row == 0, 0.0, hb)
    hs_ref[...] = (hf + hb0).astype(BF16)
    hd_ref[...] = (hb0 - hf).astype(BF16)


def _hy_filter(seq_len, w1, b1, f1, w2, b2, f2, w3):
    t = jnp.arange(seq_len, dtype=F32)
    t01 = t / max(seq_len - 1, 1)
    bands = jnp.linspace(1e-4, HY_N_BANDS - 1, HY_N_BANDS, dtype=F32)
    ang = (2.0 * math.pi / seq_len) * t[:, None] * bands[None, :]
    feats = jnp.concatenate([t01[:, None], jnp.cos(ang), -jnp.sin(ang)], axis=-1)
    feats = jnp.pad(feats, ((0, 0), (0, 128 - HY_POS_DIM)))
    w1p = jnp.pad(w1, ((0, 128 - HY_POS_DIM), (0, 0)))
    max_decay = math.log(HY_DECAY_TARGET) / HY_FAST_DECAY_PCT
    min_decay = math.log(HY_DECAY_TARGET) / HY_SLOW_DECAY_PCT
    deltas = jnp.abs(jnp.linspace(min_decay, max_decay, HY_WIDTH, dtype=F32))
    dl = jnp.concatenate([deltas, deltas])[None, :]
    hid = HY_FILTER_HIDDEN
    full = lambda shape: pl.BlockSpec(shape, lambda: tuple(0 for _ in shape))
    return pl.pallas_call(
        _hy_filter_kernel,
        in_specs=[full((seq_len, 128)), full((128, hid)), full((1, hid)), full((1, hid)),
                  full((hid, hid)), full((1, hid)), full((1, hid)), full((hid, 2 * HY_WIDTH)),
                  full((1, 2 * HY_WIDTH))],
        out_specs=[full((seq_len, HY_WIDTH)), full((seq_len, HY_WIDTH))],
        out_shape=[jax.ShapeDtypeStruct((seq_len, HY_WIDTH), BF16)] * 2,
        compiler_params=pltpu.CompilerParams(vmem_limit_bytes=V7X_VMEM_LIMIT),
        name="hyena_filter",
    )(feats, w1p, b1[None, :], f1[None, :], w2, b2[None, :], f2[None, :], w3, dl)


def _dft_matrices(seq_len):
    n2 = 2 * seq_len
    k = jnp.arange(seq_len, dtype=jnp.int32)
    prod = (k[:, None] * k[None, :]) % n2
    ang = prod.astype(F32) * (2.0 * math.pi / n2)
    cosm, sinm = jnp.cos(ang), jnp.sin(ang)
    alt = jnp.where(k % 2 == 0, 1.0, -1.0).astype(F32)
    first = (k == 0)
    fw_s = jnp.where(first[:, None], alt[None, :], sinm)
    gc = jnp.where(first[None, :], 1.0 / n2, (2.0 / n2) * cosm)
    gs = jnp.where(first[None, :], alt[:, None] / n2, (-2.0 / n2) * sinm)
    return jnp.stack([cosm, fw_s]).astype(BF16), jnp.stack([gc, gs]).astype(BF16)


def _hy_spec_kernel(hs_ref, hd_ref, f_ref, p_ref, q_ref, s_ref, *, tk):
    kre = _dot(f_ref[0], hs_ref[...])
    nyq = _dot(f_ref[1], hs_ref[...])
    kim = _dot(f_ref[1], hd_ref[...])
    row = lax.broadcasted_iota(jnp.int32, kre.shape, 0) + pl.program_id(0) * tk
    first = row == 0
    p_ref[...] = kre
    q_ref[...] = jnp.where(first, 0.0, kim)
    s_ref[...] = jnp.where(first, nyq, -kre)


def _hy_spec(hs, hd, fw, seq_len, tk):
    nf = seq_len // tk
    full = pl.BlockSpec((seq_len, HY_WIDTH), lambda f: (0, 0))
    tile = pl.BlockSpec((tk, HY_WIDTH), lambda f: (f, 0))
    return pl.pallas_call(
        functools.partial(_hy_spec_kernel, tk=tk),
        grid=(nf,),
        in_specs=[full, full, pl.BlockSpec((2, tk, seq_len), lambda f: (0, f, 0))],
        out_specs=[tile, tile, tile],
        out_shape=[jax.ShapeDtypeStruct((seq_len, HY_WIDTH), F32)] * 3,
        compiler_params=_cparams(("parallel",)),
        name="hyena_spectrum",
    )(hs, hd, fw)


def _hy_conv_kernel(v_ref, p_ref, q_ref, s_ref, f_ref, g_ref, e_ref, x1_ref, o_ref, acc_ref):
    f = pl.program_id(1)

    @pl.when(f == 0)
    def _():
        acc_ref[...] = jnp.zeros_like(acc_ref)

    v = v_ref[...]
    c = _dot(f_ref[0], v)
    s = _dot(f_ref[1], v)
    q = q_ref[...]
    yre = (c * p_ref[...] + s * q).astype(BF16)
    yim = (c * q + s * s_ref[...]).astype(BF16)
    acc_ref[...] += _dot(g_ref[0], yre) + _dot(g_ref[1], yim)

    @pl.when(f == pl.num_programs(1) - 1)
    def _():
        o_ref[...] = (acc_ref[...] + e_ref[...]) * x1_ref[...]


def _hy_conv(vq, p, q, s, fw, gi, e, x1, seq_len, batch, tk):
    nf = seq_len // tk
    seq = pl.BlockSpec((None, seq_len, HY_WIDTH), lambda b, f: (b, 0, 0))
    tile = pl.BlockSpec((tk, HY_WIDTH), lambda b, f: (f, 0))
    return pl.pallas_call(
        _hy_conv_kernel,
        grid=(batch, nf),
        in_specs=[seq, tile, tile, tile,
                  pl.BlockSpec((2, tk, seq_len), lambda b, f: (0, f, 0)),
                  pl.BlockSpec((2, seq_len, tk), lambda b, f: (0, 0, f)),
                  seq, seq],
        out_specs=seq,
        out_shape=jax.ShapeDtypeStruct((batch, seq_len, HY_WIDTH), F32),
        scratch_shapes=[pltpu.VMEM((seq_len, HY_WIDTH), F32)],
        compiler_params=_cparams(("parallel", "arbitrary")),
        name="hyena_longconv",
    )(vq, p, q, s, fw, gi, e, x1)


def _split_bf16(x):
    hi = x.astype(BF16)
    return hi, (x - hi.astype(F32)).astype(BF16)


def _dot3(a, b):
    ah, al = _split_bf16(a)
    bh, bl = _split_bf16(b)
    n = a.shape[0]
    both = _dot(jnp.concatenate([ah, al], axis=0), bh)
    return both[:n] + both[n:] + _dot(ah, bl)


def _dot_nt(a, b):
    return lax.dot_general(a, b, (((1,), (1,)), ((), ())), preferred_element_type=F32)


def _rw_pre_kernel(rkv_ref, lo_ref, cw_ref, cb_ref, w0_ref, wup_ref, a0_ref, aup_ref, gup_ref,
                   kk_ref, ka_ref, rk_ref, hs_ref, tri_ref,
                   v_out, bonus_out, g_out, at_out, rh_out, bt_out, kt_out, wt_out,
                   *, period):
    T = RW_CHUNK
    u = _conv3(rkv_ref[...], cw_ref, cb_ref, period)
    r, k, v = u[:, :RW_WIDTH], u[:, RW_WIDTH:2 * RW_WIDTH], u[:, 2 * RW_WIDTH:]
    lo = lo_ref[...]
    wraw = w0_ref[...] + _dot(jnp.tanh(lo[:, 0:128]), wup_ref[...], HI)
    z = -wraw
    softplus = jnp.maximum(z, 0.0) + jnp.log(1.0 + jnp.exp(-jnp.abs(z)))
    logw = -jnp.exp(-softplus - 0.5)
    a = jax.nn.sigmoid(a0_ref[...] + _dot(lo[:, 128:256], aup_ref[...], HI))
    g_out[...] = _dot(jax.nn.sigmoid(lo[:, 256:384]), gup_ref[...], HI)
    head_sum = hs_ref[...]
    kk = k * kk_ref[...]
    kk = kk * lax.rsqrt(jnp.maximum(_dot(kk * kk, head_sum, HI), 1e-24))
    v_out[...] = v
    n_chunk = r.shape[0] // T
    lane = lax.broadcasted_iota(jnp.int32, (RW_WIDTH, 128), 1)
    kt_sum = jnp.zeros_like(k)
    for d in range(2):
        a_d = a[:, d * RW_WIDTH:(d + 1) * RW_WIDTH]
        lw_d = logw[:, d * RW_WIDTH:(d + 1) * RW_WIDTH]
        kt_d = k * (1.0 + (a_d - 1.0) * ka_ref[...])
        kt_sum = kt_sum + kt_d
        cum = _dot(tri_ref[d], lw_d, HI)
        w_inv = jnp.exp(-cum)
        at_out[d] = -kk * jnp.exp(cum - lw_d)
        rh_out[d] = r * jnp.exp(cum)
        bt = (kk * a_d * w_inv).T
        ktt = (kt_d * w_inv).T
        for j in range(0, n_chunk, 2):
            for src, dst in ((bt, bt_out), (ktt, kt_out)):
                pair = src[:, j * T:(j + 2) * T]
                swapped = pltpu.roll(pair, T, axis=1)
                dst[d, j] = jnp.where(lane < T, pair, swapped)
                dst[d, j + 1] = jnp.where(lane < T, swapped, pair)
        for j in range(n_chunk):
            last = j * T + (T - 1 if d == 0 else 0)
            wt_out[d, j] = jnp.exp(cum[last:last + 1, :])
    bonus_out[...] = _dot(r * kt_sum * rk_ref[...], head_sum, HI) * v


def _chunk_tri(n_rows):
    i = np.arange(n_rows)
    same = (i[:, None] // RW_CHUNK) == (i[None, :] // RW_CHUNK)
    fwd = same & (i[None, :] <= i[:, None])
    bwd = same & (i[None, :] >= i[:, None])
    return jnp.asarray(np.stack([fwd, bwd]).astype(np.float32))


def _rw_pre(rkv2d, lora2d, P, l, seq_len, batch, period):
    nt = seq_len // TL_SEQ
    cpt = TL_SEQ // RW_CHUNK
    nchunk = seq_len // RW_CHUNK
    blk = lambda w: pl.BlockSpec((None, TL_SEQ, w), lambda b, i: (b, i, 0))
    par = lambda shape: pl.BlockSpec((None,) + shape, lambda b, i: (l,) + tuple(0 for _ in shape))
    row_shape = jax.ShapeDtypeStruct((batch, seq_len, RW_WIDTH), F32)
    dir_shape = jax.ShapeDtypeStruct((2, batch, seq_len, RW_WIDTH), F32)
    dir_blk = pl.BlockSpec((2, None, TL_SEQ, RW_WIDTH), lambda b, i: (0, b, i, 0))
    tr_shape = jax.ShapeDtypeStruct((2, batch, nchunk, RW_WIDTH, 128), F32)
    tr_blk = pl.BlockSpec((2, None, cpt, RW_WIDTH, 128), lambda b, i: (0, b, i, 0, 0))
    wt_shape = jax.ShapeDtypeStruct((2, batch, nchunk, 1, RW_WIDTH), F32)
    wt_blk = pl.BlockSpec((2, None, cpt, 1, RW_WIDTH), lambda b, i: (0, b, i, 0, 0))
    return pl.pallas_call(
        functools.partial(_rw_pre_kernel, period=period),
        grid=(batch, nt),
        in_specs=[
            blk(C_RKV), blk(C_LORA),
            par((3, C_RKV)), par((1, C_RKV)),
            par((1, 2 * RW_WIDTH)), par((128, 2 * RW_WIDTH)),
            par((1, 2 * RW_WIDTH)), par((128, 2 * RW_WIDTH)),
            par((128, RW_WIDTH)),
            par((1, RW_WIDTH)), par((1, RW_WIDTH)), par((1, RW_WIDTH)),
            pl.BlockSpec((RW_WIDTH, RW_WIDTH), lambda b, i: (0, 0)),
            pl.BlockSpec((2, TL_SEQ, TL_SEQ), lambda b, i: (0, 0, 0)),
        ],
        out_specs=[blk(RW_WIDTH)] * 3 + [dir_blk, dir_blk, tr_blk, tr_blk, wt_blk],
        out_shape=[row_shape] * 3 + [dir_shape, dir_shape, tr_shape, tr_shape, wt_shape],
        compiler_params=_cparams(("parallel", "parallel")),
        name="rwkv_pre",
    )(rkv2d, lora2d, P["rw_conv_w"], P["rw_conv_b3"], P["rw_w0c"], P["rw_wupc"], P["rw_a0c"],
      P["rw_aupc"], P["rw_g_up"], P["rw_k_k3"], P["rw_k_a3"], P["rw_r_k3"], P["head_sum"],
      _chunk_tri(TL_SEQ))


RW_PAIRS = RW_HEADS // 2
RW_PREP_CHUNKS = 4


def _rw_prep_kernel(at_ref, rh_ref, v_ref, bt_ref, kt_ref, whi_ref, wlo_ref, rt_ref, ot_ref):
    d = pl.program_id(0)
    T = RW_CHUNK
    sgn = 1 - 2 * d
    ti = lax.broadcasted_iota(jnp.int32, (T, 128), 0)
    tj = lax.broadcasted_iota(jnp.int32, (T, 128), 1) % T
    before = (tj - ti) * sgn < 0
    before_eq = (tj - ti) * sgn <= 0
    eye2 = (ti == tj).astype(F32)
    bi = lax.broadcasted_iota(jnp.int32, (128, 128), 0)
    bj = lax.broadcasted_iota(jnp.int32, (128, 128), 1)
    same_head = (bi // T) == (bj // T)
    eye128 = (bi == bj).astype(F32)

    def bdiag(x):
        return jnp.where(same_head, jnp.concatenate([x, x], axis=0), 0.0)

    probs = [(j, p) for j in range(RW_PREP_CHUNKS) for p in range(RW_PAIRS)]
    rows = lambda j: slice(j * T, (j + 1) * T)
    cols = lambda p: slice(p * 128, (p + 1) * 128)
    a_l = [at_ref[rows(j), cols(p)] for j, p in probs]
    r_l = [rh_ref[rows(j), cols(p)] for j, p in probs]
    bd_b = [jnp.where(same_head, bt_ref[j, cols(p), :], 0.0) for j, p in probs]
    bd_k = [jnp.where(same_head, kt_ref[j, cols(p), :], 0.0) for j, p in probs]
    gram = [_dot3(jnp.concatenate([a, r], axis=0), jnp.concatenate([b, k], axis=1))
            for a, r, b, k in zip(a_l, r_l, bd_b, bd_k)]
    l_ab = [jnp.where(before, g[:T, :128], 0.0) for g in gram]
    l_ak = [jnp.where(before, g[:T, 128:], 0.0) for g in gram]
    g_rb = [jnp.where(before_eq, g[T:, :128], 0.0) for g in gram]
    g_rk = [jnp.where(before_eq, g[T:, 128:], 0.0) for g in gram]
    m = [eye2 + l for l in l_ab]
    lp = [_dot3(l, bdiag(l)) for l in l_ab]
    n_sq = int(math.log2(T)) - 1
    for s in range(n_sq):
        if s < n_sq - 1:
            out = [_dot3(jnp.concatenate([mm, ll], axis=0), bdiag(ll)) for mm, ll in zip(m, lp)]
            m = [mm + o[:T] for mm, o in zip(m, out)]
            lp = [o[T:] for o in out]
        else:
            m = [mm + _dot3(mm, bdiag(ll)) for mm, ll in zip(m, lp)]
    x = [_dot3(mm, jnp.concatenate([bdiag(a), bdiag(l)], axis=1))
         for mm, a, l in zip(m, a_l, l_ak)]
    y = [_dot3(bt_ref[j, cols(p), 0:T], xx) for (j, p), xx in zip(probs, x)]
    bd_v = [bdiag(v_ref[rows(j), cols(p)]) for j, p in probs]
    psi = [_dot3(jnp.where(same_head, yy[:, 128:], 0.0) + k, v)
           for yy, k, v in zip(y, bd_k, bd_v)]
    for (j, p), yy, ps in zip(probs, y, psi):
        phi = jnp.where(same_head, eye128 + yy[:, :128], 0.0)
        w_hi, w_lo = _split_bf16(jnp.concatenate([phi, ps], axis=1))
        whi_ref[j, p] = w_hi
        wlo_ref[j, p] = w_lo
    z = [_dot(g.astype(BF16), jnp.concatenate([bdiag(xx[:, :128]), bdiag(xx[:, 128:])], axis=1).astype(BF16))
         for g, xx in zip(g_rb, x)]
    for (j, p), zz, r, g, v in zip(probs, z, r_l, g_rk, bd_v):
        rt_ref[rows(j), cols(p)] = (r + zz[:, :128]).astype(BF16)
        ot_ref[rows(j), cols(p)] = _dot((zz[:, 128:] + g).astype(BF16), v.astype(BF16))


def _rw_prep(at, rh, v, bt, kt, seq_len, batch):
    nchunk = seq_len // RW_CHUNK
    cb = RW_PREP_CHUNKS
    rows = cb * RW_CHUNK
    drow = pl.BlockSpec((None, None, rows, RW_WIDTH), lambda d, b, i: (d, b, i, 0))
    tr = pl.BlockSpec((None, None, cb, RW_WIDTH, 128), lambda d, b, i: (d, b, i, 0, 0))
    wsp = pl.BlockSpec((None, None, cb, RW_PAIRS, 128, 256), lambda d, b, i: (d, b, i, 0, 0, 0))
    w_shape = jax.ShapeDtypeStruct((2, batch, nchunk, RW_PAIRS, 128, 256), BF16)
    return pl.pallas_call(
        _rw_prep_kernel,
        grid=(2, batch, nchunk // cb),
        in_specs=[drow, drow, pl.BlockSpec((None, rows, RW_WIDTH), lambda d, b, i: (b, i, 0)), tr, tr],
        out_specs=[wsp, wsp, drow, drow],
        out_shape=[w_shape, w_shape,
                   jax.ShapeDtypeStruct((2, batch, seq_len, RW_WIDTH), BF16),
                   jax.ShapeDtypeStruct((2, batch, seq_len, RW_WIDTH), F32)],
        compiler_params=_cparams(("parallel", "parallel", "parallel")),
        name="rwkv_chunk_ops",
    )(at, rh, v, bt, kt)


def _rw_seq_kernel(whi_f, wlo_f, whi_b, wlo_b, rt_f, rt_b, ot_f, ot_b, wt_f, wt_b, s0_ref,
                   o_f, o_b, sfin_ref, st_ref, *, bb):
    c = pl.program_id(1)
    T = RW_CHUNK

    @pl.when(c == 0)
    def _():
        st_ref[...] = s0_ref[...]

    bi_ = lax.broadcasted_iota(jnp.int32, (128, 128), 0)
    bj_ = lax.broadcasted_iota(jnp.int32, (128, 128), 1)
    same_head = (bi_ // T) == (bj_ // T)
    ji = lax.broadcasted_iota(jnp.int32, (T, 128), 0)
    jj = lax.broadcasted_iota(jnp.int32, (T, 128), 1) % T
    eye2 = (ji == jj).astype(BF16)

    dirs = ((whi_f, wlo_f, rt_f, ot_f, wt_f, o_f), (whi_b, wlo_b, rt_b, ot_b, wt_b, o_b))
    probs = [(d, b, p) for d in range(2) for b in range(bb) for p in range(RW_PAIRS)]
    cols = lambda p: slice(p * 128, (p + 1) * 128)
    s_l =[st_ref[d, b, p] for d, b, p in probs]
    split = [_split_bf16(s) for s in s_l]
    lhs = [jnp.concatenate([hi, eye2], axis=1) for hi, _ in split]
    new = [_dot_nt(lh, dirs[d][0][b, p]) + _dot_nt(lh, dirs[d][1][b, p])
           + _dot_nt(lo, dirs[d][0][b, p, :, 0:128])
           for (d, b, p), lh, (_, lo) in zip(probs, lhs, split)]
    for (d, b, p), nw in zip(probs, new):
        st_ref[d, b, p] = nw * dirs[d][4][b, :, p * 128:(p + 1) * 128]
    for (d, b, p), s in zip(probs, s_l):
        _, _, rt, ot, _, o = dirs[d]
        bd_s = jnp.where(same_head, jnp.concatenate([s, s], axis=0), 0.0).astype(BF16)
        o[b, :, cols(p)] = _dot_nt(rt[b, :, cols(p)], bd_s) + ot[b, :, cols(p)]

    @pl.when(c == pl.num_programs(1) - 1)
    def _():
        sfin_ref[...] = st_ref[...]


def _rw_seq(whi, wlo, rt, ot, wt, s0p, seq_len, batch):
    nc = seq_len // RW_CHUNK
    bb = min(batch, 8)
    rev = lambda c: nc - 1 - c
    fwd = lambda c: c
    wsp = lambda d, ch: pl.BlockSpec((None, bb, None, RW_PAIRS, 128, 256),
                                     lambda g, c: (d, g, ch(c), 0, 0, 0))
    row = lambda d, ch: pl.BlockSpec((None, bb, RW_CHUNK, RW_WIDTH), lambda g, c: (d, g, ch(c), 0))
    wts = lambda d, ch: pl.BlockSpec((None, bb, None, 1, RW_WIDTH), lambda g, c: (d, g, ch(c), 0, 0))
    st = pl.BlockSpec((2, bb, RW_PAIRS, RW_HEAD_DIM, 128), lambda g, c: (0, g, 0, 0, 0))
    o_spec = lambda ch: pl.BlockSpec((bb, RW_CHUNK, RW_WIDTH), lambda g, c: (g, ch(c), 0))
    o_shape = jax.ShapeDtypeStruct((batch, seq_len, RW_WIDTH), F32)
    return pl.pallas_call(
        functools.partial(_rw_seq_kernel, bb=bb),
        grid=(batch // bb, nc),
        in_specs=[wsp(0, fwd), wsp(0, fwd), wsp(1, rev), wsp(1, rev),
                  row(0, fwd), row(1, rev), row(0, fwd), row(1, rev),
                  wts(0, fwd), wts(1, rev), st],
        out_specs=[o_spec(fwd), o_spec(rev), st],
        out_shape=[o_shape, o_shape,
                   jax.ShapeDtypeStruct((2, batch, RW_PAIRS, RW_HEAD_DIM, 128), F32)],
        scratch_shapes=[pltpu.VMEM((2, bb, RW_PAIRS, RW_HEAD_DIM, 128), F32)],
        compiler_params=_cparams(("parallel", "arbitrary")),
        name="rwkv_state_scan",
    )(whi, wlo, whi, wlo, rt, rt, ot, ot, wt, wt, s0p)


def _rw_pack_state(s):
    b = s.shape[0]
    s = s.reshape(b, 2, RW_PAIRS, 2, RW_HEAD_DIM, RW_HEAD_DIM)
    return jnp.transpose(s, (1, 0, 2, 4, 3, 5)).reshape(2, b, RW_PAIRS, RW_HEAD_DIM, 128)


def _rw_unpack_state(s):
    b = s.shape[1]
    s = s.reshape(2, b, RW_PAIRS, RW_HEAD_DIM, 2, RW_HEAD_DIM)
    return jnp.transpose(s, (1, 0, 2, 4, 3, 5)).reshape(b, 2, RW_HEADS, RW_HEAD_DIM, RW_HEAD_DIM)


def _s5_kernel(u_ref, wb_ref, wc_ref, a_ref, x0_ref, y_ref, fin_ref, bu_ref, st_ref, *, batch, steps):
    d = pl.program_id(0)
    i = pl.program_id(1)
    ns = S5_NS

    @pl.when(i == 0)
    def _():
        st_ref[...] = x0_ref[...]

    bu_ref[...] = _dot(u_ref[...].astype(BF16), wb_ref[...])

    if batch % 8 == 0:
        lanes = 256
        for lc in range(ns // lanes):
            re_sl = slice(lc * lanes, (lc + 1) * lanes)
            im_sl = slice(ns + lc * lanes, ns + (lc + 1) * lanes)
            ar, ai = a_ref[:, re_sl], a_ref[:, im_sl]

            def body(s, carry, re_sl=re_sl, im_sl=im_sl, ar=ar, ai=ai):
                xr, xi = carry
                step = s + d * (steps - 1 - 2 * s)
                rows = pl.ds(pl.multiple_of(step * batch, 8), batch)
                nr = ar * xr - ai * xi + bu_ref[rows, re_sl]
                ni = ar * xi + ai * xr + bu_ref[rows, im_sl]
                bu_ref[rows, re_sl] = nr
                bu_ref[rows, im_sl] = ni
                return nr, ni

            xr, xi = lax.fori_loop(0, steps, body, (st_ref[:, re_sl], st_ref[:, im_sl]))
            st_ref[:, re_sl] = xr
            st_ref[:, im_sl] = xi
    else:
        ar, ai = a_ref[:, :ns], a_ref[:, ns:]
        half = lax.broadcasted_iota(jnp.int32, (8, ns), 0) // 4
        first = half == d

        def body(s, carry):
            xr, xi = carry
            pair = s + d * (steps // 2 - 1 - 2 * s)
            rows = pl.ds(pl.multiple_of(pair * 8, 8), 8)
            br, bi = bu_ref[rows, :ns], bu_ref[rows, ns:]
            r1 = ar * xr - ai * xi + br
            i1 = ar * xi + ai * xr + bi
            r1s, i1s = pltpu.roll(r1, 4, axis=0), pltpu.roll(i1, 4, axis=0)
            r2 = ar * r1s - ai * i1s + br
            i2 = ar * i1s + ai * r1s + bi
            bu_ref[rows, :ns] = jnp.where(first, r1, r2)
            bu_ref[rows, ns:] = jnp.where(first, i1, i2)
            return pltpu.roll(r2, 4, axis=0), pltpu.roll(i2, 4, axis=0)

        xr, xi = lax.fori_loop(0, steps // 2, body, (st_ref[:, :ns], st_ref[:, ns:]))
        st_ref[:, :ns] = xr
        st_ref[:, ns:] = xi

    y_ref[...] = _dot(bu_ref[...].astype(BF16), wc_ref[...])

    @pl.when(i == pl.num_programs(1) - 1)
    def _():
        fin_ref[...] = st_ref[...]


def _s5_scan(u, wb, wc, a_bar, x0, l, seq_len, batch):
    rows_blk = 512
    steps = rows_blk // batch
    nblk = seq_len * batch // rows_blk
    rows = max(batch, 8)
    blk = lambda d, i: i + d * (nblk - 1 - 2 * i)
    return pl.pallas_call(
        functools.partial(_s5_kernel, batch=batch, steps=steps),
        grid=(2, nblk),
        in_specs=[
            pl.BlockSpec((rows_blk, S5_WIDTH), lambda d, i: (blk(d, i), 0)),
            pl.BlockSpec((None, None, S5_WIDTH, 2 * S5_NS), lambda d, i: (l, d, 0, 0)),
            pl.BlockSpec((None, None, 2 * S5_NS, S5_WIDTH), lambda d, i: (l, d, 0, 0)),
            pl.BlockSpec((None, None, 1, 2 * S5_NS), lambda d, i: (l, d, 0, 0)),
            pl.BlockSpec((None, rows, 2 * S5_NS), lambda d, i: (d, 0, 0)),
        ],
        out_specs=[
            pl.BlockSpec((None, rows_blk, S5_WIDTH), lambda d, i: (d, blk(d, i), 0)),
            pl.BlockSpec((None, rows, 2 * S5_NS), lambda d, i: (d, 0, 0)),
        ],
        out_shape=[jax.ShapeDtypeStruct((2, seq_len * batch, S5_WIDTH), F32),
                   jax.ShapeDtypeStruct((2, rows, 2 * S5_NS), F32)],
        scratch_shapes=[pltpu.VMEM((rows_blk, 2 * S5_NS), F32), pltpu.VMEM((rows, 2 * S5_NS), F32)],
        compiler_params=_cparams(("parallel", "arbitrary")),
        name="s5_scan",
    )(u, wb, wc, a_bar, x0)


def _gelu_tanh(x):
    return 0.5 * x * (1.0 + jnp.tanh(math.sqrt(2.0 / math.pi) * (x + 0.044715 * x * x * x)))


def _merge_kernel(x_ref, gate_ref, yhy_ref, of_ref, ob_ref, bonus_ref, g_ref, u5_ref, y5f_ref, y5b_ref,
                  gt1_ref, sh2_ref, sc2_ref,
                  hm_ref, gnw_ref, gnb_ref, s5d_ref, gluw_ref, glub_ref,
                  wbhy_ref, wbrw_ref, wbs5_ref, wout_ref, n2g_ref, rw_ref, rb_ref,
                  h_ref, hn_ref, gates_ref):
    o = of_ref[...] + ob_ref[...]
    head_mean = hm_ref[...]
    mu = _dot(o, head_mean, HI)
    oc = o - mu
    var = _dot(oc * oc, head_mean, HI)
    y_rw = (oc * lax.rsqrt(var + RW_GN_EPS) * gnw_ref[...] + gnb_ref[...] + bonus_ref[...]) * g_ref[...]
    y5 = _gelu_tanh(u5_ref[...] * s5d_ref[...] + y5f_ref[...] + y5b_ref[...])
    y5 = y5 * jax.nn.sigmoid(_dot(y5.astype(BF16), gluw_ref[...]) + glub_ref[...])
    gate = jax.nn.sigmoid(gate_ref[...])
    merged = (gate[:, :D_MODEL] * _dot(yhy_ref[...].astype(BF16), wbhy_ref[...])
              + gate[:, D_MODEL:2 * D_MODEL] * _dot(y_rw.astype(BF16), wbrw_ref[...])
              + gate[:, 2 * D_MODEL:] * _dot(y5.astype(BF16), wbs5_ref[...]))
    h = x_ref[...] + gt1_ref[...] * _dot(merged.astype(BF16), wout_ref[...])
    h_ref[...] = h
    hn = h * lax.rsqrt(jnp.mean(h * h, axis=-1, keepdims=True) + NORM_EPS) * n2g_ref[...]
    hn = hn * (1.0 + sc2_ref[...]) + sh2_ref[...]
    hn_ref[...] = hn.astype(BF16)

    scores = jax.nn.sigmoid(_dot(hn, rw_ref[...], HI))
    sel = scores + rb_ref[...]
    s_col = [sel[:, e:e + 1] for e in range(N_EXPERTS)]
    best_val = None
    best_grp = None
    for grp in range(N_EXPERT_GROUPS):
        a, b, c, dd = s_col[4 * grp:4 * grp + 4]
        hi1, lo1 = jnp.maximum(a, b), jnp.minimum(a, b)
        hi2, lo2 = jnp.maximum(c, dd), jnp.minimum(c, dd)
        m1 = jnp.maximum(hi1, hi2)
        m2 = jnp.maximum(jnp.minimum(hi1, hi2), jnp.where(hi1 >= hi2, lo1, lo2))
        val = m1 + m2
        if grp == 0:
            best_val, best_grp = val, jnp.zeros_like(val, dtype=jnp.int32)
        else:
            better = val > best_val
            best_val = jnp.where(better, val, best_val)
            best_grp = jnp.where(better, grp, best_grp)
    picked = []
    for e in range(N_EXPERTS):
        grp, pos = divmod(e, EXPERTS_PER_GROUP)
        rank = jnp.zeros_like(best_grp)
        for other in range(EXPERTS_PER_GROUP):
            if other == pos:
                continue
            so = s_col[4 * grp + other]
            ahead = (so > s_col[e]) if other > pos else (so >= s_col[e])
            rank = rank + ahead.astype(jnp.int32)
        take = jnp.logical_and(rank < 2, best_grp == grp)
        picked.append(jnp.where(take, scores[:, e:e + 1], 0.0))
    total = picked[0]
    for e in range(1, N_EXPERTS):
        total = total + picked[e]
    lane = lax.broadcasted_iota(jnp.int32, gates_ref.shape, 1)
    gates = jnp.zeros(gates_ref.shape, F32)
    for e in range(N_EXPERTS):
        gates = jnp.where(lane == e, picked[e] / total, gates)
    gates_ref[...] = gates


def _merge(x, gates_in, y_hy, o_f, o_b, bonus, g, u5, y5f, y5b, mod, P, l, seq_len):
    n_tok = x.shape[0]
    tm = TM_TOK
    tok = lambda w: pl.BlockSpec((tm, w), lambda i: (i, 0))
    n_mod = mod.shape[0]
    par = lambda shape: pl.BlockSpec((None,) + shape, lambda i: (l,) + tuple(0 for _ in shape))
    shared = lambda shape: pl.BlockSpec(shape, lambda i: tuple(0 for _ in shape))
    return pl.pallas_call(
        _merge_kernel,
        grid=(n_tok // tm,),
        in_specs=[
            tok(D_MODEL), tok(C_GATE), tok(HY_WIDTH), tok(RW_WIDTH), tok(RW_WIDTH), tok(RW_WIDTH),
            tok(RW_WIDTH), tok(S5_WIDTH), tok(S5_WIDTH), tok(S5_WIDTH),
            _mod_spec(2, tm, seq_len, n_mod), _mod_spec(3, tm, seq_len, n_mod), _mod_spec(4, tm, seq_len, n_mod),
            shared((RW_WIDTH, RW_WIDTH)), par((1, RW_WIDTH)), par((1, RW_WIDTH)),
            par((1, S5_WIDTH)), par((S5_WIDTH, S5_WIDTH)), par((1, S5_WIDTH)),
            par((HY_WIDTH, D_MODEL)), par((RW_WIDTH, D_MODEL)), par((S5_WIDTH, D_MODEL)),
            par((D_MODEL, D_MODEL)), par((1, D_MODEL)),
            shared((D_MODEL, 128)), shared((1, 128)),
        ],
        out_specs=[tok(D_MODEL), tok(D_MODEL), tok(128)],
        out_shape=[jax.ShapeDtypeStruct((n_tok, D_MODEL), F32),
                   jax.ShapeDtypeStruct((n_tok, D_MODEL), BF16),
                   jax.ShapeDtypeStruct((n_tok, 128), F32)],
        compiler_params=_cparams(("parallel",)),
        name="merge_router",
    )(x, gates_in, y_hy, o_f, o_b, bonus, g, u5, y5f, y5b, mod, mod, mod,
      P["head_mean"], P["rw_gn_w3"], P["rw_gn_b3"], P["s5_d3"], P["s5_glu_w_bf"], P["s5_glu_b3"],
      P["wb_hy_bf"], P["wb_rw_bf"], P["wb_s5_bf"], P["w_out_bf"], P["norm2_g3"],
      P["router_w_pad"], P["router_b_pad"])


def _moe_kernel(hn_ref, gates_ref, h_ref, gt2_ref, wg_ref, wu_ref, wd_ref, fg_ref, x_ref, acc_ref,
                *, final):
    e = pl.program_id(1)

    @pl.when(e == 0)
    def _():
        acc_ref[...] = jnp.zeros_like(acc_ref)

    hn = hn_ref[...]
    a = _dot(hn, wg_ref[...])
    he = (a * jax.nn.sigmoid(a)) * _dot(hn, wu_ref[...])
    lane = lax.broadcasted_iota(jnp.int32, gates_ref.shape, 1)
    gate = jnp.sum(jnp.where(lane == e, gates_ref[...], 0.0), axis=1, keepdims=True)
    acc_ref[...] += gate * _dot(he.astype(BF16), wd_ref[...])

    @pl.when(e == pl.num_programs(1) - 1)
    def _():
        x = h_ref[...] + gt2_ref[...] * acc_ref[...]
        if final:
            x = x * lax.rsqrt(jnp.mean(x * x, axis=-1, keepdims=True) + NORM_EPS) * fg_ref[...]
        x_ref[...] = x


def _moe(hn, gates, h, mod, wg, wu, wd, final_g, l, seq_len, final):
    n_tok = hn.shape[0]
    tm = 1024
    tok = lambda w: pl.BlockSpec((tm, w), lambda i, e: (i, 0))
    return pl.pallas_call(
        functools.partial(_moe_kernel, final=final),
        grid=(n_tok // tm, N_EXPERTS),
        in_specs=[
            tok(D_MODEL), tok(128), tok(D_MODEL),
            _mod_spec(5, tm, seq_len, mod.shape[0]),
            pl.BlockSpec((None, None, D_MODEL, EXPERT_FF), lambda i, e: (l, e, 0, 0)),
            pl.BlockSpec((None, None, D_MODEL, EXPERT_FF), lambda i, e: (l, e, 0, 0)),
            pl.BlockSpec((None, None, EXPERT_FF, D_MODEL), lambda i, e: (l, e, 0, 0)),
            pl.BlockSpec((1, D_MODEL), lambda i, e: (0, 0)),
        ],
        out_specs=tok(D_MODEL),
        out_shape=jax.ShapeDtypeStruct((n_tok, D_MODEL), F32),
        scratch_shapes=[pltpu.VMEM((tm, D_MODEL), F32)],
        compiler_params=_cparams(("parallel", "arbitrary")),
        name="moe",
    )(hn, gates, h, mod, wg, wu, wd, final_g.reshape(1, D_MODEL))


def _block_diag(blocks):
    g, r, c = blocks.shape[-3:]
    eye = jnp.eye(g, dtype=blocks.dtype)
    out = blocks[..., :, :, None, :] * eye[:, None, :, None]
    return out.reshape(blocks.shape[:-3] + (g * r, g * c))


def _s5_params(lam_re, lam_im, log_dt, b_re, b_im, c_re, c_im):
    lr, li = lam_re.astype(F32), lam_im.astype(F32)
    dt = jnp.exp(log_dt.astype(F32))[..., None]
    mag = jnp.exp(lr * dt)
    ar, ai = mag * jnp.cos(li * dt), mag * jnp.sin(li * dt)
    den = lr * lr + li * li
    qr = ((ar - 1.0) * lr + ai * li) / den
    qi = (ai * lr - (ar - 1.0) * li) / den
    bbr = qr[..., None] * b_re - qi[..., None] * b_im
    bbi = qr[..., None] * b_im + qi[..., None] * b_re
    tr = lambda m: jnp.swapaxes(m, -1, -2)
    wb = jnp.concatenate([_block_diag(tr(bbr)), _block_diag(tr(bbi))], axis=-1)
    wc = jnp.concatenate([_block_diag(tr(c_re)), -_block_diag(tr(c_im))], axis=-2)
    a_bar = jnp.concatenate([ar.reshape(DEPTH, 2, 1, S5_NS), ai.reshape(DEPTH, 2, 1, S5_NS)], axis=-1)
    return wb.astype(BF16), wc.astype(BF16), a_bar


def kernel(x_prompt, x_sample, state_rwkv, state_s5_re, state_s5_im, c, c_ctx, ada_w, ada_b, norm1_g, norm2_g, final_g, w_in, hy_conv_w, hy_conv_b, hy_f_w1, hy_f_b1, hy_f_freq1, hy_f_w2, hy_f_b2, hy_f_freq2, hy_f_w3, hy_bias, rw_conv_w, rw_conv_b, rw_w0, rw_w_up, rw_a0, rw_a_up, rw_g_up, rw_k_k, rw_k_a, rw_r_k, rw_gn_w, rw_gn_b, s5_lam_re, s5_lam_im, s5_log_dt, s5_b_re, s5_b_im, s5_c_re, s5_c_im, s5_d, s5_glu_w, s5_glu_b, wb_hy, wb_rw, wb_s5, w_out, router_w, router_b, exp_wg, exp_wu, exp_wd):
    bc, lc = x_prompt.shape[0], x_prompt.shape[1]
    bl, ll = x_sample.shape[0], x_sample.shape[1]
    n_ctx, n_lat = bc * lc, bl * ll
    assert n_ctx == n_lat and n_ctx % 1024 == 0 and bl == 4 and bc % 8 == 0
    streams = ((lc, bc, lc), (ll, bl, GRID_W))

    head_id = np.arange(RW_WIDTH) // RW_HEAD_DIM
    head_sum = (head_id[:, None] == head_id[None, :]).astype(np.float32)
    P = dict(
        rw_conv_w=rw_conv_w, rw_conv_b3=rw_conv_b.reshape(DEPTH, 1, C_RKV),
        rw_w0c=rw_w0.reshape(DEPTH, 1, 2 * RW_WIDTH),
        rw_wupc=_block_diag(rw_w_up), rw_a0c=rw_a0.reshape(DEPTH, 1, 2 * RW_WIDTH),
        rw_aupc=_block_diag(rw_a_up), rw_g_up=rw_g_up,
        rw_k_k3=rw_k_k.reshape(DEPTH, 1, RW_WIDTH), rw_k_a3=rw_k_a.reshape(DEPTH, 1, RW_WIDTH),
        rw_r_k3=rw_r_k.reshape(DEPTH, 1, RW_WIDTH),
        head_sum=jnp.asarray(head_sum), head_mean=jnp.asarray(head_sum / RW_HEAD_DIM),
        rw_gn_w3=rw_gn_w.reshape(DEPTH, 1, RW_WIDTH), rw_gn_b3=rw_gn_b.reshape(DEPTH, 1, RW_WIDTH),
        s5_d3=s5_d.reshape(DEPTH, 1, S5_WIDTH), s5_glu_w_bf=s5_glu_w.astype(BF16),
        s5_glu_b3=s5_glu_b.reshape(DEPTH, 1, S5_WIDTH),
        wb_hy_bf=wb_hy.astype(BF16), wb_rw_bf=wb_rw.astype(BF16), wb_s5_bf=wb_s5.astype(BF16),
        w_out_bf=w_out.astype(BF16), norm2_g3=norm2_g.reshape(DEPTH, 1, D_MODEL),
        router_w_pad=jnp.pad(router_w, ((0, 0), (0, 128 - N_EXPERTS))),
        router_b_pad=jnp.pad(router_b, (0, 128 - N_EXPERTS)).reshape(1, 128),
    )
    w_in_bf = w_in.astype(BF16)
    wg_bf, wu_bf, wd_bf = exp_wg.astype(BF16), exp_wu.astype(BF16), exp_wd.astype(BF16)
    s5_wb, s5_wc, s5_abar = _s5_params(s5_lam_re, s5_lam_im, s5_log_dt, s5_b_re, s5_b_im, s5_c_re, s5_c_im)
    dft = {seq_len: _dft_matrices(seq_len) for seq_len in (lc, ll)}

    xs = [x_prompt.astype(F32).reshape(n_ctx, D_MODEL), x_sample.astype(F32).reshape(n_lat, D_MODEL)]
    cond8 = jnp.zeros((8, D_MODEL), F32).at[:bl].set(c.astype(F32)).at[bl].set(c_ctx.astype(F32))

    rw_s0 = (jnp.zeros((bc, DEPTH, 2, RW_HEADS, RW_HEAD_DIM, RW_HEAD_DIM), F32),
             state_rwkv.astype(F32))
    s5_lat = jnp.concatenate([state_s5_re.reshape(bl, DEPTH, 2, S5_NS),
                              state_s5_im.reshape(bl, DEPTH, 2, S5_NS)], axis=-1).astype(F32)
    s5_lat = jnp.tile(jnp.moveaxis(s5_lat, 0, 2), (1, 1, 2, 1))
    s5_x0 = (jnp.zeros((DEPTH, 2, bc, 2 * S5_NS), F32), s5_lat)

    rw_new, s5_new = [], []
    for l in range(DEPTH):
        m = _modulation(cond8, ada_w, ada_b, l).reshape(8, 6, 1, D_MODEL)
        mods = (m[bl:bl + 1], m[:bl])
        for si, (seq_len, batch, period) in enumerate(streams):
            n_tok = seq_len * batch
            x, mod = xs[si], mods[si]
            g_in, hy_in, rkv_in, lo_in, s5_in = _inproj(x, mod, norm1_g, w_in_bf, l, seq_len)
            seq = lambda a: a.reshape(batch, seq_len, a.shape[-1])
            tok = lambda a: a.reshape(n_tok, a.shape[-1])
            vq, e, x1 = _hy_pre(seq(hy_in), hy_conv_w, hy_conv_b, hy_bias, l, seq_len, batch, period)
            hs, hd = _hy_filter(seq_len, hy_f_w1[l], hy_f_b1[l], hy_f_freq1[l], hy_f_w2[l], hy_f_b2[l],
                                hy_f_freq2[l], hy_f_w3[l])
            fw, gi = dft[seq_len]
            tk = min(seq_len, 256)
            sp, sq, ss = _hy_spec(hs, hd, fw, seq_len, tk)
            y_hy = _hy_conv(vq, sp, sq, ss, fw, gi, e, x1, seq_len, batch, tk)
            v_rw, bon, g_rw, at, rh, bt, kt, wt = _rw_pre(seq(rkv_in), seq(lo_in), P, l, seq_len, batch, period)
            whi, wlo, rt, ot = _rw_prep(at, rh, v_rw, bt, kt, seq_len, batch)
            o_f, o_b, s_fin = _rw_seq(whi, wlo, rt, ot, wt, _rw_pack_state(rw_s0[si][:, l]), seq_len, batch)
            u_tm = jnp.swapaxes(seq(s5_in), 0, 1).reshape(n_tok, S5_WIDTH)
            y5d, fin5 = _s5_scan(u_tm, s5_wb, s5_wc, s5_abar, s5_x0[si][l], l, seq_len, batch)
            y5d = jnp.swapaxes(y5d.reshape(2, seq_len, batch, S5_WIDTH), 1, 2).reshape(2, n_tok, S5_WIDTH)
            if si == 0:
                rw_new.append(_rw_unpack_state(s_fin))
                s5_new.append(fin5)
            h, hn, gates = _merge(x, g_in, tok(y_hy), tok(o_f), tok(o_b), tok(bon), tok(g_rw), s5_in,
                                  y5d[0], y5d[1], mod, P, l, seq_len)
            xs[si] = _moe(hn, gates, h, mod, wg_bf, wu_bf, wd_bf, final_g, l, seq_len, final=(l == DEPTH - 1))

    y_prompt = xs[0].reshape(bc, lc, D_MODEL).astype(x_prompt.dtype)
    y_sample = xs[1].reshape(bl, ll, D_MODEL).astype(x_sample.dtype)
    new_state_rwkv = jnp.stack(rw_new, axis=1)
    s5_fin = jnp.stack(s5_new, axis=0)
    s5_fin = jnp.moveaxis(s5_fin, 2, 0)
    new_re = s5_fin[..., :S5_NS].reshape(bc, DEPTH, 2, S5_GROUPS, S5_STATE)
    new_im = s5_fin[..., S5_NS:].reshape(bc, DEPTH, 2, S5_GROUPS, S5_STATE)
    return (y_prompt, y_sample, new_state_rwkv, new_re, new_im)
```

```python
import functools
import math

import jax
import jax.numpy as jnp
import numpy as np
from jax import lax
from jax.experimental import pallas as pl
from jax.experimental.pallas import tpu as pltpu

F32 = jnp.float32
BF16 = jnp.bfloat16
HI = lax.Precision.HIGHEST

D_MODEL = 1024
DEPTH = 2
GRID_W = 64
NORM_EPS = 1e-6

HY_WIDTH = 384
HY_FILTER_HIDDEN = 64
HY_N_BANDS = 8
HY_POS_DIM = 1 + 2 * HY_N_BANDS
HY_FAST_DECAY_PCT = 0.3
HY_SLOW_DECAY_PCT = 1.5
HY_DECAY_TARGET = 1e-2

RW_HEAD_DIM = 64
RW_HEADS = 6
RW_WIDTH = RW_HEADS * RW_HEAD_DIM
RW_GN_EPS = 64e-5
RW_CHUNK = 64

S5_GROUPS = 16
S5_GROUP_CH = 16
S5_WIDTH = S5_GROUPS * S5_GROUP_CH
S5_STATE = 64
S5_NS = S5_GROUPS * S5_STATE

N_EXPERTS = 16
N_EXPERT_GROUPS = 4
EXPERTS_PER_GROUP = N_EXPERTS // N_EXPERT_GROUPS
EXPERT_FF = 512

C_GATE = 3 * D_MODEL
C_HY = 3 * HY_WIDTH
C_RKV = 3 * RW_WIDTH
C_LORA = 384
C_S5 = S5_WIDTH
IN_COLS = C_GATE + C_HY + C_RKV + C_LORA + C_S5

V7X_VMEM_LIMIT = 56 * 1024 * 1024

TM_TOK = 256
TL_SEQ = 256


def _cparams(sem):
    return pltpu.CompilerParams(dimension_semantics=sem, vmem_limit_bytes=V7X_VMEM_LIMIT)


def _dot(a, b, precision=None):
    return jnp.dot(a, b, preferred_element_type=F32, precision=precision)


def _mod_kernel(c_ref, w_ref, b_ref, o_ref):
    c = c_ref[...]
    s = c * jax.nn.sigmoid(c)
    o_ref[...] = _dot(s, w_ref[...], HI) + b_ref[...]


def _modulation(cond8, ada_w, ada_b, l):
    tn = 1536
    return pl.pallas_call(
        _mod_kernel,
        grid=(6 * D_MODEL // tn,),
        in_specs=[
            pl.BlockSpec((8, D_MODEL), lambda j: (0, 0)),
            pl.BlockSpec((None, D_MODEL, tn), lambda j: (l, 0, j)),
            pl.BlockSpec((None, 1, tn), lambda j: (l, 0, j)),
        ],
        out_specs=pl.BlockSpec((8, tn), lambda j: (0, j)),
        out_shape=jax.ShapeDtypeStruct((8, 6 * D_MODEL), F32),
        compiler_params=_cparams(("arbitrary",)),
        name="modulation",
    )(cond8, ada_w, ada_b.reshape(DEPTH, 1, 6 * D_MODEL))


def _inproj_kernel(x_ref, sh_ref, sc_ref, g_ref, w_ref, og, ohy, orkv, olo, os5):
    x = x_ref[...]
    xn = x * lax.rsqrt(jnp.mean(x * x, axis=-1, keepdims=True) + NORM_EPS) * g_ref[...]
    xn = (xn * (1.0 + sc_ref[...]) + sh_ref[...]).astype(BF16)
    off = 0
    for o_ref, width in ((og, C_GATE), (ohy, C_HY), (orkv, C_RKV), (olo, C_LORA), (os5, C_S5)):
        o_ref[...] = _dot(xn, w_ref[:, off:off + width])
        off += width


def _mod_spec(which, tm, seq_len, n_mod):
    if n_mod == 1:
        return pl.BlockSpec((None, None, 1, D_MODEL), lambda i, *_: (0, which, 0, 0))
    return pl.BlockSpec((None, None, 1, D_MODEL), lambda i, *_: (i * tm // seq_len, which, 0, 0))


def _inproj(x, mod, norm_g, w_in_bf, l, seq_len):
    n_tok = x.shape[0]
    tm = TM_TOK
    widths = (C_GATE, C_HY, C_RKV, C_LORA, C_S5)
    return pl.pallas_call(
        _inproj_kernel,
        grid=(n_tok // tm,),
        in_specs=[
            pl.BlockSpec((tm, D_MODEL), lambda i: (i, 0)),
            _mod_spec(0, tm, seq_len, mod.shape[0]),
            _mod_spec(1, tm, seq_len, mod.shape[0]),
            pl.BlockSpec((None, 1, D_MODEL), lambda i: (l, 0, 0)),
            pl.BlockSpec((None, D_MODEL, IN_COLS), lambda i: (l, 0, 0)),
        ],
        out_specs=[pl.BlockSpec((tm, w), lambda i: (i, 0)) for w in widths],
        out_shape=[jax.ShapeDtypeStruct((n_tok, w), F32) for w in widths],
        compiler_params=_cparams(("parallel",)),
        name="inproj",
    )(x, mod, mod, norm_g.reshape(DEPTH, 1, D_MODEL), w_in_bf)


def _conv3(x, w_ref, b_ref, period):
    n = x.shape[0]
    t = lax.broadcasted_iota(jnp.int32, x.shape, 0) % period
    prev = jnp.where(t == 0, 0.0, pltpu.roll(x, 1, axis=0))
    nxt = jnp.where(t == period - 1, 0.0, pltpu.roll(x, n - 1, axis=0))
    return prev * w_ref[0:1, :] + x * w_ref[1:2, :] + nxt * w_ref[2:3, :] + b_ref[...]


def _hy_pre_kernel(hy_ref, cw_ref, cb_ref, bias_ref, vq_ref, e_ref, x1_ref, *, period):
    u = _conv3(hy_ref[...], cw_ref, cb_ref, period)
    x0, x1, v = u[:, :HY_WIDTH], u[:, HY_WIDTH:2 * HY_WIDTH], u[:, 2 * HY_WIDTH:]
    v = v * x0
    vq_ref[...] = v.astype(BF16)
    e_ref[...] = v * bias_ref[...]
    x1_ref[...] = x1


def _hy_pre(hy, conv_w, conv_b, bias, l, seq_len, batch, period):
    nt = seq_len // TL_SEQ
    blk = lambda w: pl.BlockSpec((None, TL_SEQ, w), lambda b, i: (b, i, 0))
    shp = lambda dt: jax.ShapeDtypeStruct((batch, seq_len, HY_WIDTH), dt)
    return pl.pallas_call(
        functools.partial(_hy_pre_kernel, period=period),
        grid=(batch, nt),
        in_specs=[
            blk(C_HY),
            pl.BlockSpec((None, 3, C_HY), lambda b, i: (l, 0, 0)),
            pl.BlockSpec((None, 1, C_HY), lambda b, i: (l, 0, 0)),
            pl.BlockSpec((None, 1, HY_WIDTH), lambda b, i: (l, 0, 0)),
        ],
        out_specs=[blk(HY_WIDTH), blk(HY_WIDTH), blk(HY_WIDTH)],
        out_shape=[shp(BF16), shp(F32), shp(F32)],
        compiler_params=_cparams(("parallel", "parallel")),
        name="hyena_pre",
    )(hy, conv_w, conv_b.reshape(DEPTH, 1, C_HY), bias.reshape(DEPTH, 1, HY_WIDTH))


def _hy_filter_kernel(feat_ref, w1_ref, b1_ref, f1_ref, w2_ref, b2_ref, f2_ref, w3_ref, dl_ref,
                      hs_ref, hd_ref, ny_ref):
    feats = feat_ref[...]
    h = jnp.sin(f1_ref[...] * (_dot(feats, w1_ref[...], HI) + b1_ref[...]))
    h = jnp.sin(f2_ref[...] * (_dot(h, w2_ref[...], HI) + b2_ref[...]))
    h = _dot(h, w3_ref[...], HI)
    h = h * jnp.exp(-feats[:, 0:1] * dl_ref[...])
    hf, hb = h[:, :HY_WIDTH], h[:, HY_WIDTH:]
    l1 = (jnp.sum(jnp.abs(hf), axis=0, keepdims=True)
          + jnp.sum(jnp.abs(hb), axis=0, keepdims=True) + 1e-6)
    hf = hf / l1
    hb = hb / l1
    row = lax.broadcasted_iota(jnp.int32, hb.shape, 0)
    hb0 = jnp.where(row == 0, 0.0, hb)
    hs = hf + hb0
    hs_ref[...] = hs.astype(BF16)
    hd_ref[...] = (hb0 - hf).astype(BF16)
    ny_ref[...] = jnp.sum(hs * _alternating(hs.shape), axis=0, keepdims=True) * (0.5 / hs.shape[0])


def _hy_filter(seq_len, w1, b1, f1, w2, b2, f2, w3):
    t = jnp.arange(seq_len, dtype=F32)
    t01 = t / max(seq_len - 1, 1)
    bands = jnp.linspace(1e-4, HY_N_BANDS - 1, HY_N_BANDS, dtype=F32)
    ang = (2.0 * math.pi / seq_len) * t[:, None] * bands[None, :]
    feats = jnp.concatenate([t01[:, None], jnp.cos(ang), -jnp.sin(ang)], axis=-1)
    feats = jnp.pad(feats, ((0, 0), (0, 128 - HY_POS_DIM)))
    w1p = jnp.pad(w1, ((0, 128 - HY_POS_DIM), (0, 0)))
    max_decay = math.log(HY_DECAY_TARGET) / HY_FAST_DECAY_PCT
    min_decay = math.log(HY_DECAY_TARGET) / HY_SLOW_DECAY_PCT
    deltas = jnp.abs(jnp.linspace(min_decay, max_decay, HY_WIDTH, dtype=F32))
    dl = jnp.concatenate([deltas, deltas])[None, :]
    hid = HY_FILTER_HIDDEN
    full = lambda shape: pl.BlockSpec(shape, lambda: tuple(0 for _ in shape))
    return pl.pallas_call(
        _hy_filter_kernel,
        in_specs=[full((seq_len, 128)), full((128, hid)), full((1, hid)), full((1, hid)),
                  full((hid, hid)), full((1, hid)), full((1, hid)), full((hid, 2 * HY_WIDTH)),
                  full((1, 2 * HY_WIDTH))],
        out_specs=[full((seq_len, HY_WIDTH)), full((seq_len, HY_WIDTH)), full((1, HY_WIDTH))],
        out_shape=[jax.ShapeDtypeStruct((seq_len, HY_WIDTH), BF16)] * 2
        + [jax.ShapeDtypeStruct((1, HY_WIDTH), F32)],
        compiler_params=pltpu.CompilerParams(vmem_limit_bytes=V7X_VMEM_LIMIT),
        name="hyena_filter",
    )(feats, w1p, b1[None, :], f1[None, :], w2, b2[None, :], f2[None, :], w3, dl)


def _dft_matrices(seq_len):
    n2 = 2 * seq_len
    k = jnp.arange(seq_len, dtype=jnp.int32)
    prod = (k[:, None] * k[None, :]) % n2
    ang = prod.astype(F32) * (2.0 * math.pi / n2)
    return jnp.stack([jnp.cos(ang), jnp.sin(ang)]).astype(BF16)


def _alternating(shape):
    t = lax.broadcasted_iota(jnp.int32, shape, 0)
    return (1 - 2 * (t % 2)).astype(F32)


def _hy_spec_kernel(hs_ref, hd_ref, f_ref, p_ref, q_ref, *, tk, seq_len):
    kre = _dot(f_ref[0], hs_ref[...])
    kim = _dot(f_ref[1], hd_ref[...])
    row = lax.broadcasted_iota(jnp.int32, kre.shape, 0) + pl.program_id(0) * tk
    w = jnp.where(row == 0, 0.5 / seq_len, 1.0 / seq_len)
    p_ref[...] = kre * w
    q_ref[...] = kim * w


def _hy_spec(hs, hd, fw, seq_len, tk):
    nf = seq_len // tk
    full = pl.BlockSpec((seq_len, HY_WIDTH), lambda f: (0, 0))
    tile = pl.BlockSpec((tk, HY_WIDTH), lambda f: (f, 0))
    return pl.pallas_call(
        functools.partial(_hy_spec_kernel, tk=tk, seq_len=seq_len),
        grid=(nf,),
        in_specs=[full, full, pl.BlockSpec((2, tk, seq_len), lambda f: (0, f, 0))],
        out_specs=[tile, tile],
        out_shape=[jax.ShapeDtypeStruct((seq_len, HY_WIDTH), F32)] * 2,
        compiler_params=_cparams(("parallel",)),
        name="hyena_spectrum",
    )(hs, hd, fw)


def _hy_conv_kernel(v_ref, p_ref, q_ref, ny_ref, f_ref, g_ref, e_ref, x1_ref, o_ref, acc_ref):
    f = pl.program_id(1)
    v = v_ref[...]

    @pl.when(f == 0)
    def _():
        alt = _alternating(acc_ref.shape)
        v_nyq = jnp.sum(v.astype(F32) * alt, axis=0, keepdims=True)
        acc_ref[...] = alt * (v_nyq * ny_ref[...])

    c = _dot(f_ref[0], v)
    s = _dot(f_ref[1], v)
    p, q = p_ref[...], q_ref[...]
    yre = (c * p + s * q).astype(BF16)
    yim_neg = (s * p - c * q).astype(BF16)
    acc_ref[...] += _dot(g_ref[0], yre) + _dot(g_ref[1], yim_neg)

    @pl.when(f == pl.num_programs(1) - 1)
    def _():
        o_ref[...] = (acc_ref[...] + e_ref[...]) * x1_ref[...]


def _hy_conv(vq, p, q, ny, fw, e, x1, seq_len, batch, tk):
    nf = seq_len // tk
    seq = pl.BlockSpec((None, seq_len, HY_WIDTH), lambda b, f: (b, 0, 0))
    tile = pl.BlockSpec((tk, HY_WIDTH), lambda b, f: (f, 0))
    return pl.pallas_call(
        _hy_conv_kernel,
        grid=(batch, nf),
        in_specs=[seq, tile, tile, pl.BlockSpec((1, HY_WIDTH), lambda b, f: (0, 0)),
                  pl.BlockSpec((2, tk, seq_len), lambda b, f: (0, f, 0)),
                  pl.BlockSpec((2, seq_len, tk), lambda b, f: (0, 0, f)),
                  seq, seq],
        out_specs=seq,
        out_shape=jax.ShapeDtypeStruct((batch, seq_len, HY_WIDTH), F32),
        scratch_shapes=[pltpu.VMEM((seq_len, HY_WIDTH), F32)],
        compiler_params=_cparams(("parallel", "arbitrary")),
        name="hyena_longconv",
    )(vq, p, q, ny, fw, fw, e, x1)


def _split_bf16(x):
    hi = x.astype(BF16)
    return hi, (x - hi.astype(F32)).astype(BF16)


def _dot3(a, b):
    ah, al = _split_bf16(a)
    bh, bl = _split_bf16(b)
    n = a.shape[0]
    both = _dot(jnp.concatenate([ah, al], axis=0), bh)
    return both[:n] + both[n:] + _dot(ah, bl)


def _dot_exact_rhs(a, b):
    ah, al = _split_bf16(a)
    n = a.shape[0]
    both = _dot(jnp.concatenate([ah, al], axis=0), b.astype(BF16))
    return both[:n] + both[n:]


def _dot_exact_lhs(a, b):
    bh, bl = _split_bf16(b)
    a16 = a.astype(BF16)
    return _dot(a16, bh) + _dot(a16, bl)


def _dot_nt(a, b):
    return lax.dot_general(a, b, (((1,), (1,)), ((), ())), preferred_element_type=F32)


def _rw_pre_kernel(rkv_ref, lo_ref, cw_ref, cb_ref, w0_ref, wup_ref, a0_ref, aup_ref, gup_ref,
                   kk_ref, ka_ref, rk_ref, hs_ref, tri_ref,
                   v_out, bonus_out, g_out, at_out, rh_out, bt_out, kt_out, wt_out,
                   *, period):
    T = RW_CHUNK
    u = _conv3(rkv_ref[...], cw_ref, cb_ref, period)
    r, k, v = u[:, :RW_WIDTH], u[:, RW_WIDTH:2 * RW_WIDTH], u[:, 2 * RW_WIDTH:]
    lo = lo_ref[...]
    wraw = w0_ref[...] + _dot3(jnp.tanh(lo[:, 0:128]), wup_ref[...])
    z = -wraw
    softplus = jnp.maximum(z, 0.0) + jnp.log(1.0 + jnp.exp(-jnp.abs(z)))
    logw = -jnp.exp(-softplus - 0.5)
    a = jax.nn.sigmoid(a0_ref[...] + _dot3(lo[:, 128:256], aup_ref[...]))
    g_out[...] = _dot3(jax.nn.sigmoid(lo[:, 256:384]), gup_ref[...])
    head_sum = hs_ref[...]
    kk = k * kk_ref[...]
    kk = kk * lax.rsqrt(jnp.maximum(_dot_exact_rhs(kk * kk, head_sum), 1e-24))
    v_out[...] = v
    n_chunk = r.shape[0] // T
    lane = lax.broadcasted_iota(jnp.int32, (RW_WIDTH, 128), 1)
    kt_sum = jnp.zeros_like(k)
    for d in range(2):
        a_d = a[:, d * RW_WIDTH:(d + 1) * RW_WIDTH]
        lw_d = logw[:, d * RW_WIDTH:(d + 1) * RW_WIDTH]
        kt_d = k * (1.0 + (a_d - 1.0) * ka_ref[...])
        kt_sum = kt_sum + kt_d
        cum = _dot_exact_lhs(tri_ref[d], lw_d)
        w_inv = jnp.exp(-cum)
        at_out[d] = -kk * jnp.exp(cum - lw_d)
        rh_out[d] = r * jnp.exp(cum)
        bt = (kk * a_d * w_inv).T
        ktt = (kt_d * w_inv).T
        for j in range(0, n_chunk, 2):
            for src, dst in ((bt, bt_out), (ktt, kt_out)):
                pair = src[:, j * T:(j + 2) * T]
                swapped = pltpu.roll(pair, T, axis=1)
                dst[d, j] = jnp.where(lane < T, pair, swapped)
                dst[d, j + 1] = jnp.where(lane < T, swapped, pair)
        for j in range(n_chunk):
            last = j * T + (T - 1 if d == 0 else 0)
            wt_out[d, j] = jnp.exp(cum[last:last + 1, :])
    bonus_out[...] = _dot_exact_rhs(r * kt_sum * rk_ref[...], head_sum) * v


def _chunk_tri(n_rows):
    i = np.arange(n_rows)
    same = (i[:, None] // RW_CHUNK) == (i[None, :] // RW_CHUNK)
    fwd = same & (i[None, :] <= i[:, None])
    bwd = same & (i[None, :] >= i[:, None])
    return jnp.asarray(np.stack([fwd, bwd]).astype(np.float32))


def _rw_pre(rkv2d, lora2d, P, l, seq_len, batch, period):
    nt = seq_len // TL_SEQ
    cpt = TL_SEQ // RW_CHUNK
    nchunk = seq_len // RW_CHUNK
    blk = lambda w: pl.BlockSpec((None, TL_SEQ, w), lambda b, i: (b, i, 0))
    par = lambda shape: pl.BlockSpec((None,) + shape, lambda b, i: (l,) + tuple(0 for _ in shape))
    row_shape = jax.ShapeDtypeStruct((batch, seq_len, RW_WIDTH), F32)
    dir_shape = jax.ShapeDtypeStruct((2, batch, seq_len, RW_WIDTH), F32)
    dir_blk = pl.BlockSpec((2, None, TL_SEQ, RW_WIDTH), lambda b, i: (0, b, i, 0))
    tr_shape = jax.ShapeDtypeStruct((2, batch, nchunk, RW_WIDTH, 128), F32)
    tr_blk = pl.BlockSpec((2, None, cpt, RW_WIDTH, 128), lambda b, i: (0, b, i, 0, 0))
    wt_shape = jax.ShapeDtypeStruct((2, batch, nchunk, 1, RW_WIDTH), F32)
    wt_blk = pl.BlockSpec((2, None, cpt, 1, RW_WIDTH), lambda b, i: (0, b, i, 0, 0))
    return pl.pallas_call(
        functools.partial(_rw_pre_kernel, period=period),
        grid=(batch, nt),
        in_specs=[
            blk(C_RKV), blk(C_LORA),
            par((3, C_RKV)), par((1, C_RKV)),
            par((1, 2 * RW_WIDTH)), par((128, 2 * RW_WIDTH)),
            par((1, 2 * RW_WIDTH)), par((128, 2 * RW_WIDTH)),
            par((128, RW_WIDTH)),
            par((1, RW_WIDTH)), par((1, RW_WIDTH)), par((1, RW_WIDTH)),
            pl.BlockSpec((RW_WIDTH, RW_WIDTH), lambda b, i: (0, 0)),
            pl.BlockSpec((2, TL_SEQ, TL_SEQ), lambda b, i: (0, 0, 0)),
        ],
        out_specs=[blk(RW_WIDTH)] * 3 + [dir_blk, dir_blk, tr_blk, tr_blk, wt_blk],
        out_shape=[row_shape] * 3 + [dir_shape, dir_shape, tr_shape, tr_shape, wt_shape],
        compiler_params=_cparams(("parallel", "parallel")),
        name="rwkv_pre",
    )(rkv2d, lora2d, P["rw_conv_w"], P["rw_conv_b3"], P["rw_w0c"], P["rw_wupc"], P["rw_a0c"],
      P["rw_aupc"], P["rw_g_up"], P["rw_k_k3"], P["rw_k_a3"], P["rw_r_k3"], P["head_sum"],
      _chunk_tri(TL_SEQ))


RW_PAIRS = RW_HEADS // 2
RW_PREP_CHUNKS = 4


def _rw_prep_kernel(at_ref, rh_ref, v_ref, bt_ref, kt_ref, whi_ref, wlo_ref, rt_ref, ot_ref):
    d = pl.program_id(0)
    T = RW_CHUNK
    sgn = 1 - 2 * d
    ti = lax.broadcasted_iota(jnp.int32, (T, 128), 0)
    tj = lax.broadcasted_iota(jnp.int32, (T, 128), 1) % T
    before = (tj - ti) * sgn < 0
    before_eq = (tj - ti) * sgn <= 0
    eye2 = (ti == tj).astype(F32)
    bi = lax.broadcasted_iota(jnp.int32, (128, 128), 0)
    bj = lax.broadcasted_iota(jnp.int32, (128, 128), 1)
    same_head = (bi // T) == (bj // T)
    eye128 = (bi == bj).astype(F32)

    def bdiag(x):
        return jnp.where(same_head, jnp.concatenate([x, x], axis=0), 0.0)

    probs = [(j, p) for j in range(RW_PREP_CHUNKS) for p in range(RW_PAIRS)]
    rows = lambda j: slice(j * T, (j + 1) * T)
    cols = lambda p: slice(p * 128, (p + 1) * 128)
    a_l = [at_ref[rows(j), cols(p)] for j, p in probs]
    r_l = [rh_ref[rows(j), cols(p)] for j, p in probs]
    bd_b = [jnp.where(same_head, bt_ref[j, cols(p), :], 0.0) for j, p in probs]
    bd_k = [jnp.where(same_head, kt_ref[j, cols(p), :], 0.0) for j, p in probs]
    gram = [_dot3(jnp.concatenate([a, r], axis=0), jnp.concatenate([b, k], axis=1))
            for a, r, b, k in zip(a_l, r_l, bd_b, bd_k)]
    l_ab = [jnp.where(before, g[:T, :128], 0.0) for g in gram]
    l_ak = [jnp.where(before, g[:T, 128:], 0.0) for g in gram]
    g_rb = [jnp.where(before_eq, g[T:, :128], 0.0) for g in gram]
    g_rk = [jnp.where(before_eq, g[T:, 128:], 0.0) for g in gram]
    m = [eye2 + l for l in l_ab]
    lp = [_dot3(l, bdiag(l)) for l in l_ab]
    n_sq = int(math.log2(T)) - 1
    for s in range(n_sq):
        if s < n_sq - 1:
            out = [_dot3(jnp.concatenate([mm, ll], axis=0), bdiag(ll)) for mm, ll in zip(m, lp)]
            m = [mm + o[:T] for mm, o in zip(m, out)]
            lp = [o[T:] for o in out]
        else:
            m = [mm + _dot3(mm, bdiag(ll)) for mm, ll in zip(m, lp)]
    x = [_dot3(mm, jnp.concatenate([bdiag(a), bdiag(l)], axis=1))
         for mm, a, l in zip(m, a_l, l_ak)]
    y = [_dot3(bt_ref[j, cols(p), 0:T], xx) for (j, p), xx in zip(probs, x)]
    bd_v = [bdiag(v_ref[rows(j), cols(p)]) for j, p in probs]
    psi = [_dot3(jnp.where(same_head, yy[:, 128:], 0.0) + k, v)
           for yy, k, v in zip(y, bd_k, bd_v)]
    for (j, p), yy, ps in zip(probs, y, psi):
        phi = jnp.where(same_head, eye128 + yy[:, :128], 0.0)
        w_hi, w_lo = _split_bf16(jnp.concatenate([phi, ps], axis=1))
        whi_ref[j, p] = w_hi
        wlo_ref[j, p] = w_lo
    z = [_dot(g.astype(BF16), jnp.concatenate([bdiag(xx[:, :128]), bdiag(xx[:, 128:])], axis=1).astype(BF16))
         for g, xx in zip(g_rb, x)]
    for (j, p), zz, r, g, v in zip(probs, z, r_l, g_rk, bd_v):
        rt_ref[rows(j), cols(p)] = (r + zz[:, :128]).astype(BF16)
        ot_ref[rows(j), cols(p)] = _dot((zz[:, 128:] + g).astype(BF16), v.astype(BF16))


def _rw_prep(at, rh, v, bt, kt, seq_len, batch):
    nchunk = seq_len // RW_CHUNK
    cb = RW_PREP_CHUNKS
    rows = cb * RW_CHUNK
    drow = pl.BlockSpec((None, None, rows, RW_WIDTH), lambda d, b, i: (d, b, i, 0))
    tr = pl.BlockSpec((None, None, cb, RW_WIDTH, 128), lambda d, b, i: (d, b, i, 0, 0))
    wsp = pl.BlockSpec((None, None, cb, RW_PAIRS, 128, 256), lambda d, b, i: (d, b, i, 0, 0, 0))
    w_shape = jax.ShapeDtypeStruct((2, batch, nchunk, RW_PAIRS, 128, 256), BF16)
    return pl.pallas_call(
        _rw_prep_kernel,
        grid=(2, batch, nchunk // cb),
        in_specs=[drow, drow, pl.BlockSpec((None, rows, RW_WIDTH), lambda d, b, i: (b, i, 0)), tr, tr],
        out_specs=[wsp, wsp, drow, drow],
        out_shape=[w_shape, w_shape,
                   jax.ShapeDtypeStruct((2, batch, seq_len, RW_WIDTH), BF16),
                   jax.ShapeDtypeStruct((2, batch, seq_len, RW_WIDTH), F32)],
        compiler_params=_cparams(("parallel", "parallel", "parallel")),
        name="rwkv_chunk_ops",
    )(at, rh, v, bt, kt)


def _rw_seq_kernel(whi_f, wlo_f, whi_b, wlo_b, rt_f, rt_b, ot_f, ot_b, wt_f, wt_b, s0_ref,
                   o_f, o_b, sfin_ref, st_ref, *, bb):
    c = pl.program_id(1)
    T = RW_CHUNK

    @pl.when(c == 0)
    def _():
        st_ref[...] = s0_ref[...]

    bi_ = lax.broadcasted_iota(jnp.int32, (128, 128), 0)
    bj_ = lax.broadcasted_iota(jnp.int32, (128, 128), 1)
    same_head = (bi_ // T) == (bj_ // T)
    ji = lax.broadcasted_iota(jnp.int32, (T, 128), 0)
    jj = lax.broadcasted_iota(jnp.int32, (T, 128), 1) % T
    eye2 = (ji == jj).astype(BF16)

    dirs = ((whi_f, wlo_f, rt_f, ot_f, wt_f, o_f), (whi_b, wlo_b, rt_b, ot_b, wt_b, o_b))
    probs = [(d, b, p) for d in range(2) for b in range(bb) for p in range(RW_PAIRS)]
    cols = lambda p: slice(p * 128, (p + 1) * 128)
    s_l =[st_ref[d, b, p] for d, b, p in probs]
    split = [_split_bf16(s) for s in s_l]
    lhs = [jnp.concatenate([hi, eye2], axis=1) for hi, _ in split]
    new = [_dot_nt(lh, dirs[d][0][b, p]) + _dot_nt(lh, dirs[d][1][b, p])
           + _dot_nt(lo, dirs[d][0][b, p, :, 0:128])
           for (d, b, p), lh, (_, lo) in zip(probs, lhs, split)]
    for (d, b, p), nw in zip(probs, new):
        st_ref[d, b, p] = nw * dirs[d][4][b, :, p * 128:(p + 1) * 128]
    for (d, b, p), s in zip(probs, s_l):
        _, _, rt, ot, _, o = dirs[d]
        bd_s = jnp.where(same_head, jnp.concatenate([s, s], axis=0), 0.0).astype(BF16)
        o[b, :, cols(p)] = _dot_nt(rt[b, :, cols(p)], bd_s) + ot[b, :, cols(p)]

    @pl.when(c == pl.num_programs(1) - 1)
    def _():
        sfin_ref[...] = st_ref[...]


def _rw_seq(whi, wlo, rt, ot, wt, s0p, seq_len, batch):
    nc = seq_len // RW_CHUNK
    bb = min(batch, 8)
    rev = lambda c: nc - 1 - c
    fwd = lambda c: c
    wsp = lambda d, ch: pl.BlockSpec((None, bb, None, RW_PAIRS, 128, 256),
                                     lambda g, c: (d, g, ch(c), 0, 0, 0))
    row = lambda d, ch: pl.BlockSpec((None, bb, RW_CHUNK, RW_WIDTH), lambda g, c: (d, g, ch(c), 0))
    wts = lambda d, ch: pl.BlockSpec((None, bb, None, 1, RW_WIDTH), lambda g, c: (d, g, ch(c), 0, 0))
    st = pl.BlockSpec((2, bb, RW_PAIRS, RW_HEAD_DIM, 128), lambda g, c: (0, g, 0, 0, 0))
    o_spec = lambda ch: pl.BlockSpec((bb, RW_CHUNK, RW_WIDTH), lambda g, c: (g, ch(c), 0))
    o_shape = jax.ShapeDtypeStruct((batch, seq_len, RW_WIDTH), F32)
    return pl.pallas_call(
        functools.partial(_rw_seq_kernel, bb=bb),
        grid=(batch // bb, nc),
        in_specs=[wsp(0, fwd), wsp(0, fwd), wsp(1, rev), wsp(1, rev),
                  row(0, fwd), row(1, rev), row(0, fwd), row(1, rev),
                  wts(0, fwd), wts(1, rev), st],
        out_specs=[o_spec(fwd), o_spec(rev), st],
        out_shape=[o_shape, o_shape,
                   jax.ShapeDtypeStruct((2, batch, RW_PAIRS, RW_HEAD_DIM, 128), F32)],
        scratch_shapes=[pltpu.VMEM((2, bb, RW_PAIRS, RW_HEAD_DIM, 128), F32)],
        compiler_params=_cparams(("parallel", "arbitrary")),
        name="rwkv_state_scan",
    )(whi, wlo, whi, wlo, rt, rt, ot, ot, wt, wt, s0p)


def _rw_pack_state(s):
    b = s.shape[0]
    s = s.reshape(b, 2, RW_PAIRS, 2, RW_HEAD_DIM, RW_HEAD_DIM)
    return jnp.transpose(s, (1, 0, 2, 4, 3, 5)).reshape(2, b, RW_PAIRS, RW_HEAD_DIM, 128)


def _rw_unpack_state(s):
    b = s.shape[1]
    s = s.reshape(2, b, RW_PAIRS, RW_HEAD_DIM, 2, RW_HEAD_DIM)
    return jnp.transpose(s, (1, 0, 2, 4, 3, 5)).reshape(b, 2, RW_HEADS, RW_HEAD_DIM, RW_HEAD_DIM)


def _s5_kernel(u_ref, wb_ref, wc_ref, a_ref, x0_ref, y_ref, fin_ref, bu_ref, st_ref, *, batch, steps):
    d = pl.program_id(0)
    i = pl.program_id(1)
    ns = S5_NS

    @pl.when(i == 0)
    def _():
        st_ref[...] = x0_ref[...]

    bu_ref[...] = _dot(u_ref[...].astype(BF16), wb_ref[...])

    if batch % 8 == 0:
        lanes = 256
        for lc in range(ns // lanes):
            re_sl = slice(lc * lanes, (lc + 1) * lanes)
            im_sl = slice(ns + lc * lanes, ns + (lc + 1) * lanes)
            ar, ai = a_ref[:, re_sl], a_ref[:, im_sl]

            def body(s, carry, re_sl=re_sl, im_sl=im_sl, ar=ar, ai=ai):
                xr, xi = carry
                step = s + d * (steps - 1 - 2 * s)
                rows = pl.ds(pl.multiple_of(step * batch, 8), batch)
                nr = ar * xr - ai * xi + bu_ref[rows, re_sl]
                ni = ar * xi + ai * xr + bu_ref[rows, im_sl]
                bu_ref[rows, re_sl] = nr
                bu_ref[rows, im_sl] = ni
                return nr, ni

            xr, xi = lax.fori_loop(0, steps, body, (st_ref[:, re_sl], st_ref[:, im_sl]))
            st_ref[:, re_sl] = xr
            st_ref[:, im_sl] = xi
    else:
        ar, ai = a_ref[:, :ns], a_ref[:, ns:]
        half = lax.broadcasted_iota(jnp.int32, (8, ns), 0) // 4
        first = half == d

        def body(s, carry):
            xr, xi = carry
            pair = s + d * (steps // 2 - 1 - 2 * s)
            rows = pl.ds(pl.multiple_of(pair * 8, 8), 8)
            br, bi = bu_ref[rows, :ns], bu_ref[rows, ns:]
            r1 = ar * xr - ai * xi + br
            i1 = ar * xi + ai * xr + bi
            r1s, i1s = pltpu.roll(r1, 4, axis=0), pltpu.roll(i1, 4, axis=0)
            r2 = ar * r1s - ai * i1s + br
            i2 = ar * i1s + ai * r1s + bi
            bu_ref[rows, :ns] = jnp.where(first, r1, r2)
            bu_ref[rows, ns:] = jnp.where(first, i1, i2)
            return pltpu.roll(r2, 4, axis=0), pltpu.roll(i2, 4, axis=0)

        xr, xi = lax.fori_loop(0, steps // 2, body, (st_ref[:, :ns], st_ref[:, ns:]))
        st_ref[:, :ns] = xr
        st_ref[:, ns:] = xi

    y_ref[...] = _dot(bu_ref[...].astype(BF16), wc_ref[...])

    @pl.when(i == pl.num_programs(1) - 1)
    def _():
        fin_ref[...] = st_ref[...]


def _s5_scan(u, wb, wc, a_bar, x0, l, seq_len, batch):
    rows_blk = 512
    steps = rows_blk // batch
    nblk = seq_len * batch // rows_blk
    rows = max(batch, 8)
    blk = lambda d, i: i + d * (nblk - 1 - 2 * i)
    return pl.pallas_call(
        functools.partial(_s5_kernel, batch=batch, steps=steps),
        grid=(2, nblk),
        in_specs=[
            pl.BlockSpec((rows_blk, S5_WIDTH), lambda d, i: (blk(d, i), 0)),
            pl.BlockSpec((None, None, S5_WIDTH, 2 * S5_NS), lambda d, i: (l, d, 0, 0)),
            pl.BlockSpec((None, None, 2 * S5_NS, S5_WIDTH), lambda d, i: (l, d, 0, 0)),
            pl.BlockSpec((None, None, 1, 2 * S5_NS), lambda d, i: (l, d, 0, 0)),
            pl.BlockSpec((None, rows, 2 * S5_NS), lambda d, i: (d, 0, 0)),
        ],
        out_specs=[
            pl.BlockSpec((None, rows_blk, S5_WIDTH), lambda d, i: (d, blk(d, i), 0)),
            pl.BlockSpec((None, rows, 2 * S5_NS), lambda d, i: (d, 0, 0)),
        ],
        out_shape=[jax.ShapeDtypeStruct((2, seq_len * batch, S5_WIDTH), F32),
                   jax.ShapeDtypeStruct((2, rows, 2 * S5_NS), F32)],
        scratch_shapes=[pltpu.VMEM((rows_blk, 2 * S5_NS), F32), pltpu.VMEM((rows, 2 * S5_NS), F32)],
        compiler_params=_cparams(("parallel", "arbitrary")),
        name="s5_scan",
    )(u, wb, wc, a_bar, x0)


def _gelu_tanh(x):
    return 0.5 * x * (1.0 + jnp.tanh(math.sqrt(2.0 / math.pi) * (x + 0.044715 * x * x * x)))


def _merge_kernel(x_ref, gate_ref, yhy_ref, of_ref, ob_ref, bonus_ref, g_ref, u5_ref, y5f_ref, y5b_ref,
                  gt1_ref, sh2_ref, sc2_ref,
                  hm_ref, gnw_ref, gnb_ref, s5d_ref, gluw_ref, glub_ref,
                  wbhy_ref, wbrw_ref, wbs5_ref, wout_ref, n2g_ref, rw_ref, rb_ref,
                  h_ref, hn_ref, gates_ref):
    o = of_ref[...] + ob_ref[...]
    head_mean = hm_ref[...]
    mu = _dot_exact_rhs(o, head_mean)
    oc = o - mu
    var = _dot_exact_rhs(oc * oc, head_mean)
    y_rw = (oc * lax.rsqrt(var + RW_GN_EPS) * gnw_ref[...] + gnb_ref[...] + bonus_ref[...]) * g_ref[...]
    y5 = _gelu_tanh(u5_ref[...] * s5d_ref[...] + y5f_ref[...] + y5b_ref[...])
    y5 = y5 * jax.nn.sigmoid(_dot(y5.astype(BF16), gluw_ref[...]) + glub_ref[...])
    gate = jax.nn.sigmoid(gate_ref[...])
    merged = (gate[:, :D_MODEL] * _dot(yhy_ref[...].astype(BF16), wbhy_ref[...])
              + gate[:, D_MODEL:2 * D_MODEL] * _dot(y_rw.astype(BF16), wbrw_ref[...])
              + gate[:, 2 * D_MODEL:] * _dot(y5.astype(BF16), wbs5_ref[...]))
    h = x_ref[...] + gt1_ref[...] * _dot(merged.astype(BF16), wout_ref[...])
    h_ref[...] = h
    hn = h * lax.rsqrt(jnp.mean(h * h, axis=-1, keepdims=True) + NORM_EPS) * n2g_ref[...]
    hn = hn * (1.0 + sc2_ref[...]) + sh2_ref[...]
    hn_ref[...] = hn.astype(BF16)

    scores = jax.nn.sigmoid(_dot3(hn, rw_ref[...]))
    sel = scores + rb_ref[...]
    scores_t, sel_t = scores.T, sel.T
    s_col = [sel_t[e:e + 1, :] for e in range(N_EXPERTS)]
    best_val = None
    best_grp = None
    for grp in range(N_EXPERT_GROUPS):
        a, b, c, dd = s_col[4 * grp:4 * grp + 4]
        hi1, lo1 = jnp.maximum(a, b), jnp.minimum(a, b)
        hi2, lo2 = jnp.maximum(c, dd), jnp.minimum(c, dd)
        m1 = jnp.maximum(hi1, hi2)
        m2 = jnp.maximum(jnp.minimum(hi1, hi2), jnp.where(hi1 >= hi2, lo1, lo2))
        val = m1 + m2
        if grp == 0:
            best_val, best_grp = val, jnp.zeros_like(val, dtype=jnp.int32)
        else:
            better = val > best_val
            best_val = jnp.where(better, val, best_val)
            best_grp = jnp.where(better, grp, best_grp)
    picked = []
    for e in range(N_EXPERTS):
        grp, pos = divmod(e, EXPERTS_PER_GROUP)
        rank = jnp.zeros_like(best_grp)
        for other in range(EXPERTS_PER_GROUP):
            if other == pos:
                continue
            so = s_col[4 * grp + other]
            ahead = (so > s_col[e]) if other > pos else (so >= s_col[e])
            rank = rank + ahead.astype(jnp.int32)
        take = jnp.logical_and(rank < 2, best_grp == grp)
        picked.append(jnp.where(take, scores_t[e:e + 1, :], 0.0))
    total = picked[0]
    for e in range(1, N_EXPERTS):
        total = total + picked[e]
    tm = gates_ref.shape[0]
    row = lax.broadcasted_iota(jnp.int32, (N_EXPERTS, tm), 0)
    gates_t = jnp.zeros((N_EXPERTS, tm), F32)
    for e in range(N_EXPERTS):
        gates_t = jnp.where(row == e, picked[e] / total, gates_t)
    gates_t = jnp.concatenate([gates_t, jnp.zeros((128 - N_EXPERTS, tm), F32)], axis=0)
    gates_ref[...] = gates_t.T


def _merge(x, gates_in, y_hy, o_f, o_b, bonus, g, u5, y5f, y5b, mod, P, l, seq_len):
    n_tok = x.shape[0]
    tm = TM_TOK
    tok = lambda w: pl.BlockSpec((tm, w), lambda i: (i, 0))
    n_mod = mod.shape[0]
    par = lambda shape: pl.BlockSpec((None,) + shape, lambda i: (l,) + tuple(0 for _ in shape))
    shared = lambda shape: pl.BlockSpec(shape, lambda i: tuple(0 for _ in shape))
    return pl.pallas_call(
        _merge_kernel,
        grid=(n_tok // tm,),
        in_specs=[
            tok(D_MODEL), tok(C_GATE), tok(HY_WIDTH), tok(RW_WIDTH), tok(RW_WIDTH), tok(RW_WIDTH),
            tok(RW_WIDTH), tok(S5_WIDTH), tok(S5_WIDTH), tok(S5_WIDTH),
            _mod_spec(2, tm, seq_len, n_mod), _mod_spec(3, tm, seq_len, n_mod), _mod_spec(4, tm, seq_len, n_mod),
            shared((RW_WIDTH, RW_WIDTH)), par((1, RW_WIDTH)), par((1, RW_WIDTH)),
            par((1, S5_WIDTH)), par((S5_WIDTH, S5_WIDTH)), par((1, S5_WIDTH)),
            par((HY_WIDTH, D_MODEL)), par((RW_WIDTH, D_MODEL)), par((S5_WIDTH, D_MODEL)),
            par((D_MODEL, D_MODEL)), par((1, D_MODEL)),
            shared((D_MODEL, 128)), shared((1, 128)),
        ],
        out_specs=[tok(D_MODEL), tok(D_MODEL), tok(128)],
        out_shape=[jax.ShapeDtypeStruct((n_tok, D_MODEL), F32),
                   jax.ShapeDtypeStruct((n_tok, D_MODEL), BF16),
                   jax.ShapeDtypeStruct((n_tok, 128), F32)],
        compiler_params=_cparams(("parallel",)),
        name="merge_router",
    )(x, gates_in, y_hy, o_f, o_b, bonus, g, u5, y5f, y5b, mod, mod, mod,
      P["head_mean"], P["rw_gn_w3"], P["rw_gn_b3"], P["s5_d3"], P["s5_glu_w_bf"], P["s5_glu_b3"],
      P["wb_hy_bf"], P["wb_rw_bf"], P["wb_s5_bf"], P["w_out_bf"], P["norm2_g3"],
      P["router_w_pad"], P["router_b_pad"])


def _moe_kernel(hn_ref, gates_ref, h_ref, gt2_ref, wg_ref, wu_ref, wd_ref, fg_ref, x_ref, acc_ref,
                *, final):
    e = pl.program_id(1)

    @pl.when(e == 0)
    def _():
        acc_ref[...] = jnp.zeros_like(acc_ref)

    hn = hn_ref[...]
    a = _dot(hn, wg_ref[...])
    he = (a * jax.nn.sigmoid(a)) * _dot(hn, wu_ref[...])
    lane = lax.broadcasted_iota(jnp.int32, gates_ref.shape, 1)
    gate = jnp.sum(jnp.where(lane == e, gates_ref[...], 0.0), axis=1, keepdims=True)
    acc_ref[...] += gate * _dot(he.astype(BF16), wd_ref[...])

    @pl.when(e == pl.num_programs(1) - 1)
    def _():
        x = h_ref[...] + gt2_ref[...] * acc_ref[...]
        if final:
            x = x * lax.rsqrt(jnp.mean(x * x, axis=-1, keepdims=True) + NORM_EPS) * fg_ref[...]
        x_ref[...] = x


def _moe(hn, gates, h, mod, wg, wu, wd, final_g, l, seq_len, final):
    n_tok = hn.shape[0]
    tm = 1024
    tok = lambda w: pl.BlockSpec((tm, w), lambda i, e: (i, 0))
    return pl.pallas_call(
        functools.partial(_moe_kernel, final=final),
        grid=(n_tok // tm, N_EXPERTS),
        in_specs=[
            tok(D_MODEL), tok(128), tok(D_MODEL),
            _mod_spec(5, tm, seq_len, mod.shape[0]),
            pl.BlockSpec((None, None, D_MODEL, EXPERT_FF), lambda i, e: (l, e, 0, 0)),
            pl.BlockSpec((None, None, D_MODEL, EXPERT_FF), lambda i, e: (l, e, 0, 0)),
            pl.BlockSpec((None, None, EXPERT_FF, D_MODEL), lambda i, e: (l, e, 0, 0)),
            pl.BlockSpec((1, D_MODEL), lambda i, e: (0, 0)),
        ],
        out_specs=tok(D_MODEL),
        out_shape=jax.ShapeDtypeStruct((n_tok, D_MODEL), F32),
        scratch_shapes=[pltpu.VMEM((tm, D_MODEL), F32)],
        compiler_params=_cparams(("parallel", "arbitrary")),
        name="moe",
    )(hn, gates, h, mod, wg, wu, wd, final_g.reshape(1, D_MODEL))


def _block_diag(blocks):
    g, r, c = blocks.shape[-3:]
    eye = jnp.eye(g, dtype=blocks.dtype)
    out = blocks[..., :, :, None, :] * eye[:, None, :, None]
    return out.reshape(blocks.shape[:-3] + (g * r, g * c))


def _s5_params(lam_re, lam_im, log_dt, b_re, b_im, c_re, c_im):
    lr, li = lam_re.astype(F32), lam_im.astype(F32)
    dt = jnp.exp(log_dt.astype(F32))[..., None]
    mag = jnp.exp(lr * dt)
    ar, ai = mag * jnp.cos(li * dt), mag * jnp.sin(li * dt)
    den = lr * lr + li * li
    qr = ((ar - 1.0) * lr + ai * li) / den
    qi = (ai * lr - (ar - 1.0) * li) / den
    bbr = qr[..., None] * b_re - qi[..., None] * b_im
    bbi = qr[..., None] * b_im + qi[..., None] * b_re
    tr = lambda m: jnp.swapaxes(m, -1, -2)
    wb = jnp.concatenate([_block_diag(tr(bbr)), _block_diag(tr(bbi))], axis=-1)
    wc = jnp.concatenate([_block_diag(tr(c_re)), -_block_diag(tr(c_im))], axis=-2)
    a_bar = jnp.concatenate([ar.reshape(DEPTH, 2, 1, S5_NS), ai.reshape(DEPTH, 2, 1, S5_NS)], axis=-1)
    return wb.astype(BF16), wc.astype(BF16), a_bar


def kernel(x_prompt, x_sample, state_rwkv, state_s5_re, state_s5_im, c, c_ctx, ada_w, ada_b, norm1_g, norm2_g, final_g, w_in, hy_conv_w, hy_conv_b, hy_f_w1, hy_f_b1, hy_f_freq1, hy_f_w2, hy_f_b2, hy_f_freq2, hy_f_w3, hy_bias, rw_conv_w, rw_conv_b, rw_w0, rw_w_up, rw_a0, rw_a_up, rw_g_up, rw_k_k, rw_k_a, rw_r_k, rw_gn_w, rw_gn_b, s5_lam_re, s5_lam_im, s5_log_dt, s5_b_re, s5_b_im, s5_c_re, s5_c_im, s5_d, s5_glu_w, s5_glu_b, wb_hy, wb_rw, wb_s5, w_out, router_w, router_b, exp_wg, exp_wu, exp_wd):
    bc, lc = x_prompt.shape[0], x_prompt.shape[1]
    bl, ll = x_sample.shape[0], x_sample.shape[1]
    n_ctx, n_lat = bc * lc, bl * ll
    assert n_ctx == n_lat and n_ctx % 1024 == 0 and bl == 4 and bc % 8 == 0
    streams = ((lc, bc, lc), (ll, bl, GRID_W))

    head_id = np.arange(RW_WIDTH) // RW_HEAD_DIM
    head_sum = (head_id[:, None] == head_id[None, :]).astype(np.float32)
    P = dict(
        rw_conv_w=rw_conv_w, rw_conv_b3=rw_conv_b.reshape(DEPTH, 1, C_RKV),
        rw_w0c=rw_w0.reshape(DEPTH, 1, 2 * RW_WIDTH),
        rw_wupc=_block_diag(rw_w_up), rw_a0c=rw_a0.reshape(DEPTH, 1, 2 * RW_WIDTH),
        rw_aupc=_block_diag(rw_a_up), rw_g_up=rw_g_up,
        rw_k_k3=rw_k_k.reshape(DEPTH, 1, RW_WIDTH), rw_k_a3=rw_k_a.reshape(DEPTH, 1, RW_WIDTH),
        rw_r_k3=rw_r_k.reshape(DEPTH, 1, RW_WIDTH),
        head_sum=jnp.asarray(head_sum), head_mean=jnp.asarray(head_sum / RW_HEAD_DIM),
        rw_gn_w3=rw_gn_w.reshape(DEPTH, 1, RW_WIDTH), rw_gn_b3=rw_gn_b.reshape(DEPTH, 1, RW_WIDTH),
        s5_d3=s5_d.reshape(DEPTH, 1, S5_WIDTH), s5_glu_w_bf=s5_glu_w.astype(BF16),
        s5_glu_b3=s5_glu_b.reshape(DEPTH, 1, S5_WIDTH),
        wb_hy_bf=wb_hy.astype(BF16), wb_rw_bf=wb_rw.astype(BF16), wb_s5_bf=wb_s5.astype(BF16),
        w_out_bf=w_out.astype(BF16), norm2_g3=norm2_g.reshape(DEPTH, 1, D_MODEL),
        router_w_pad=jnp.pad(router_w, ((0, 0), (0, 128 - N_EXPERTS))),
        router_b_pad=jnp.pad(router_b, (0, 128 - N_EXPERTS)).reshape(1, 128),
    )
    w_in_bf = w_in.astype(BF16)
    wg_bf, wu_bf, wd_bf = exp_wg.astype(BF16), exp_wu.astype(BF16), exp_wd.astype(BF16)
    s5_wb, s5_wc, s5_abar = _s5_params(s5_lam_re, s5_lam_im, s5_log_dt, s5_b_re, s5_b_im, s5_c_re, s5_c_im)
    dft = {seq_len: _dft_matrices(seq_len) for seq_len in (lc, ll)}

    xs = [x_prompt.astype(F32).reshape(n_ctx, D_MODEL), x_sample.astype(F32).reshape(n_lat, D_MODEL)]
    cond8 = jnp.zeros((8, D_MODEL), F32).at[:bl].set(c.astype(F32)).at[bl].set(c_ctx.astype(F32))

    rw_s0 = (jnp.zeros((bc, DEPTH, 2, RW_HEADS, RW_HEAD_DIM, RW_HEAD_DIM), F32),
             state_rwkv.astype(F32))
    s5_lat = jnp.concatenate([state_s5_re.reshape(bl, DEPTH, 2, S5_NS),
                              state_s5_im.reshape(bl, DEPTH, 2, S5_NS)], axis=-1).astype(F32)
    s5_lat = jnp.tile(jnp.moveaxis(s5_lat, 0, 2), (1, 1, 2, 1))
    s5_x0 = (jnp.zeros((DEPTH, 2, bc, 2 * S5_NS), F32), s5_lat)

    rw_new, s5_new = [], []
    for l in range(DEPTH):
        m = _modulation(cond8, ada_w, ada_b, l).reshape(8, 6, 1, D_MODEL)
        mods = (m[bl:bl + 1], m[:bl])
        for si, (seq_len, batch, period) in enumerate(streams):
            n_tok = seq_len * batch
            x, mod = xs[si], mods[si]
            g_in, hy_in, rkv_in, lo_in, s5_in = _inproj(x, mod, norm1_g, w_in_bf, l, seq_len)
            seq = lambda a: a.reshape(batch, seq_len, a.shape[-1])
            tok = lambda a: a.reshape(n_tok, a.shape[-1])
            vq, e, x1 = _hy_pre(seq(hy_in), hy_conv_w, hy_conv_b, hy_bias, l, seq_len, batch, period)
            hs, hd, ny = _hy_filter(seq_len, hy_f_w1[l], hy_f_b1[l], hy_f_freq1[l], hy_f_w2[l], hy_f_b2[l],
                                    hy_f_freq2[l], hy_f_w3[l])
            fw = dft[seq_len]
            tk = min(seq_len, 256)
            sp, sq = _hy_spec(hs, hd, fw, seq_len, tk)
            y_hy = _hy_conv(vq, sp, sq, ny, fw, e, x1, seq_len, batch, tk)
            v_rw, bon, g_rw, at, rh, bt, kt, wt = _rw_pre(seq(rkv_in), seq(lo_in), P, l, seq_len, batch, period)
            whi, wlo, rt, ot = _rw_prep(at, rh, v_rw, bt, kt, seq_len, batch)
            o_f, o_b, s_fin = _rw_seq(whi, wlo, rt, ot, wt, _rw_pack_state(rw_s0[si][:, l]), seq_len, batch)
            u_tm = jnp.swapaxes(seq(s5_in), 0, 1).reshape(n_tok, S5_WIDTH)
            y5d, fin5 = _s5_scan(u_tm, s5_wb, s5_wc, s5_abar, s5_x0[si][l], l, seq_len, batch)
            y5d = jnp.swapaxes(y5d.reshape(2, seq_len, batch, S5_WIDTH), 1, 2).reshape(2, n_tok, S5_WIDTH)
            if si == 0:
                rw_new.append(_rw_unpack_state(s_fin))
                s5_new.append(fin5)
            h, hn, gates = _merge(x, g_in, tok(y_hy), tok(o_f), tok(o_b), tok(bon), tok(g_rw), s5_in,
                                  y5d[0], y5d[1], mod, P, l, seq_len)
            xs[si] = _moe(hn, gates, h, mod, wg_bf, wu_bf, wd_bf, final_g, l, seq_len, final=(l == DEPTH - 1))

    y_prompt = xs[0].reshape(bc, lc, D_MODEL).astype(x_prompt.dtype)
    y_sample = xs[1].reshape(bl, ll, D_MODEL).astype(x_sample.dtype)
    new_state_rwkv = jnp.stack(rw_new, axis=1)
    s5_fin = jnp.stack(s5_new, axis=0)
    s5_fin = jnp.moveaxis(s5_fin, 2, 0)
    new_re = s5_fin[..., :S5_NS].reshape(bc, DEPTH, 2, S5_GROUPS, S5_STATE)
    new_im = s5_fin[..., S5_NS:].reshape(bc, DEPTH, 2, S5_GROUPS, S5_STATE)
    return (y_prompt, y_sample, new_state_rwkv, new_re, new_im)
```

```python
import functools
import math

import jax
import jax.numpy as jnp
import numpy as np
from jax import lax
from jax.experimental import pallas as pl
from jax.experimental.pallas import tpu as pltpu

F32 = jnp.float32
BF16 = jnp.bfloat16
HI = lax.Precision.HIGHEST

D_MODEL = 1024
DEPTH = 2
GRID_W = 64
NORM_EPS = 1e-6

HY_WIDTH = 384
HY_FILTER_HIDDEN = 64
HY_N_BANDS = 8
HY_POS_DIM = 1 + 2 * HY_N_BANDS
HY_FAST_DECAY_PCT = 0.3
HY_SLOW_DECAY_PCT = 1.5
HY_DECAY_TARGET = 1e-2

RW_HEAD_DIM = 64
RW_HEADS = 6
RW_WIDTH = RW_HEADS * RW_HEAD_DIM
RW_GN_EPS = 64e-5
RW_CHUNK = 64

S5_GROUPS = 16
S5_GROUP_CH = 16
S5_WIDTH = S5_GROUPS * S5_GROUP_CH
S5_STATE = 64
S5_NS = S5_GROUPS * S5_STATE

N_EXPERTS = 16
N_EXPERT_GROUPS = 4
EXPERTS_PER_GROUP = N_EXPERTS // N_EXPERT_GROUPS
EXPERT_FF = 512

C_GATE = 3 * D_MODEL
C_HY = 3 * HY_WIDTH
C_RKV = 3 * RW_WIDTH
C_LORA = 384
C_S5 = S5_WIDTH
IN_COLS = C_GATE + C_HY + C_RKV + C_LORA + C_S5

V7X_VMEM_LIMIT = 56 * 1024 * 1024

TM_TOK = 256
TL_SEQ = 256


def _cparams(sem):
    return pltpu.CompilerParams(dimension_semantics=sem, vmem_limit_bytes=V7X_VMEM_LIMIT)


def _dot(a, b, precision=None):
    return jnp.dot(a, b, preferred_element_type=F32, precision=precision)


def _mod_kernel(c_ref, w_ref, b_ref, o_ref):
    c = c_ref[...]
    s = c * jax.nn.sigmoid(c)
    o_ref[...] = _dot(s, w_ref[...], HI) + b_ref[...]


def _modulation(cond8, ada_w, ada_b, l):
    tn = 1536
    return pl.pallas_call(
        _mod_kernel,
        grid=(6 * D_MODEL // tn,),
        in_specs=[
            pl.BlockSpec((8, D_MODEL), lambda j: (0, 0)),
            pl.BlockSpec((None, D_MODEL, tn), lambda j: (l, 0, j)),
            pl.BlockSpec((None, 1, tn), lambda j: (l, 0, j)),
        ],
        out_specs=pl.BlockSpec((8, tn), lambda j: (0, j)),
        out_shape=jax.ShapeDtypeStruct((8, 6 * D_MODEL), F32),
        compiler_params=_cparams(("arbitrary",)),
        name="modulation",
    )(cond8, ada_w, ada_b.reshape(DEPTH, 1, 6 * D_MODEL))


def _inproj_kernel(x_ref, sh_ref, sc_ref, g_ref, w_ref, cw_ref, cb_ref, bias_ref,
                   og, ovq, oe, ox1, orkv, olo, os5, *, period):
    x = x_ref[...]
    xn = x * lax.rsqrt(jnp.mean(x * x, axis=-1, keepdims=True) + NORM_EPS) * g_ref[...]
    xn = (xn * (1.0 + sc_ref[...]) + sh_ref[...]).astype(BF16)
    off = 0
    for o_ref, width in ((og, C_GATE), (None, C_HY), (orkv, C_RKV), (olo, C_LORA), (os5, C_S5)):
        y = _dot(xn, w_ref[:, off:off + width])
        off += width
        if o_ref is not None:
            o_ref[...] = y
            continue
        u = _conv3(y, cw_ref, cb_ref, period)
        v = u[:, 2 * HY_WIDTH:] * u[:, :HY_WIDTH]
        ovq[...] = v.astype(BF16)
        oe[...] = v * bias_ref[...]
        ox1[...] = u[:, HY_WIDTH:2 * HY_WIDTH]


def _mod_spec(which, tm, seq_len, n_mod):
    if n_mod == 1:
        return pl.BlockSpec((None, None, 1, D_MODEL), lambda i, *_: (0, which, 0, 0))
    return pl.BlockSpec((None, None, 1, D_MODEL), lambda i, *_: (i * tm // seq_len, which, 0, 0))


def _inproj(x, mod, norm_g3, w_in_bf, hy_conv_w, hy_conv_b3, hy_bias3, l, seq_len, period):
    n_tok = x.shape[0]
    tm = TM_TOK
    assert seq_len % tm == 0 and tm % period == 0
    outs = ((C_GATE, F32), (HY_WIDTH, BF16), (HY_WIDTH, F32), (HY_WIDTH, F32),
            (C_RKV, F32), (C_LORA, F32), (C_S5, F32))
    par = lambda shape: pl.BlockSpec((None,) + shape, lambda i: (l,) + tuple(0 for _ in shape))
    return pl.pallas_call(
        functools.partial(_inproj_kernel, period=period),
        grid=(n_tok // tm,),
        in_specs=[
            pl.BlockSpec((tm, D_MODEL), lambda i: (i, 0)),
            _mod_spec(0, tm, seq_len, mod.shape[0]),
            _mod_spec(1, tm, seq_len, mod.shape[0]),
            par((1, D_MODEL)), par((D_MODEL, IN_COLS)),
            par((3, C_HY)), par((1, C_HY)), par((1, HY_WIDTH)),
        ],
        out_specs=[pl.BlockSpec((tm, w), lambda i: (i, 0)) for w, _ in outs],
        out_shape=[jax.ShapeDtypeStruct((n_tok, w), dt) for w, dt in outs],
        compiler_params=_cparams(("parallel",)),
        name="inproj",
    )(x, mod, mod, norm_g3, w_in_bf, hy_conv_w, hy_conv_b3, hy_bias3)


def _conv3(x, w_ref, b_ref, period):
    n = x.shape[0]
    t = lax.broadcasted_iota(jnp.int32, x.shape, 0) % period
    prev = jnp.where(t == 0, 0.0, pltpu.roll(x, 1, axis=0))
    nxt = jnp.where(t == period - 1, 0.0, pltpu.roll(x, n - 1, axis=0))
    return prev * w_ref[0:1, :] + x * w_ref[1:2, :] + nxt * w_ref[2:3, :] + b_ref[...]


def _hy_filter_kernel(feat_ref, w1_ref, b1_ref, f1_ref, w2_ref, b2_ref, f2_ref, w3_ref, dl_ref,
                      hs_ref, hd_ref, ny_ref):
    feats = feat_ref[...]
    h = jnp.sin(f1_ref[...] * (_dot(feats, w1_ref[...], HI) + b1_ref[...]))
    h = jnp.sin(f2_ref[...] * (_dot(h, w2_ref[...], HI) + b2_ref[...]))
    h = _dot(h, w3_ref[...], HI)
    h = h * jnp.exp(-feats[:, 0:1] * dl_ref[...])
    hf, hb = h[:, :HY_WIDTH], h[:, HY_WIDTH:]
    l1 = (jnp.sum(jnp.abs(hf), axis=0, keepdims=True)
          + jnp.sum(jnp.abs(hb), axis=0, keepdims=True) + 1e-6)
    hf = hf / l1
    hb = hb / l1
    row = lax.broadcasted_iota(jnp.int32, hb.shape, 0)
    hb0 = jnp.where(row == 0, 0.0, hb)
    hs = hf + hb0
    hs_ref[...] = hs.astype(BF16)
    hd_ref[...] = (hb0 - hf).astype(BF16)
    ny_ref[...] = jnp.sum(hs * _alternating(hs.shape), axis=0, keepdims=True) * (0.5 / hs.shape[0])


def _hy_positions(seq_len):
    t = np.arange(seq_len, dtype=np.float32)
    t01 = t / np.float32(max(seq_len - 1, 1))
    bands = np.linspace(1e-4, HY_N_BANDS - 1, HY_N_BANDS, dtype=np.float32)
    ang = np.float32(2.0 * math.pi / seq_len) * t[:, None] * bands[None, :]
    feats = np.concatenate([t01[:, None], np.cos(ang), -np.sin(ang)], axis=-1).astype(np.float32)
    feats = np.pad(feats, ((0, 0), (0, 128 - HY_POS_DIM)))
    max_decay = math.log(HY_DECAY_TARGET) / HY_FAST_DECAY_PCT
    min_decay = math.log(HY_DECAY_TARGET) / HY_SLOW_DECAY_PCT
    deltas = np.abs(np.linspace(min_decay, max_decay, HY_WIDTH, dtype=np.float32))
    return jnp.asarray(feats), jnp.asarray(np.concatenate([deltas, deltas])[None, :])


def _hy_filter(seq_len, l, w1p, b1, f1, w2, b2, f2, w3):
    feats, dl = _hy_positions(seq_len)
    hid = HY_FILTER_HIDDEN
    full = lambda shape: pl.BlockSpec(shape, lambda i: tuple(0 for _ in shape))
    par = lambda shape: pl.BlockSpec((None,) + shape, lambda i: (l,) + tuple(0 for _ in shape))
    return pl.pallas_call(
        _hy_filter_kernel,
        grid=(1,),
        in_specs=[full((seq_len, 128)), par((128, hid)), par((1, hid)), par((1, hid)),
                  par((hid, hid)), par((1, hid)), par((1, hid)), par((hid, 2 * HY_WIDTH)),
                  full((1, 2 * HY_WIDTH))],
        out_specs=[full((seq_len, HY_WIDTH)), full((seq_len, HY_WIDTH)), full((1, HY_WIDTH))],
        out_shape=[jax.ShapeDtypeStruct((seq_len, HY_WIDTH), BF16)] * 2
        + [jax.ShapeDtypeStruct((1, HY_WIDTH), F32)],
        compiler_params=pltpu.CompilerParams(vmem_limit_bytes=V7X_VMEM_LIMIT),
        name="hyena_filter",
    )(feats, w1p, b1, f1, w2, b2, f2, w3, dl)


def _dft_matrices(seq_len):
    n2 = 2 * seq_len
    k = np.arange(seq_len, dtype=np.int64)
    ang = ((k[:, None] * k[None, :]) % n2).astype(np.float64) * (2.0 * math.pi / n2)
    return jnp.asarray(np.stack([np.cos(ang), np.sin(ang)]).astype(np.float32)).astype(BF16)


def _alternating(shape):
    t = lax.broadcasted_iota(jnp.int32, shape, 0)
    return (1 - 2 * (t % 2)).astype(F32)


def _hy_spec_kernel(hs_ref, hd_ref, f_ref, p_ref, q_ref, *, tk, seq_len):
    kre = _dot(f_ref[0], hs_ref[...])
    kim = _dot(f_ref[1], hd_ref[...])
    row = lax.broadcasted_iota(jnp.int32, kre.shape, 0) + pl.program_id(0) * tk
    w = jnp.where(row == 0, 0.5 / seq_len, 1.0 / seq_len)
    p_ref[...] = kre * w
    q_ref[...] = kim * w


def _hy_spec(hs, hd, fw, seq_len, tk):
    nf = seq_len // tk
    full = pl.BlockSpec((seq_len, HY_WIDTH), lambda f: (0, 0))
    tile = pl.BlockSpec((tk, HY_WIDTH), lambda f: (f, 0))
    return pl.pallas_call(
        functools.partial(_hy_spec_kernel, tk=tk, seq_len=seq_len),
        grid=(nf,),
        in_specs=[full, full, pl.BlockSpec((2, tk, seq_len), lambda f: (0, f, 0))],
        out_specs=[tile, tile],
        out_shape=[jax.ShapeDtypeStruct((seq_len, HY_WIDTH), F32)] * 2,
        compiler_params=_cparams(("parallel",)),
        name="hyena_spectrum",
    )(hs, hd, fw)


def _hy_conv_kernel(v_ref, p_ref, q_ref, ny_ref, f_ref, g_ref, e_ref, x1_ref, o_ref, acc_ref):
    f = pl.program_id(1)
    v = v_ref[...]

    @pl.when(f == 0)
    def _():
        alt = _alternating(acc_ref.shape)
        v_nyq = jnp.sum(v.astype(F32) * alt, axis=0, keepdims=True)
        acc_ref[...] = alt * (v_nyq * ny_ref[...])

    c = _dot(f_ref[0], v)
    s = _dot(f_ref[1], v)
    p, q = p_ref[...], q_ref[...]
    yre = (c * p + s * q).astype(BF16)
    yim_neg = (s * p - c * q).astype(BF16)
    acc_ref[...] += _dot(g_ref[0], yre) + _dot(g_ref[1], yim_neg)

    @pl.when(f == pl.num_programs(1) - 1)
    def _():
        o_ref[...] = (acc_ref[...] + e_ref[...]) * x1_ref[...]


def _hy_conv(vq, p, q, ny, fw, e, x1, seq_len, batch, tk):
    nf = seq_len // tk
    seq = pl.BlockSpec((None, seq_len, HY_WIDTH), lambda b, f: (b, 0, 0))
    tile = pl.BlockSpec((tk, HY_WIDTH), lambda b, f: (f, 0))
    return pl.pallas_call(
        _hy_conv_kernel,
        grid=(batch, nf),
        in_specs=[seq, tile, tile, pl.BlockSpec((1, HY_WIDTH), lambda b, f: (0, 0)),
                  pl.BlockSpec((2, tk, seq_len), lambda b, f: (0, f, 0)),
                  pl.BlockSpec((2, seq_len, tk), lambda b, f: (0, 0, f)),
                  seq, seq],
        out_specs=seq,
        out_shape=jax.ShapeDtypeStruct((batch, seq_len, HY_WIDTH), F32),
        scratch_shapes=[pltpu.VMEM((seq_len, HY_WIDTH), F32)],
        compiler_params=_cparams(("parallel", "arbitrary")),
        name="hyena_longconv",
    )(vq, p, q, ny, fw, fw, e, x1)


def _split_bf16(x):
    hi = x.astype(BF16)
    return hi, (x - hi.astype(F32)).astype(BF16)


def _dot3(a, b):
    ah, al = _split_bf16(a)
    bh, bl = _split_bf16(b)
    n = a.shape[0]
    both = _dot(jnp.concatenate([ah, al], axis=0), bh)
    return both[:n] + both[n:] + _dot(ah, bl)


def _dot_exact_rhs(a, b):
    ah, al = _split_bf16(a)
    n = a.shape[0]
    both = _dot(jnp.concatenate([ah, al], axis=0), b.astype(BF16))
    return both[:n] + both[n:]


def _dot_exact_lhs(a, b):
    bh, bl = _split_bf16(b)
    a16 = a.astype(BF16)
    return _dot(a16, bh) + _dot(a16, bl)


def _dot_nt(a, b):
    return lax.dot_general(a, b, (((1,), (1,)), ((), ())), preferred_element_type=F32)


def _rw_pre_kernel(rkv_ref, lo_ref, cw_ref, cb_ref, w0_ref, wup_ref, a0_ref, aup_ref, gup_ref,
                   kk_ref, ka_ref, rk_ref, hs_ref, tri_ref,
                   v_out, bonus_out, g_out, at_out, rh_out, bt_out, kt_out, wt_out,
                   *, period):
    T = RW_CHUNK
    u = _conv3(rkv_ref[...], cw_ref, cb_ref, period)
    r, k, v = u[:, :RW_WIDTH], u[:, RW_WIDTH:2 * RW_WIDTH], u[:, 2 * RW_WIDTH:]
    lo = lo_ref[...]
    wraw = w0_ref[...] + _dot3(jnp.tanh(lo[:, 0:128]), wup_ref[...])
    z = -wraw
    softplus = jnp.maximum(z, 0.0) + jnp.log(1.0 + jnp.exp(-jnp.abs(z)))
    logw = -jnp.exp(-softplus - 0.5)
    a = jax.nn.sigmoid(a0_ref[...] + _dot3(lo[:, 128:256], aup_ref[...]))
    g_out[...] = _dot3(jax.nn.sigmoid(lo[:, 256:384]), gup_ref[...])
    head_sum = hs_ref[...]
    kk = k * kk_ref[...]
    kk = kk * lax.rsqrt(jnp.maximum(_dot_exact_rhs(kk * kk, head_sum), 1e-24))
    v_out[...] = v
    n_chunk = r.shape[0] // T
    lane = lax.broadcasted_iota(jnp.int32, (RW_WIDTH, 128), 1)
    kt_sum = jnp.zeros_like(k)
    for d in range(2):
        a_d = a[:, d * RW_WIDTH:(d + 1) * RW_WIDTH]
        lw_d = logw[:, d * RW_WIDTH:(d + 1) * RW_WIDTH]
        kt_d = k * (1.0 + (a_d - 1.0) * ka_ref[...])
        kt_sum = kt_sum + kt_d
        cum = _dot_exact_lhs(tri_ref[d], lw_d)
        w_inv = jnp.exp(-cum)
        at_out[d] = -kk * jnp.exp(cum - lw_d)
        rh_out[d] = r * jnp.exp(cum)
        bt = (kk * a_d * w_inv).T
        ktt = (kt_d * w_inv).T
        for j in range(0, n_chunk, 2):
            for src, dst in ((bt, bt_out), (ktt, kt_out)):
                pair = src[:, j * T:(j + 2) * T]
                swapped = pltpu.roll(pair, T, axis=1)
                dst[d, j] = jnp.where(lane < T, pair, swapped)
                dst[d, j + 1] = jnp.where(lane < T, swapped, pair)
        for j in range(n_chunk):
            last = j * T + (T - 1 if d == 0 else 0)
            wt_out[d, j] = jnp.exp(cum[last:last + 1, :])
    bonus_out[...] = _dot_exact_rhs(r * kt_sum * rk_ref[...], head_sum) * v


def _chunk_tri(n_rows):
    i = np.arange(n_rows)
    same = (i[:, None] // RW_CHUNK) == (i[None, :] // RW_CHUNK)
    fwd = same & (i[None, :] <= i[:, None])
    bwd = same & (i[None, :] >= i[:, None])
    return jnp.asarray(np.stack([fwd, bwd]).astype(np.float32))


def _rw_pre(rkv2d, lora2d, P, l, seq_len, batch, period):
    nt = seq_len // TL_SEQ
    cpt = TL_SEQ // RW_CHUNK
    nchunk = seq_len // RW_CHUNK
    blk = lambda w: pl.BlockSpec((None, TL_SEQ, w), lambda b, i: (b, i, 0))
    par = lambda shape: pl.BlockSpec((None,) + shape, lambda b, i: (l,) + tuple(0 for _ in shape))
    row_shape = jax.ShapeDtypeStruct((batch, seq_len, RW_WIDTH), F32)
    dir_shape = jax.ShapeDtypeStruct((2, batch, seq_len, RW_WIDTH), F32)
    dir_blk = pl.BlockSpec((2, None, TL_SEQ, RW_WIDTH), lambda b, i: (0, b, i, 0))
    tr_shape = jax.ShapeDtypeStruct((2, batch, nchunk, RW_WIDTH, 128), F32)
    tr_blk = pl.BlockSpec((2, None, cpt, RW_WIDTH, 128), lambda b, i: (0, b, i, 0, 0))
    wt_shape = jax.ShapeDtypeStruct((2, batch, nchunk, 1, RW_WIDTH), F32)
    wt_blk = pl.BlockSpec((2, None, cpt, 1, RW_WIDTH), lambda b, i: (0, b, i, 0, 0))
    return pl.pallas_call(
        functools.partial(_rw_pre_kernel, period=period),
        grid=(batch, nt),
        in_specs=[
            blk(C_RKV), blk(C_LORA),
            par((3, C_RKV)), par((1, C_RKV)),
            par((1, 2 * RW_WIDTH)), par((128, 2 * RW_WIDTH)),
            par((1, 2 * RW_WIDTH)), par((128, 2 * RW_WIDTH)),
            par((128, RW_WIDTH)),
            par((1, RW_WIDTH)), par((1, RW_WIDTH)), par((1, RW_WIDTH)),
            pl.BlockSpec((RW_WIDTH, RW_WIDTH), lambda b, i: (0, 0)),
            pl.BlockSpec((2, TL_SEQ, TL_SEQ), lambda b, i: (0, 0, 0)),
        ],
        out_specs=[blk(RW_WIDTH)] * 3 + [dir_blk, dir_blk, tr_blk, tr_blk, wt_blk],
        out_shape=[row_shape] * 3 + [dir_shape, dir_shape, tr_shape, tr_shape, wt_shape],
        compiler_params=_cparams(("parallel", "parallel")),
        name="rwkv_pre",
    )(rkv2d, lora2d, P["rw_conv_w"], P["rw_conv_b3"], P["rw_w0c"], P["rw_wupc"], P["rw_a0c"],
      P["rw_aupc"], P["rw_g_up"], P["rw_k_k3"], P["rw_k_a3"], P["rw_r_k3"], P["head_sum"],
      _chunk_tri(TL_SEQ))


RW_PAIRS = RW_HEADS // 2
RW_PREP_CHUNKS = 4


def _rw_prep_kernel(at_ref, rh_ref, v_ref, bt_ref, kt_ref, whi_ref, wlo_ref, rt_ref, ot_ref):
    d = pl.program_id(0)
    T = RW_CHUNK
    sgn = 1 - 2 * d
    ti = lax.broadcasted_iota(jnp.int32, (T, 128), 0)
    tj = lax.broadcasted_iota(jnp.int32, (T, 128), 1) % T
    before = (tj - ti) * sgn < 0
    before_eq = (tj - ti) * sgn <= 0
    eye2 = (ti == tj).astype(F32)
    bi = lax.broadcasted_iota(jnp.int32, (128, 128), 0)
    bj = lax.broadcasted_iota(jnp.int32, (128, 128), 1)
    same_head = (bi // T) == (bj // T)
    eye128 = (bi == bj).astype(F32)

    def bdiag(x):
        return jnp.where(same_head, jnp.concatenate([x, x], axis=0), 0.0)

    probs = [(j, p) for j in range(RW_PREP_CHUNKS) for p in range(RW_PAIRS)]
    rows = lambda j: slice(j * T, (j + 1) * T)
    cols = lambda p: slice(p * 128, (p + 1) * 128)
    a_l = [at_ref[rows(j), cols(p)] for j, p in probs]
    r_l = [rh_ref[rows(j), cols(p)] for j, p in probs]
    bd_b = [jnp.where(same_head, bt_ref[j, cols(p), :], 0.0) for j, p in probs]
    bd_k = [jnp.where(same_head, kt_ref[j, cols(p), :], 0.0) for j, p in probs]
    gram = [_dot3(jnp.concatenate([a, r], axis=0), jnp.concatenate([b, k], axis=1))
            for a, r, b, k in zip(a_l, r_l, bd_b, bd_k)]
    l_ab = [jnp.where(before, g[:T, :128], 0.0) for g in gram]
    l_ak = [jnp.where(before, g[:T, 128:], 0.0) for g in gram]
    g_rb = [jnp.where(before_eq, g[T:, :128], 0.0) for g in gram]
    g_rk = [jnp.where(before_eq, g[T:, 128:], 0.0) for g in gram]
    m = [eye2 + l for l in l_ab]
    lp = [_dot3(l, bdiag(l)) for l in l_ab]
    n_sq = int(math.log2(T)) - 1
    for s in range(n_sq):
        if s < n_sq - 1:
            out = [_dot3(jnp.concatenate([mm, ll], axis=0), bdiag(ll)) for mm, ll in zip(m, lp)]
            m = [mm + o[:T] for mm, o in zip(m, out)]
            lp = [o[T:] for o in out]
        else:
            m = [mm + _dot3(mm, bdiag(ll)) for mm, ll in zip(m, lp)]
    x = [_dot3(mm, jnp.concatenate([bdiag(a), bdiag(l)], axis=1))
         for mm, a, l in zip(m, a_l, l_ak)]
    y = [_dot3(bt_ref[j, cols(p), 0:T], xx) for (j, p), xx in zip(probs, x)]
    bd_v = [bdiag(v_ref[rows(j), cols(p)]) for j, p in probs]
    psi = [_dot3(jnp.where(same_head, yy[:, 128:], 0.0) + k, v)
           for yy, k, v in zip(y, bd_k, bd_v)]
    for (j, p), yy, ps in zip(probs, y, psi):
        phi = jnp.where(same_head, eye128 + yy[:, :128], 0.0)
        w_hi, w_lo = _split_bf16(jnp.concatenate([phi, ps], axis=1))
        whi_ref[j, p] = w_hi
        wlo_ref[j, p] = w_lo
    z = [_dot(g.astype(BF16), jnp.concatenate([bdiag(xx[:, :128]), bdiag(xx[:, 128:])], axis=1).astype(BF16))
         for g, xx in zip(g_rb, x)]
    for (j, p), zz, r, g, v in zip(probs, z, r_l, g_rk, bd_v):
        rt_ref[rows(j), cols(p)] = (r + zz[:, :128]).astype(BF16)
        ot_ref[rows(j), cols(p)] = _dot((zz[:, 128:] + g).astype(BF16), v.astype(BF16))


def _rw_prep(at, rh, v, bt, kt, seq_len, batch):
    nchunk = seq_len // RW_CHUNK
    cb = RW_PREP_CHUNKS
    rows = cb * RW_CHUNK
    drow = pl.BlockSpec((None, None, rows, RW_WIDTH), lambda d, b, i: (d, b, i, 0))
    tr = pl.BlockSpec((None, None, cb, RW_WIDTH, 128), lambda d, b, i: (d, b, i, 0, 0))
    wsp = pl.BlockSpec((None, None, cb, RW_PAIRS, 128, 256), lambda d, b, i: (d, b, i, 0, 0, 0))
    w_shape = jax.ShapeDtypeStruct((2, batch, nchunk, RW_PAIRS, 128, 256), BF16)
    return pl.pallas_call(
        _rw_prep_kernel,
        grid=(2, batch, nchunk // cb),
        in_specs=[drow, drow, pl.BlockSpec((None, rows, RW_WIDTH), lambda d, b, i: (b, i, 0)), tr, tr],
        out_specs=[wsp, wsp, drow, drow],
        out_shape=[w_shape, w_shape,
                   jax.ShapeDtypeStruct((2, batch, seq_len, RW_WIDTH), BF16),
                   jax.ShapeDtypeStruct((2, batch, seq_len, RW_WIDTH), F32)],
        compiler_params=_cparams(("parallel", "parallel", "parallel")),
        name="rwkv_chunk_ops",
    )(at, rh, v, bt, kt)


def _rw_seq_kernel(whi_f, wlo_f, whi_b, wlo_b, rt_f, rt_b, ot_f, ot_b, wt_f, wt_b, s0_ref,
                   o_f, o_b, sfin_ref, st_ref, *, bb):
    c = pl.program_id(1)
    T = RW_CHUNK

    @pl.when(c == 0)
    def _():
        st_ref[...] = s0_ref[...]

    bi_ = lax.broadcasted_iota(jnp.int32, (128, 128), 0)
    bj_ = lax.broadcasted_iota(jnp.int32, (128, 128), 1)
    same_head = (bi_ // T) == (bj_ // T)
    ji = lax.broadcasted_iota(jnp.int32, (T, 128), 0)
    jj = lax.broadcasted_iota(jnp.int32, (T, 128), 1) % T
    eye2 = (ji == jj).astype(BF16)

    dirs = ((whi_f, wlo_f, rt_f, ot_f, wt_f, o_f), (whi_b, wlo_b, rt_b, ot_b, wt_b, o_b))
    probs = [(d, b, p) for d in range(2) for b in range(bb) for p in range(RW_PAIRS)]
    cols = lambda p: slice(p * 128, (p + 1) * 128)
    s_l =[st_ref[d, b, p] for d, b, p in probs]
    split = [_split_bf16(s) for s in s_l]
    lhs = [jnp.concatenate([hi, eye2], axis=1) for hi, _ in split]
    new = [_dot_nt(lh, dirs[d][0][b, p]) + _dot_nt(lh, dirs[d][1][b, p])
           + _dot_nt(lo, dirs[d][0][b, p, :, 0:128])
           for (d, b, p), lh, (_, lo) in zip(probs, lhs, split)]
    for (d, b, p), nw in zip(probs, new):
        st_ref[d, b, p] = nw * dirs[d][4][b, :, p * 128:(p + 1) * 128]
    for (d, b, p), s in zip(probs, s_l):
        _, _, rt, ot, _, o = dirs[d]
        bd_s = jnp.where(same_head, jnp.concatenate([s, s], axis=0), 0.0).astype(BF16)
        o[b, :, cols(p)] = _dot_nt(rt[b, :, cols(p)], bd_s) + ot[b, :, cols(p)]

    @pl.when(c == pl.num_programs(1) - 1)
    def _():
        sfin_ref[...] = st_ref[...]


def _rw_seq(whi, wlo, rt, ot, wt, s0p, seq_len, batch):
    nc = seq_len // RW_CHUNK
    bb = min(batch, 8)
    rev = lambda c: nc - 1 - c
    fwd = lambda c: c
    wsp = lambda d, ch: pl.BlockSpec((None, bb, None, RW_PAIRS, 128, 256),
                                     lambda g, c: (d, g, ch(c), 0, 0, 0))
    row = lambda d, ch: pl.BlockSpec((None, bb, RW_CHUNK, RW_WIDTH), lambda g, c: (d, g, ch(c), 0))
    wts = lambda d, ch: pl.BlockSpec((None, bb, None, 1, RW_WIDTH), lambda g, c: (d, g, ch(c), 0, 0))
    st = pl.BlockSpec((2, bb, RW_PAIRS, RW_HEAD_DIM, 128), lambda g, c: (0, g, 0, 0, 0))
    o_spec = lambda ch: pl.BlockSpec((bb, RW_CHUNK, RW_WIDTH), lambda g, c: (g, ch(c), 0))
    o_shape = jax.ShapeDtypeStruct((batch, seq_len, RW_WIDTH), F32)
    return pl.pallas_call(
        functools.partial(_rw_seq_kernel, bb=bb),
        grid=(batch // bb, nc),
        in_specs=[wsp(0, fwd), wsp(0, fwd), wsp(1, rev), wsp(1, rev),
                  row(0, fwd), row(1, rev), row(0, fwd), row(1, rev),
                  wts(0, fwd), wts(1, rev), st],
        out_specs=[o_spec(fwd), o_spec(rev), st],
        out_shape=[o_shape, o_shape,
                   jax.ShapeDtypeStruct((2, batch, RW_PAIRS, RW_HEAD_DIM, 128), F32)],
        scratch_shapes=[pltpu.VMEM((2, bb, RW_PAIRS, RW_HEAD_DIM, 128), F32)],
        compiler_params=_cparams(("parallel", "arbitrary")),
        name="rwkv_state_scan",
    )(whi, wlo, whi, wlo, rt, rt, ot, ot, wt, wt, s0p)


def _rw_pack_state(s):
    b = s.shape[0]
    s = s.reshape(b, 2, RW_PAIRS, 2, RW_HEAD_DIM, RW_HEAD_DIM)
    return jnp.transpose(s, (1, 0, 2, 4, 3, 5)).reshape(2, b, RW_PAIRS, RW_HEAD_DIM, 128)


def _rw_unpack_state(s):
    b = s.shape[1]
    s = s.reshape(2, b, RW_PAIRS, RW_HEAD_DIM, 2, RW_HEAD_DIM)
    return jnp.transpose(s, (1, 0, 2, 4, 3, 5)).reshape(b, 2, RW_HEADS, RW_HEAD_DIM, RW_HEAD_DIM)


def _s5_kernel(u_ref, wb_ref, wc_ref, a_ref, x0_ref, y_ref, fin_ref, bu_ref, st_ref, tm_ref, *, batch, steps):
    d = pl.program_id(0)
    i = pl.program_id(1)
    ns = S5_NS
    halves = S5_WIDTH // 128

    @pl.when(i == 0)
    def _():
        st_ref[...] = x0_ref[...]

    for b in range(batch):
        for k in range(halves):
            tm_ref[k, pl.ds(b, steps, stride=batch), :] = u_ref[b, :, k * 128:(k + 1) * 128]
    u_tm = jnp.concatenate([tm_ref[k] for k in range(halves)], axis=1)
    bu_ref[...] = _dot(u_tm.astype(BF16), wb_ref[...])

    if batch % 8 == 0:
        lanes = 256
        for lc in range(ns // lanes):
            re_sl = slice(lc * lanes, (lc + 1) * lanes)
            im_sl = slice(ns + lc * lanes, ns + (lc + 1) * lanes)
            ar, ai = a_ref[:, re_sl], a_ref[:, im_sl]

            def body(s, carry, re_sl=re_sl, im_sl=im_sl, ar=ar, ai=ai):
                xr, xi = carry
                step = s + d * (steps - 1 - 2 * s)
                rows = pl.ds(pl.multiple_of(step * batch, 8), batch)
                nr = ar * xr - ai * xi + bu_ref[rows, re_sl]
                ni = ar * xi + ai * xr + bu_ref[rows, im_sl]
                bu_ref[rows, re_sl] = nr
                bu_ref[rows, im_sl] = ni
                return nr, ni

            xr, xi = lax.fori_loop(0, steps, body, (st_ref[:, re_sl], st_ref[:, im_sl]))
            st_ref[:, re_sl] = xr
            st_ref[:, im_sl] = xi
    else:
        ar, ai = a_ref[:, :ns], a_ref[:, ns:]
        half = lax.broadcasted_iota(jnp.int32, (8, ns), 0) // 4
        first = half == d

        def body(s, carry):
            xr, xi = carry
            pair = s + d * (steps // 2 - 1 - 2 * s)
            rows = pl.ds(pl.multiple_of(pair * 8, 8), 8)
            br, bi = bu_ref[rows, :ns], bu_ref[rows, ns:]
            r1 = ar * xr - ai * xi + br
            i1 = ar * xi + ai * xr + bi
            r1s, i1s = pltpu.roll(r1, 4, axis=0), pltpu.roll(i1, 4, axis=0)
            r2 = ar * r1s - ai * i1s + br
            i2 = ar * i1s + ai * r1s + bi
            bu_ref[rows, :ns] = jnp.where(first, r1, r2)
            bu_ref[rows, ns:] = jnp.where(first, i1, i2)
            return pltpu.roll(r2, 4, axis=0), pltpu.roll(i2, 4, axis=0)

        xr, xi = lax.fori_loop(0, steps // 2, body, (st_ref[:, :ns], st_ref[:, ns:]))
        st_ref[:, :ns] = xr
        st_ref[:, ns:] = xi

    y = _dot(bu_ref[...].astype(BF16), wc_ref[...])
    for k in range(halves):
        tm_ref[k] = y[:, k * 128:(k + 1) * 128]
    for b in range(batch):
        for k in range(halves):
            y_ref[b, :, k * 128:(k + 1) * 128] = tm_ref[k, pl.ds(b, steps, stride=batch), :]

    @pl.when(i == pl.num_programs(1) - 1)
    def _():
        fin_ref[...] = st_ref[...]


def _s5_scan(u, wb, wc, a_bar, x0, l, seq_len, batch):
    rows_blk = 512
    steps = rows_blk // batch
    nblk = seq_len * batch // rows_blk
    rows = max(batch, 8)
    blk = lambda d, i: i + d * (nblk - 1 - 2 * i)
    return pl.pallas_call(
        functools.partial(_s5_kernel, batch=batch, steps=steps),
        grid=(2, nblk),
        in_specs=[
            pl.BlockSpec((batch, steps, S5_WIDTH), lambda d, i: (0, blk(d, i), 0)),
            pl.BlockSpec((None, None, S5_WIDTH, 2 * S5_NS), lambda d, i: (l, d, 0, 0)),
            pl.BlockSpec((None, None, 2 * S5_NS, S5_WIDTH), lambda d, i: (l, d, 0, 0)),
            pl.BlockSpec((None, None, 1, 2 * S5_NS), lambda d, i: (l, d, 0, 0)),
            pl.BlockSpec((None, rows, 2 * S5_NS), lambda d, i: (d, 0, 0)),
        ],
        out_specs=[
            pl.BlockSpec((None, batch, steps, S5_WIDTH), lambda d, i: (d, 0, blk(d, i), 0)),
            pl.BlockSpec((None, rows, 2 * S5_NS), lambda d, i: (d, 0, 0)),
        ],
        out_shape=[jax.ShapeDtypeStruct((2, batch, seq_len, S5_WIDTH), F32),
                   jax.ShapeDtypeStruct((2, rows, 2 * S5_NS), F32)],
        scratch_shapes=[pltpu.VMEM((rows_blk, 2 * S5_NS), F32), pltpu.VMEM((rows, 2 * S5_NS), F32),
                        pltpu.VMEM((S5_WIDTH // 128, rows_blk, 128), F32)],
        compiler_params=_cparams(("parallel", "arbitrary")),
        name="s5_scan",
    )(u, wb, wc, a_bar, x0)


def _gelu_tanh(x):
    return 0.5 * x * (1.0 + jnp.tanh(math.sqrt(2.0 / math.pi) * (x + 0.044715 * x * x * x)))


def _merge_kernel(x_ref, gate_ref, yhy_ref, of_ref, ob_ref, bonus_ref, g_ref, u5_ref, y5f_ref, y5b_ref,
                  gt1_ref, sh2_ref, sc2_ref,
                  hm_ref, gnw_ref, gnb_ref, s5d_ref, gluw_ref, glub_ref,
                  wbhy_ref, wbrw_ref, wbs5_ref, wout_ref, n2g_ref, rw_ref, rb_ref,
                  h_ref, hn_ref, gates_ref):
    o = of_ref[...] + ob_ref[...]
    head_mean = hm_ref[...]
    mu = _dot_exact_rhs(o, head_mean)
    oc = o - mu
    var = _dot_exact_rhs(oc * oc, head_mean)
    y_rw = (oc * lax.rsqrt(var + RW_GN_EPS) * gnw_ref[...] + gnb_ref[...] + bonus_ref[...]) * g_ref[...]
    y5 = _gelu_tanh(u5_ref[...] * s5d_ref[...] + y5f_ref[...] + y5b_ref[...])
    y5 = y5 * jax.nn.sigmoid(_dot(y5.astype(BF16), gluw_ref[...]) + glub_ref[...])
    gate = jax.nn.sigmoid(gate_ref[...])
    merged = (gate[:, :D_MODEL] * _dot(yhy_ref[...].astype(BF16), wbhy_ref[...])
              + gate[:, D_MODEL:2 * D_MODEL] * _dot(y_rw.astype(BF16), wbrw_ref[...])
              + gate[:, 2 * D_MODEL:] * _dot(y5.astype(BF16), wbs5_ref[...]))
    h = x_ref[...] + gt1_ref[...] * _dot(merged.astype(BF16), wout_ref[...])
    h_ref[...] = h
    hn = h * lax.rsqrt(jnp.mean(h * h, axis=-1, keepdims=True) + NORM_EPS) * n2g_ref[...]
    hn = hn * (1.0 + sc2_ref[...]) + sh2_ref[...]
    hn_ref[...] = hn.astype(BF16)

    scores = jax.nn.sigmoid(_dot3(hn, rw_ref[...]))
    sel = scores + rb_ref[...]
    scores_t, sel_t = scores.T, sel.T
    s_col = [sel_t[e:e + 1, :] for e in range(N_EXPERTS)]
    best_val = None
    best_grp = None
    for grp in range(N_EXPERT_GROUPS):
        a, b, c, dd = s_col[4 * grp:4 * grp + 4]
        hi1, lo1 = jnp.maximum(a, b), jnp.minimum(a, b)
        hi2, lo2 = jnp.maximum(c, dd), jnp.minimum(c, dd)
        m1 = jnp.maximum(hi1, hi2)
        m2 = jnp.maximum(jnp.minimum(hi1, hi2), jnp.where(hi1 >= hi2, lo1, lo2))
        val = m1 + m2
        if grp == 0:
            best_val, best_grp = val, jnp.zeros_like(val, dtype=jnp.int32)
        else:
            better = val > best_val
            best_val = jnp.where(better, val, best_val)
            best_grp = jnp.where(better, grp, best_grp)
    picked = []
    for e in range(N_EXPERTS):
        grp, pos = divmod(e, EXPERTS_PER_GROUP)
        rank = jnp.zeros_like(best_grp)
        for other in range(EXPERTS_PER_GROUP):
            if other == pos:
                continue
            so = s_col[4 * grp + other]
            ahead = (so > s_col[e]) if other > pos else (so >= s_col[e])
            rank = rank + ahead.astype(jnp.int32)
        take = jnp.logical_and(rank < 2, best_grp == grp)
        picked.append(jnp.where(take, scores_t[e:e + 1, :], 0.0))
    total = picked[0]
    for e in range(1, N_EXPERTS):
        total = total + picked[e]
    tm = gates_ref.shape[0]
    row = lax.broadcasted_iota(jnp.int32, (N_EXPERTS, tm), 0)
    gates_t = jnp.zeros((N_EXPERTS, tm), F32)
    for e in range(N_EXPERTS):
        gates_t = jnp.where(row == e, picked[e] / total, gates_t)
    gates_t = jnp.concatenate([gates_t, jnp.zeros((128 - N_EXPERTS, tm), F32)], axis=0)
    gates_ref[...] = gates_t.T


def _merge(x, gates_in, y_hy, o_f, o_b, bonus, g, u5, y5f, y5b, mod, P, l, seq_len):
    n_tok = x.shape[0]
    tm = TM_TOK
    tok = lambda w: pl.BlockSpec((tm, w), lambda i: (i, 0))
    n_mod = mod.shape[0]
    par = lambda shape: pl.BlockSpec((None,) + shape, lambda i: (l,) + tuple(0 for _ in shape))
    shared = lambda shape: pl.BlockSpec(shape, lambda i: tuple(0 for _ in shape))
    return pl.pallas_call(
        _merge_kernel,
        grid=(n_tok // tm,),
        in_specs=[
            tok(D_MODEL), tok(C_GATE), tok(HY_WIDTH), tok(RW_WIDTH), tok(RW_WIDTH), tok(RW_WIDTH),
            tok(RW_WIDTH), tok(S5_WIDTH), tok(S5_WIDTH), tok(S5_WIDTH),
            _mod_spec(2, tm, seq_len, n_mod), _mod_spec(3, tm, seq_len, n_mod), _mod_spec(4, tm, seq_len, n_mod),
            shared((RW_WIDTH, RW_WIDTH)), par((1, RW_WIDTH)), par((1, RW_WIDTH)),
            par((1, S5_WIDTH)), par((S5_WIDTH, S5_WIDTH)), par((1, S5_WIDTH)),
            par((HY_WIDTH, D_MODEL)), par((RW_WIDTH, D_MODEL)), par((S5_WIDTH, D_MODEL)),
            par((D_MODEL, D_MODEL)), par((1, D_MODEL)),
            shared((D_MODEL, 128)), shared((1, 128)),
        ],
        out_specs=[tok(D_MODEL), tok(D_MODEL), tok(128)],
        out_shape=[jax.ShapeDtypeStruct((n_tok, D_MODEL), F32),
                   jax.ShapeDtypeStruct((n_tok, D_MODEL), BF16),
                   jax.ShapeDtypeStruct((n_tok, 128), F32)],
        compiler_params=_cparams(("parallel",)),
        name="merge_router",
    )(x, gates_in, y_hy, o_f, o_b, bonus, g, u5, y5f, y5b, mod, mod, mod,
      P["head_mean"], P["rw_gn_w3"], P["rw_gn_b3"], P["s5_d3"], P["s5_glu_w_bf"], P["s5_glu_b3"],
      P["wb_hy_bf"], P["wb_rw_bf"], P["wb_s5_bf"], P["w_out_bf"], P["norm2_g3"],
      P["router_w_pad"], P["router_b_pad"])


def _moe_kernel(hn_ref, gates_ref, h_ref, gt2_ref, wg_ref, wu_ref, wd_ref, fg_ref, x_ref, acc_ref,
                *, final):
    e = pl.program_id(1)

    @pl.when(e == 0)
    def _():
        acc_ref[...] = jnp.zeros_like(acc_ref)

    hn = hn_ref[...]
    a = _dot(hn, wg_ref[...].astype(BF16))
    he = (a * jax.nn.sigmoid(a)) * _dot(hn, wu_ref[...].astype(BF16))
    lane = lax.broadcasted_iota(jnp.int32, gates_ref.shape, 1)
    gate = jnp.sum(jnp.where(lane == e, gates_ref[...], 0.0), axis=1, keepdims=True)
    acc_ref[...] += gate * _dot(he.astype(BF16), wd_ref[...].astype(BF16))

    @pl.when(e == pl.num_programs(1) - 1)
    def _():
        x = h_ref[...] + gt2_ref[...] * acc_ref[...]
        if final:
            x = x * lax.rsqrt(jnp.mean(x * x, axis=-1, keepdims=True) + NORM_EPS) * fg_ref[...]
        x_ref[...] = x


def _moe(hn, gates, h, mod, wg, wu, wd, final_g, l, seq_len, final):
    n_tok = hn.shape[0]
    tm = 1024
    tok = lambda w: pl.BlockSpec((tm, w), lambda i, e: (i, 0))
    return pl.pallas_call(
        functools.partial(_moe_kernel, final=final),
        grid=(n_tok // tm, N_EXPERTS),
        in_specs=[
            tok(D_MODEL), tok(128), tok(D_MODEL),
            _mod_spec(5, tm, seq_len, mod.shape[0]),
            pl.BlockSpec((None, None, D_MODEL, EXPERT_FF), lambda i, e: (l, e, 0, 0)),
            pl.BlockSpec((None, None, D_MODEL, EXPERT_FF), lambda i, e: (l, e, 0, 0)),
            pl.BlockSpec((None, None, EXPERT_FF, D_MODEL), lambda i, e: (l, e, 0, 0)),
            pl.BlockSpec((1, D_MODEL), lambda i, e: (0, 0)),
        ],
        out_specs=tok(D_MODEL),
        out_shape=jax.ShapeDtypeStruct((n_tok, D_MODEL), F32),
        scratch_shapes=[pltpu.VMEM((tm, D_MODEL), F32)],
        compiler_params=_cparams(("parallel", "arbitrary")),
        name="moe",
    )(hn, gates, h, mod, wg, wu, wd, final_g.reshape(1, D_MODEL))


def _block_diag(blocks):
    g, r, c = blocks.shape[-3:]
    eye = jnp.eye(g, dtype=blocks.dtype)
    out = blocks[..., :, :, None, :] * eye[:, None, :, None]
    return out.reshape(blocks.shape[:-3] + (g * r, g * c))


def _s5_params(lam_re, lam_im, log_dt, b_re, b_im, c_re, c_im):
    lr, li = lam_re.astype(F32), lam_im.astype(F32)
    dt = jnp.exp(log_dt.astype(F32))[..., None]
    mag = jnp.exp(lr * dt)
    ar, ai = mag * jnp.cos(li * dt), mag * jnp.sin(li * dt)
    den = lr * lr + li * li
    qr = ((ar - 1.0) * lr + ai * li) / den
    qi = (ai * lr - (ar - 1.0) * li) / den
    bbr = qr[..., None] * b_re - qi[..., None] * b_im
    bbi = qr[..., None] * b_im + qi[..., None] * b_re
    tr = lambda m: jnp.swapaxes(m, -1, -2)
    wb = jnp.concatenate([_block_diag(tr(bbr)), _block_diag(tr(bbi))], axis=-1)
    wc = jnp.concatenate([_block_diag(tr(c_re)), -_block_diag(tr(c_im))], axis=-2)
    a_bar = jnp.concatenate([ar.reshape(DEPTH, 2, 1, S5_NS), ai.reshape(DEPTH, 2, 1, S5_NS)], axis=-1)
    return wb.astype(BF16), wc.astype(BF16), a_bar


def kernel(x_prompt, x_sample, state_rwkv, state_s5_re, state_s5_im, c, c_ctx, ada_w, ada_b, norm1_g, norm2_g, final_g, w_in, hy_conv_w, hy_conv_b, hy_f_w1, hy_f_b1, hy_f_freq1, hy_f_w2, hy_f_b2, hy_f_freq2, hy_f_w3, hy_bias, rw_conv_w, rw_conv_b, rw_w0, rw_w_up, rw_a0, rw_a_up, rw_g_up, rw_k_k, rw_k_a, rw_r_k, rw_gn_w, rw_gn_b, s5_lam_re, s5_lam_im, s5_log_dt, s5_b_re, s5_b_im, s5_c_re, s5_c_im, s5_d, s5_glu_w, s5_glu_b, wb_hy, wb_rw, wb_s5, w_out, router_w, router_b, exp_wg, exp_wu, exp_wd):
    bc, lc = x_prompt.shape[0], x_prompt.shape[1]
    bl, ll = x_sample.shape[0], x_sample.shape[1]
    n_ctx, n_lat = bc * lc, bl * ll
    assert n_ctx == n_lat and n_ctx % 1024 == 0 and bl == 4 and bc % 8 == 0
    streams = ((lc, bc, lc), (ll, bl, GRID_W))

    head_id = np.arange(RW_WIDTH) // RW_HEAD_DIM
    head_sum = (head_id[:, None] == head_id[None, :]).astype(np.float32)
    P = dict(
        rw_conv_w=rw_conv_w, rw_conv_b3=rw_conv_b.reshape(DEPTH, 1, C_RKV),
        rw_w0c=rw_w0.reshape(DEPTH, 1, 2 * RW_WIDTH),
        rw_wupc=_block_diag(rw_w_up), rw_a0c=rw_a0.reshape(DEPTH, 1, 2 * RW_WIDTH),
        rw_aupc=_block_diag(rw_a_up), rw_g_up=rw_g_up,
        rw_k_k3=rw_k_k.reshape(DEPTH, 1, RW_WIDTH), rw_k_a3=rw_k_a.reshape(DEPTH, 1, RW_WIDTH),
        rw_r_k3=rw_r_k.reshape(DEPTH, 1, RW_WIDTH),
        head_sum=jnp.asarray(head_sum), head_mean=jnp.asarray(head_sum / RW_HEAD_DIM),
        rw_gn_w3=rw_gn_w.reshape(DEPTH, 1, RW_WIDTH), rw_gn_b3=rw_gn_b.reshape(DEPTH, 1, RW_WIDTH),
        s5_d3=s5_d.reshape(DEPTH, 1, S5_WIDTH), s5_glu_w_bf=s5_glu_w.astype(BF16),
        s5_glu_b3=s5_glu_b.reshape(DEPTH, 1, S5_WIDTH),
        wb_hy_bf=wb_hy.astype(BF16), wb_rw_bf=wb_rw.astype(BF16), wb_s5_bf=wb_s5.astype(BF16),
        w_out_bf=w_out.astype(BF16), norm2_g3=norm2_g.reshape(DEPTH, 1, D_MODEL),
        router_w_pad=jnp.pad(router_w, ((0, 0), (0, 128 - N_EXPERTS))),
        router_b_pad=jnp.pad(router_b, (0, 128 - N_EXPERTS)).reshape(1, 128),
    )
    w_in_bf = w_in.astype(BF16)
    norm1_g3 = norm1_g.reshape(DEPTH, 1, D_MODEL)
    hy_conv_b3, hy_bias3 = hy_conv_b.reshape(DEPTH, 1, C_HY), hy_bias.reshape(DEPTH, 1, HY_WIDTH)
    row3 = lambda a: a.reshape(DEPTH, 1, a.shape[-1])
    hy_filter_params = (jnp.pad(hy_f_w1, ((0, 0), (0, 128 - HY_POS_DIM), (0, 0))), row3(hy_f_b1),
                        row3(hy_f_freq1), hy_f_w2, row3(hy_f_b2), row3(hy_f_freq2), hy_f_w3)
    s5_wb, s5_wc, s5_abar = _s5_params(s5_lam_re, s5_lam_im, s5_log_dt, s5_b_re, s5_b_im, s5_c_re, s5_c_im)
    dft = {seq_len: _dft_matrices(seq_len) for seq_len in (lc, ll)}

    xs = [x_prompt.astype(F32).reshape(n_ctx, D_MODEL), x_sample.astype(F32).reshape(n_lat, D_MODEL)]
    cond8 = jnp.zeros((8, D_MODEL), F32).at[:bl].set(c.astype(F32)).at[bl].set(c_ctx.astype(F32))

    rw_s0 = (jnp.zeros((bc, DEPTH, 2, RW_HEADS, RW_HEAD_DIM, RW_HEAD_DIM), F32),
             state_rwkv.astype(F32))
    s5_lat = jnp.concatenate([state_s5_re.reshape(bl, DEPTH, 2, S5_NS),
                              state_s5_im.reshape(bl, DEPTH, 2, S5_NS)], axis=-1).astype(F32)
    s5_lat = jnp.tile(jnp.moveaxis(s5_lat, 0, 2), (1, 1, 2, 1))
    s5_x0 = (jnp.zeros((DEPTH, 2, bc, 2 * S5_NS), F32), s5_lat)

    rw_new, s5_new = [], []
    for l in range(DEPTH):
        m = _modulation(cond8, ada_w, ada_b, l).reshape(8, 6, 1, D_MODEL)
        mods = (m[bl:bl + 1], m[:bl])
        for si, (seq_len, batch, period) in enumerate(streams):
            n_tok = seq_len * batch
            x, mod = xs[si], mods[si]
            g_in, vq, e, x1, rkv_in, lo_in, s5_in = _inproj(x, mod, norm1_g3, w_in_bf, hy_conv_w, hy_conv_b3,
                                                            hy_bias3, l, seq_len, period)
            seq = lambda a: a.reshape(batch, seq_len, a.shape[-1])
            tok = lambda a: a.reshape(n_tok, a.shape[-1])
            hs, hd, ny = _hy_filter(seq_len, l, *hy_filter_params)
            fw = dft[seq_len]
            tk = min(seq_len, 256)
            sp, sq = _hy_spec(hs, hd, fw, seq_len, tk)
            y_hy = _hy_conv(seq(vq), sp, sq, ny, fw, seq(e), seq(x1), seq_len, batch, tk)
            v_rw, bon, g_rw, at, rh, bt, kt, wt = _rw_pre(seq(rkv_in), seq(lo_in), P, l, seq_len, batch, period)
            whi, wlo, rt, ot = _rw_prep(at, rh, v_rw, bt, kt, seq_len, batch)
            o_f, o_b, s_fin = _rw_seq(whi, wlo, rt, ot, wt, _rw_pack_state(rw_s0[si][:, l]), seq_len, batch)
            y5d, fin5 = _s5_scan(seq(s5_in), s5_wb, s5_wc, s5_abar, s5_x0[si][l], l, seq_len, batch)
            y5d = y5d.reshape(2, n_tok, S5_WIDTH)
            if si == 0:
                rw_new.append(_rw_unpack_state(s_fin))
                s5_new.append(fin5)
            h, hn, gates = _merge(x, g_in, tok(y_hy), tok(o_f), tok(o_b), tok(bon), tok(g_rw), s5_in,
                                  y5d[0], y5d[1], mod, P, l, seq_len)
            xs[si] = _moe(hn, gates, h, mod, exp_wg, exp_wu, exp_wd, final_g, l, seq_len, final=(l == DEPTH - 1))

    y_prompt = xs[0].reshape(bc, lc, D_MODEL).astype(x_prompt.dtype)
    y_sample = xs[1].reshape(bl, ll, D_MODEL).astype(x_sample.dtype)
    new_state_rwkv = jnp.stack(rw_new, axis=1)
    s5_fin = jnp.stack(s5_new, axis=0)
    s5_fin = jnp.moveaxis(s5_fin, 2, 0)
    new_re = s5_fin[..., :S5_NS].reshape(bc, DEPTH, 2, S5_GROUPS, S5_STATE)
    new_im = s5_fin[..., S5_NS:].reshape(bc, DEPTH, 2, S5_GROUPS, S5_STATE)
    return (y_prompt, y_sample, new_state_rwkv, new_re, new_im)
```

```python
import functools
import math

import jax
import jax.numpy as jnp
import numpy as np
from jax import lax
from jax.experimental import pallas as pl
from jax.experimental.pallas import tpu as pltpu

F32 = jnp.float32
BF16 = jnp.bfloat16
HI = lax.Precision.HIGHEST

D_MODEL = 1024
DEPTH = 2
GRID_W = 64
NORM_EPS = 1e-6

HY_WIDTH = 384
HY_FILTER_HIDDEN = 64
HY_N_BANDS = 8
HY_POS_DIM = 1 + 2 * HY_N_BANDS
HY_FAST_DECAY_PCT = 0.3
HY_SLOW_DECAY_PCT = 1.5
HY_DECAY_TARGET = 1e-2

RW_HEAD_DIM = 64
RW_HEADS = 6
RW_WIDTH = RW_HEADS * RW_HEAD_DIM
RW_GN_EPS = 64e-5
RW_CHUNK = 64

S5_GROUPS = 16
S5_GROUP_CH = 16
S5_WIDTH = S5_GROUPS * S5_GROUP_CH
S5_STATE = 64
S5_NS = S5_GROUPS * S5_STATE

N_EXPERTS = 16
N_EXPERT_GROUPS = 4
EXPERTS_PER_GROUP = N_EXPERTS // N_EXPERT_GROUPS
EXPERT_FF = 512

C_GATE = 3 * D_MODEL
C_HY = 3 * HY_WIDTH
C_RKV = 3 * RW_WIDTH
C_LORA = 384
C_S5 = S5_WIDTH
IN_COLS = C_GATE + C_HY + C_RKV + C_LORA + C_S5

V7X_VMEM_LIMIT = 56 * 1024 * 1024

TM_TOK = 256
TL_SEQ = 256


def _cparams(sem):
    return pltpu.CompilerParams(dimension_semantics=sem, vmem_limit_bytes=V7X_VMEM_LIMIT)


def _dot(a, b, precision=None):
    return jnp.dot(a, b, preferred_element_type=F32, precision=precision)


def _mod_kernel(c_ref, w_ref, b_ref, o_ref):
    c = c_ref[...]
    s = c * jax.nn.sigmoid(c)
    o_ref[...] = _dot(s, w_ref[...], HI) + b_ref[...]


def _modulation(cond8, ada_w, ada_b, l):
    tn = 1536
    return pl.pallas_call(
        _mod_kernel,
        grid=(6 * D_MODEL // tn,),
        in_specs=[
            pl.BlockSpec((8, D_MODEL), lambda j: (0, 0)),
            pl.BlockSpec((None, D_MODEL, tn), lambda j: (l, 0, j)),
            pl.BlockSpec((None, 1, tn), lambda j: (l, 0, j)),
        ],
        out_specs=pl.BlockSpec((8, tn), lambda j: (0, j)),
        out_shape=jax.ShapeDtypeStruct((8, 6 * D_MODEL), F32),
        compiler_params=_cparams(("arbitrary",)),
        name="modulation",
    )(cond8, ada_w, ada_b.reshape(DEPTH, 1, 6 * D_MODEL))


def _inproj_kernel(x_ref, sh_ref, sc_ref, g_ref, w_ref, cw_ref, cb_ref, bias_ref,
                   og, ovq, oe, ox1, orkv, olo, os5, *, period):
    x = x_ref[...]
    xn = x * lax.rsqrt(jnp.mean(x * x, axis=-1, keepdims=True) + NORM_EPS) * g_ref[...]
    xn = (xn * (1.0 + sc_ref[...]) + sh_ref[...]).astype(BF16)
    off = 0
    for o_ref, width in ((og, C_GATE), (None, C_HY), (orkv, C_RKV), (olo, C_LORA), (os5, C_S5)):
        y = _dot(xn, w_ref[:, off:off + width])
        off += width
        if o_ref is og:
            o_ref[...] = jax.nn.sigmoid(y).astype(BF16)
            continue
        if o_ref is not None:
            o_ref[...] = y
            continue
        u = _conv3(y, cw_ref, cb_ref, period)
        v = u[:, 2 * HY_WIDTH:] * u[:, :HY_WIDTH]
        ovq[...] = v.astype(BF16)
        oe[...] = v * bias_ref[...]
        ox1[...] = u[:, HY_WIDTH:2 * HY_WIDTH]


def _mod_spec(which, tm, seq_len, n_mod):
    if n_mod == 1:
        return pl.BlockSpec((None, None, 1, D_MODEL), lambda i, *_: (0, which, 0, 0))
    return pl.BlockSpec((None, None, 1, D_MODEL), lambda i, *_: (i * tm // seq_len, which, 0, 0))


def _inproj(x, mod, norm_g3, w_in_bf, hy_conv_w, hy_conv_b3, hy_bias3, l, seq_len, period):
    n_tok = x.shape[0]
    tm = 2 * TM_TOK
    assert tm % period == 0 and (seq_len % tm == 0 or (tm % seq_len == 0 and mod.shape[0] == 1))
    outs = ((C_GATE, BF16), (HY_WIDTH, BF16), (HY_WIDTH, F32), (HY_WIDTH, F32),
            (C_RKV, F32), (C_LORA, F32), (C_S5, F32))
    par = lambda shape: pl.BlockSpec((None,) + shape, lambda i: (l,) + tuple(0 for _ in shape))
    return pl.pallas_call(
        functools.partial(_inproj_kernel, period=period),
        grid=(n_tok // tm,),
        in_specs=[
            pl.BlockSpec((tm, D_MODEL), lambda i: (i, 0)),
            _mod_spec(0, tm, seq_len, mod.shape[0]),
            _mod_spec(1, tm, seq_len, mod.shape[0]),
            par((1, D_MODEL)),
            pl.BlockSpec((None, D_MODEL, IN_COLS), lambda i: (l, 0, 0), pipeline_mode=pl.Buffered(1)),
            par((3, C_HY)), par((1, C_HY)), par((1, HY_WIDTH)),
        ],
        out_specs=[pl.BlockSpec((tm, w), lambda i: (i, 0)) for w, _ in outs],
        out_shape=[jax.ShapeDtypeStruct((n_tok, w), dt) for w, dt in outs],
        compiler_params=_cparams(("parallel",)),
        name="inproj",
    )(x, mod, mod, norm_g3, w_in_bf, hy_conv_w, hy_conv_b3, hy_bias3)


def _conv3(x, w_ref, b_ref, period):
    n = x.shape[0]
    t = lax.broadcasted_iota(jnp.int32, x.shape, 0) % period
    prev = jnp.where(t == 0, 0.0, pltpu.roll(x, 1, axis=0))
    nxt = jnp.where(t == period - 1, 0.0, pltpu.roll(x, n - 1, axis=0))
    return prev * w_ref[0:1, :] + x * w_ref[1:2, :] + nxt * w_ref[2:3, :] + b_ref[...]


def _hy_filter_kernel(feat_ref, w1_ref, b1_ref, f1_ref, w2_ref, b2_ref, f2_ref, w3_ref, dl_ref,
                      hs_ref, hd_ref, ny_ref):
    feats = feat_ref[...]
    h = jnp.sin(f1_ref[...] * (_dot(feats, w1_ref[...], HI) + b1_ref[...]))
    h = jnp.sin(f2_ref[...] * (_dot(h, w2_ref[...], HI) + b2_ref[...]))
    h = _dot(h, w3_ref[...], HI)
    h = h * jnp.exp(-feats[:, 0:1] * dl_ref[...])
    hf, hb = h[:, :HY_WIDTH], h[:, HY_WIDTH:]
    l1 = (jnp.sum(jnp.abs(hf), axis=0, keepdims=True)
          + jnp.sum(jnp.abs(hb), axis=0, keepdims=True) + 1e-6)
    hf = hf / l1
    hb = hb / l1
    row = lax.broadcasted_iota(jnp.int32, hb.shape, 0)
    hb0 = jnp.where(row == 0, 0.0, hb)
    hs = hf + hb0
    hs_ref[...] = hs.astype(BF16)
    hd_ref[...] = (hb0 - hf).astype(BF16)
    ny_ref[...] = jnp.sum(hs * _alternating(hs.shape), axis=0, keepdims=True) * (0.5 / hs.shape[0])


def _hy_positions(seq_len):
    t = np.arange(seq_len, dtype=np.float32)
    t01 = t / np.float32(max(seq_len - 1, 1))
    bands = np.linspace(1e-4, HY_N_BANDS - 1, HY_N_BANDS, dtype=np.float32)
    ang = np.float32(2.0 * math.pi / seq_len) * t[:, None] * bands[None, :]
    feats = np.concatenate([t01[:, None], np.cos(ang), -np.sin(ang)], axis=-1).astype(np.float32)
    feats = np.pad(feats, ((0, 0), (0, 128 - HY_POS_DIM)))
    max_decay = math.log(HY_DECAY_TARGET) / HY_FAST_DECAY_PCT
    min_decay = math.log(HY_DECAY_TARGET) / HY_SLOW_DECAY_PCT
    deltas = np.abs(np.linspace(min_decay, max_decay, HY_WIDTH, dtype=np.float32))
    return jnp.asarray(feats), jnp.asarray(np.concatenate([deltas, deltas])[None, :])


def _hy_filter(seq_len, l, w1p, b1, f1, w2, b2, f2, w3):
    feats, dl = _hy_positions(seq_len)
    hid = HY_FILTER_HIDDEN
    full = lambda shape: pl.BlockSpec(shape, lambda i: tuple(0 for _ in shape))
    par = lambda shape: pl.BlockSpec((None,) + shape, lambda i: (l,) + tuple(0 for _ in shape))
    return pl.pallas_call(
        _hy_filter_kernel,
        grid=(1,),
        in_specs=[full((seq_len, 128)), par((128, hid)), par((1, hid)), par((1, hid)),
                  par((hid, hid)), par((1, hid)), par((1, hid)), par((hid, 2 * HY_WIDTH)),
                  full((1, 2 * HY_WIDTH))],
        out_specs=[full((seq_len, HY_WIDTH)), full((seq_len, HY_WIDTH)), full((1, HY_WIDTH))],
        out_shape=[jax.ShapeDtypeStruct((seq_len, HY_WIDTH), BF16)] * 2
        + [jax.ShapeDtypeStruct((1, HY_WIDTH), F32)],
        compiler_params=pltpu.CompilerParams(vmem_limit_bytes=V7X_VMEM_LIMIT),
        name="hyena_filter",
    )(feats, w1p, b1, f1, w2, b2, f2, w3, dl)


def _dft_matrices(seq_len):
    n2 = 2 * seq_len
    k = np.arange(seq_len, dtype=np.int64)
    ang = ((k[:, None] * k[None, :]) % n2).astype(np.float64) * (2.0 * math.pi / n2)
    return jnp.asarray(np.stack([np.cos(ang), np.sin(ang)]).astype(np.float32)).astype(BF16)


def _alternating(shape):
    t = lax.broadcasted_iota(jnp.int32, shape, 0)
    return (1 - 2 * (t % 2)).astype(F32)


def _hy_spec_kernel(hs_ref, hd_ref, f_ref, p_ref, q_ref, *, tk, seq_len):
    kre = _dot(f_ref[0], hs_ref[...])
    kim = _dot(f_ref[1], hd_ref[...])
    row = lax.broadcasted_iota(jnp.int32, kre.shape, 0) + pl.program_id(0) * tk
    w = jnp.where(row == 0, 0.5 / seq_len, 1.0 / seq_len)
    p_ref[...] = kre * w
    q_ref[...] = kim * w


def _hy_spec(hs, hd, fw, seq_len, tk):
    nf = seq_len // tk
    full = pl.BlockSpec((seq_len, HY_WIDTH), lambda f: (0, 0))
    tile = pl.BlockSpec((tk, HY_WIDTH), lambda f: (f, 0))
    return pl.pallas_call(
        functools.partial(_hy_spec_kernel, tk=tk, seq_len=seq_len),
        grid=(nf,),
        in_specs=[full, full, pl.BlockSpec((2, tk, seq_len), lambda f: (0, f, 0))],
        out_specs=[tile, tile],
        out_shape=[jax.ShapeDtypeStruct((seq_len, HY_WIDTH), F32)] * 2,
        compiler_params=_cparams(("parallel",)),
        name="hyena_spectrum",
    )(hs, hd, fw)


def _hy_conv_kernel(v_ref, p_ref, q_ref, ny_ref, f_ref, g_ref, e_ref, x1_ref, o_ref, acc_ref):
    f = pl.program_id(1)
    v = v_ref[...]

    @pl.when(f == 0)
    def _():
        alt = _alternating(acc_ref.shape)
        v_nyq = jnp.sum(v.astype(F32) * alt, axis=0, keepdims=True)
        acc_ref[...] = alt * (v_nyq * ny_ref[...])

    c = _dot(f_ref[0], v)
    s = _dot(f_ref[1], v)
    p, q = p_ref[...], q_ref[...]
    yre = (c * p + s * q).astype(BF16)
    yim_neg = (s * p - c * q).astype(BF16)
    acc_ref[...] += _dot(g_ref[0], yre) + _dot(g_ref[1], yim_neg)

    @pl.when(f == pl.num_programs(1) - 1)
    def _():
        o_ref[...] = (acc_ref[...] + e_ref[...]) * x1_ref[...]


def _hy_conv(vq, p, q, ny, fw, e, x1, seq_len, batch, tk):
    nf = seq_len // tk
    seq = pl.BlockSpec((None, seq_len, HY_WIDTH), lambda b, f: (b, 0, 0))
    tile = pl.BlockSpec((tk, HY_WIDTH), lambda b, f: (f, 0))
    return pl.pallas_call(
        _hy_conv_kernel,
        grid=(batch, nf),
        in_specs=[seq, tile, tile, pl.BlockSpec((1, HY_WIDTH), lambda b, f: (0, 0)),
                  pl.BlockSpec((2, tk, seq_len), lambda b, f: (0, f, 0)),
                  pl.BlockSpec((2, seq_len, tk), lambda b, f: (0, 0, f)),
                  seq, seq],
        out_specs=seq,
        out_shape=jax.ShapeDtypeStruct((batch, seq_len, HY_WIDTH), F32),
        scratch_shapes=[pltpu.VMEM((seq_len, HY_WIDTH), F32)],
        compiler_params=_cparams(("parallel", "arbitrary")),
        name="hyena_longconv",
    )(vq, p, q, ny, fw, fw, e, x1)


def _split_bf16(x):
    hi = x.astype(BF16)
    return hi, (x - hi.astype(F32)).astype(BF16)


def _dot3(a, b):
    ah, al = _split_bf16(a)
    bh, bl = _split_bf16(b)
    n = a.shape[0]
    both = _dot(jnp.concatenate([ah, al], axis=0), bh)
    return both[:n] + both[n:] + _dot(ah, bl)


def _dot_exact_rhs(a, b):
    ah, al = _split_bf16(a)
    n = a.shape[0]
    both = _dot(jnp.concatenate([ah, al], axis=0), b.astype(BF16))
    return both[:n] + both[n:]


def _dot_exact_lhs(a, b):
    bh, bl = _split_bf16(b)
    a16 = a.astype(BF16)
    return _dot(a16, bh) + _dot(a16, bl)


def _dot_nt(a, b):
    return lax.dot_general(a, b, (((1,), (1,)), ((), ())), preferred_element_type=F32)


def _rw_pre_kernel(rkv_ref, lo_ref, cw_ref, cb_ref, w0_ref, wup_ref, a0_ref, aup_ref, gup_ref,
                   kk_ref, ka_ref, rk_ref, hs_ref, tri_ref,
                   v_out, bonus_out, g_out, at_out, rh_out, bt_out, kt_out, wt_out,
                   *, period):
    T = RW_CHUNK
    u = _conv3(rkv_ref[...], cw_ref, cb_ref, period)
    r, k, v = u[:, :RW_WIDTH], u[:, RW_WIDTH:2 * RW_WIDTH], u[:, 2 * RW_WIDTH:]
    lo = lo_ref[...]
    wraw = w0_ref[...] + _dot3(jnp.tanh(lo[:, 0:128]), wup_ref[...])
    z = -wraw
    softplus = jnp.maximum(z, 0.0) + jnp.log(1.0 + jnp.exp(-jnp.abs(z)))
    logw = -jnp.exp(-softplus - 0.5)
    a = jax.nn.sigmoid(a0_ref[...] + _dot3(lo[:, 128:256], aup_ref[...]))
    g_out[...] = _dot3(jax.nn.sigmoid(lo[:, 256:384]), gup_ref[...])
    head_sum = hs_ref[...]
    kk = k * kk_ref[...]
    kk = kk * lax.rsqrt(jnp.maximum(_dot_exact_rhs(kk * kk, head_sum), 1e-24))
    v_out[...] = v
    n_chunk = r.shape[0] // T
    lane = lax.broadcasted_iota(jnp.int32, (RW_WIDTH, 128), 1)
    kt_sum = jnp.zeros_like(k)
    for d in range(2):
        a_d = a[:, d * RW_WIDTH:(d + 1) * RW_WIDTH]
        lw_d = logw[:, d * RW_WIDTH:(d + 1) * RW_WIDTH]
        kt_d = k * (1.0 + (a_d - 1.0) * ka_ref[...])
        kt_sum = kt_sum + kt_d
        cum = _dot_exact_lhs(tri_ref[d], lw_d)
        w_inv = jnp.exp(-cum)
        at_out[d] = -kk * jnp.exp(cum - lw_d)
        rh_out[d] = r * jnp.exp(cum)
        bt = (kk * a_d * w_inv).T
        ktt = (kt_d * w_inv).T
        for j in range(0, n_chunk, 2):
            for src, dst in ((bt, bt_out), (ktt, kt_out)):
                pair = src[:, j * T:(j + 2) * T]
                swapped = pltpu.roll(pair, T, axis=1)
                dst[d, j] = jnp.where(lane < T, pair, swapped)
                dst[d, j + 1] = jnp.where(lane < T, swapped, pair)
        for j in range(n_chunk):
            last = j * T + (T - 1 if d == 0 else 0)
            wt_out[d, j] = jnp.exp(cum[last:last + 1, :])
    bonus_out[...] = _dot_exact_rhs(r * kt_sum * rk_ref[...], head_sum) * v


def _chunk_tri(n_rows):
    i = np.arange(n_rows)
    same = (i[:, None] // RW_CHUNK) == (i[None, :] // RW_CHUNK)
    fwd = same & (i[None, :] <= i[:, None])
    bwd = same & (i[None, :] >= i[:, None])
    return jnp.asarray(np.stack([fwd, bwd]).astype(np.float32))


def _rw_pre(rkv2d, lora2d, P, l, seq_len, batch, period):
    nt = seq_len // TL_SEQ
    cpt = TL_SEQ // RW_CHUNK
    nchunk = seq_len // RW_CHUNK
    blk = lambda w: pl.BlockSpec((None, TL_SEQ, w), lambda b, i: (b, i, 0))
    par = lambda shape: pl.BlockSpec((None,) + shape, lambda b, i: (l,) + tuple(0 for _ in shape))
    row_shape = jax.ShapeDtypeStruct((batch, seq_len, RW_WIDTH), F32)
    dir_shape = jax.ShapeDtypeStruct((2, batch, seq_len, RW_WIDTH), F32)
    dir_blk = pl.BlockSpec((2, None, TL_SEQ, RW_WIDTH), lambda b, i: (0, b, i, 0))
    tr_shape = jax.ShapeDtypeStruct((2, batch, nchunk, RW_WIDTH, 128), F32)
    tr_blk = pl.BlockSpec((2, None, cpt, RW_WIDTH, 128), lambda b, i: (0, b, i, 0, 0))
    wt_shape = jax.ShapeDtypeStruct((2, batch, nchunk, 1, RW_WIDTH), F32)
    wt_blk = pl.BlockSpec((2, None, cpt, 1, RW_WIDTH), lambda b, i: (0, b, i, 0, 0))
    return pl.pallas_call(
        functools.partial(_rw_pre_kernel, period=period),
        grid=(batch, nt),
        in_specs=[
            blk(C_RKV), blk(C_LORA),
            par((3, C_RKV)), par((1, C_RKV)),
            par((1, 2 * RW_WIDTH)), par((128, 2 * RW_WIDTH)),
            par((1, 2 * RW_WIDTH)), par((128, 2 * RW_WIDTH)),
            par((128, RW_WIDTH)),
            par((1, RW_WIDTH)), par((1, RW_WIDTH)), par((1, RW_WIDTH)),
            pl.BlockSpec((RW_WIDTH, RW_WIDTH), lambda b, i: (0, 0)),
            pl.BlockSpec((2, TL_SEQ, TL_SEQ), lambda b, i: (0, 0, 0)),
        ],
        out_specs=[blk(RW_WIDTH)] * 3 + [dir_blk, dir_blk, tr_blk, tr_blk, wt_blk],
        out_shape=[row_shape] * 3 + [dir_shape, dir_shape, tr_shape, tr_shape, wt_shape],
        compiler_params=_cparams(("parallel", "parallel")),
        name="rwkv_pre",
    )(rkv2d, lora2d, P["rw_conv_w"], P["rw_conv_b3"], P["rw_w0c"], P["rw_wupc"], P["rw_a0c"],
      P["rw_aupc"], P["rw_g_up"], P["rw_k_k3"], P["rw_k_a3"], P["rw_r_k3"], P["head_sum"],
      _chunk_tri(TL_SEQ))


RW_PAIRS = RW_HEADS // 2
RW_PREP_CHUNKS = 4


def _rw_prep_kernel(at_ref, rh_ref, v_ref, bt_ref, kt_ref, whi_ref, wlo_ref, rt_ref, ot_ref):
    d = pl.program_id(0)
    T = RW_CHUNK
    sgn = 1 - 2 * d
    ti = lax.broadcasted_iota(jnp.int32, (T, 128), 0)
    tj = lax.broadcasted_iota(jnp.int32, (T, 128), 1) % T
    before = (tj - ti) * sgn < 0
    before_eq = (tj - ti) * sgn <= 0
    eye2 = (ti == tj).astype(F32)
    bi = lax.broadcasted_iota(jnp.int32, (128, 128), 0)
    bj = lax.broadcasted_iota(jnp.int32, (128, 128), 1)
    same_head = (bi // T) == (bj // T)
    eye128 = (bi == bj).astype(F32)

    def bdiag(x):
        return jnp.where(same_head, jnp.concatenate([x, x], axis=0), 0.0)

    probs = [(j, p) for j in range(RW_PREP_CHUNKS) for p in range(RW_PAIRS)]
    rows = lambda j: slice(j * T, (j + 1) * T)
    cols = lambda p: slice(p * 128, (p + 1) * 128)
    a_l = [at_ref[rows(j), cols(p)] for j, p in probs]
    r_l = [rh_ref[rows(j), cols(p)] for j, p in probs]
    bd_b = [jnp.where(same_head, bt_ref[j, cols(p), :], 0.0) for j, p in probs]
    bd_k = [jnp.where(same_head, kt_ref[j, cols(p), :], 0.0) for j, p in probs]
    gram = [_dot3(jnp.concatenate([a, r], axis=0), jnp.concatenate([b, k], axis=1))
            for a, r, b, k in zip(a_l, r_l, bd_b, bd_k)]
    l_ab = [jnp.where(before, g[:T, :128], 0.0) for g in gram]
    l_ak = [jnp.where(before, g[:T, 128:], 0.0) for g in gram]
    g_rb = [jnp.where(before_eq, g[T:, :128], 0.0) for g in gram]
    g_rk = [jnp.where(before_eq, g[T:, 128:], 0.0) for g in gram]
    m = [eye2 + l for l in l_ab]
    lp = [_dot3(l, bdiag(l)) for l in l_ab]
    n_sq = int(math.log2(T)) - 1
    for s in range(n_sq):
        if s < n_sq - 1:
            out = [_dot3(jnp.concatenate([mm, ll], axis=0), bdiag(ll)) for mm, ll in zip(m, lp)]
            m = [mm + o[:T] for mm, o in zip(m, out)]
            lp = [o[T:] for o in out]
        else:
            m = [mm + _dot3(mm, bdiag(ll)) for mm, ll in zip(m, lp)]
    x = [_dot3(mm, jnp.concatenate([bdiag(a), bdiag(l)], axis=1))
         for mm, a, l in zip(m, a_l, l_ak)]
    y = [_dot3(bt_ref[j, cols(p), 0:T], xx) for (j, p), xx in zip(probs, x)]
    bd_v = [bdiag(v_ref[rows(j), cols(p)]) for j, p in probs]
    psi = [_dot3(jnp.where(same_head, yy[:, 128:], 0.0) + k, v)
           for yy, k, v in zip(y, bd_k, bd_v)]
    for (j, p), yy, ps in zip(probs, y, psi):
        phi = jnp.where(same_head, eye128 + yy[:, :128], 0.0)
        w_hi, w_lo = _split_bf16(jnp.concatenate([phi, ps], axis=1))
        whi_ref[j, p] = w_hi
        wlo_ref[j, p] = w_lo
    z = [_dot(g.astype(BF16), jnp.concatenate([bdiag(xx[:, :128]), bdiag(xx[:, 128:])], axis=1).astype(BF16))
         for g, xx in zip(g_rb, x)]
    for (j, p), zz, r, g, v in zip(probs, z, r_l, g_rk, bd_v):
        rt_ref[rows(j), cols(p)] = (r + zz[:, :128]).astype(BF16)
        ot_ref[rows(j), cols(p)] = _dot((zz[:, 128:] + g).astype(BF16), v.astype(BF16))


def _rw_prep(at, rh, v, bt, kt, seq_len, batch):
    nchunk = seq_len // RW_CHUNK
    cb = RW_PREP_CHUNKS
    rows = cb * RW_CHUNK
    drow = pl.BlockSpec((None, None, rows, RW_WIDTH), lambda d, b, i: (d, b, i, 0))
    tr = pl.BlockSpec((None, None, cb, RW_WIDTH, 128), lambda d, b, i: (d, b, i, 0, 0))
    wsp = pl.BlockSpec((None, None, cb, RW_PAIRS, 128, 256), lambda d, b, i: (d, b, i, 0, 0, 0))
    w_shape = jax.ShapeDtypeStruct((2, batch, nchunk, RW_PAIRS, 128, 256), BF16)
    return pl.pallas_call(
        _rw_prep_kernel,
        grid=(2, batch, nchunk // cb),
        in_specs=[drow, drow, pl.BlockSpec((None, rows, RW_WIDTH), lambda d, b, i: (b, i, 0)), tr, tr],
        out_specs=[wsp, wsp, drow, drow],
        out_shape=[w_shape, w_shape,
                   jax.ShapeDtypeStruct((2, batch, seq_len, RW_WIDTH), BF16),
                   jax.ShapeDtypeStruct((2, batch, seq_len, RW_WIDTH), F32)],
        compiler_params=_cparams(("parallel", "parallel", "parallel")),
        name="rwkv_chunk_ops",
    )(at, rh, v, bt, kt)


def _rw_seq_kernel(whi_f, wlo_f, whi_b, wlo_b, rt_f, rt_b, ot_f, ot_b, wt_f, wt_b, s0_ref,
                   o_f, o_b, sfin_ref, st_ref, *, bb):
    c = pl.program_id(1)
    T = RW_CHUNK

    @pl.when(c == 0)
    def _():
        st_ref[...] = s0_ref[...]

    bi_ = lax.broadcasted_iota(jnp.int32, (128, 128), 0)
    bj_ = lax.broadcasted_iota(jnp.int32, (128, 128), 1)
    same_head = (bi_ // T) == (bj_ // T)
    ji = lax.broadcasted_iota(jnp.int32, (T, 128), 0)
    jj = lax.broadcasted_iota(jnp.int32, (T, 128), 1) % T
    eye2 = (ji == jj).astype(BF16)

    dirs = ((whi_f, wlo_f, rt_f, ot_f, wt_f, o_f), (whi_b, wlo_b, rt_b, ot_b, wt_b, o_b))
    probs = [(d, b, p) for d in range(2) for b in range(bb) for p in range(RW_PAIRS)]
    cols = lambda p: slice(p * 128, (p + 1) * 128)
    s_l =[st_ref[d, b, p] for d, b, p in probs]
    split = [_split_bf16(s) for s in s_l]
    zero = jnp.zeros((T, 128), BF16)
    lhs = [jnp.concatenate([jnp.concatenate([hi, eye2], axis=1), jnp.concatenate([lo, zero], axis=1)], axis=0)
           for hi, lo in split]
    both = [_dot_nt(lh, dirs[d][0][b, p]) for (d, b, p), lh in zip(probs, lhs)]
    new = [bt[:T] + bt[T:] + _dot_nt(lh[:T], dirs[d][1][b, p])
           for (d, b, p), lh, bt in zip(probs, lhs, both)]
    for (d, b, p), nw in zip(probs, new):
        st_ref[d, b, p] = nw * dirs[d][4][b, :, p * 128:(p + 1) * 128]
    for (d, b, p), s in zip(probs, s_l):
        _, _, rt, ot, _, o = dirs[d]
        bd_s = jnp.where(same_head, jnp.concatenate([s, s], axis=0), 0.0).astype(BF16)
        o[b, :, cols(p)] = _dot_nt(rt[b, :, cols(p)], bd_s) + ot[b, :, cols(p)]

    @pl.when(c == pl.num_programs(1) - 1)
    def _():
        for d, b, p in probs:
            s = st_ref[d, b, p]
            for hh in range(2):
                sfin_ref[b, d, 2 * p + hh] = s[:, hh * RW_HEAD_DIM:(hh + 1) * RW_HEAD_DIM]


def _rw_seq(whi, wlo, rt, ot, wt, s0p, seq_len, batch):
    nc = seq_len // RW_CHUNK
    bb = min(batch, 8)
    rev = lambda c: nc - 1 - c
    fwd = lambda c: c
    wsp = lambda d, ch: pl.BlockSpec((None, bb, None, RW_PAIRS, 128, 256),
                                     lambda g, c: (d, g, ch(c), 0, 0, 0))
    row = lambda d, ch: pl.BlockSpec((None, bb, RW_CHUNK, RW_WIDTH), lambda g, c: (d, g, ch(c), 0))
    wts = lambda d, ch: pl.BlockSpec((None, bb, None, 1, RW_WIDTH), lambda g, c: (d, g, ch(c), 0, 0))
    st = pl.BlockSpec((2, bb, RW_PAIRS, RW_HEAD_DIM, 128), lambda g, c: (0, g, 0, 0, 0))
    o_spec = lambda ch: pl.BlockSpec((bb, RW_CHUNK, RW_WIDTH), lambda g, c: (g, ch(c), 0))
    o_shape = jax.ShapeDtypeStruct((batch, seq_len, RW_WIDTH), F32)
    return pl.pallas_call(
        functools.partial(_rw_seq_kernel, bb=bb),
        grid=(batch // bb, nc),
        in_specs=[wsp(0, fwd), wsp(0, fwd), wsp(1, rev), wsp(1, rev),
                  row(0, fwd), row(1, rev), row(0, fwd), row(1, rev),
                  wts(0, fwd), wts(1, rev), st],
        out_specs=[o_spec(fwd), o_spec(rev),
                   pl.BlockSpec((bb, 2, RW_HEADS, RW_HEAD_DIM, RW_HEAD_DIM), lambda g, c: (g, 0, 0, 0, 0))],
        out_shape=[o_shape, o_shape,
                   jax.ShapeDtypeStruct((batch, 2, RW_HEADS, RW_HEAD_DIM, RW_HEAD_DIM), F32)],
        scratch_shapes=[pltpu.VMEM((2, bb, RW_PAIRS, RW_HEAD_DIM, 128), F32)],
        compiler_params=_cparams(("parallel", "arbitrary")),
        name="rwkv_state_scan",
    )(whi, wlo, whi, wlo, rt, rt, ot, ot, wt, wt, s0p)


def _rw_pack_state(s):
    b = s.shape[0]
    s = s.reshape(b, 2, RW_PAIRS, 2, RW_HEAD_DIM, RW_HEAD_DIM)
    return jnp.transpose(s, (1, 0, 2, 4, 3, 5)).reshape(2, b, RW_PAIRS, RW_HEAD_DIM, 128)


def _s5_kernel(u_ref, wb_ref, wc_ref, a_ref, x0_ref, y_ref, fin_ref, bu_ref, st_ref, tm_ref, *, batch, steps):
    d = pl.program_id(0)
    i = pl.program_id(1)
    ns = S5_NS
    halves = S5_WIDTH // 128

    @pl.when(i == 0)
    def _():
        st_ref[...] = x0_ref[...]

    for b in range(batch):
        for k in range(halves):
            tm_ref[k, pl.ds(b, steps, stride=batch), :] = u_ref[b, :, k * 128:(k + 1) * 128]
    u_tm = jnp.concatenate([tm_ref[k] for k in range(halves)], axis=1)
    bu_ref[...] = _dot(u_tm.astype(BF16), wb_ref[...])

    if batch % 8 == 0:
        lanes = 256
        for lc in range(ns // lanes):
            re_sl = slice(lc * lanes, (lc + 1) * lanes)
            im_sl = slice(ns + lc * lanes, ns + (lc + 1) * lanes)
            ar, ai = a_ref[:, re_sl], a_ref[:, im_sl]

            def body(s, carry, re_sl=re_sl, im_sl=im_sl, ar=ar, ai=ai):
                xr, xi = carry
                step = s + d * (steps - 1 - 2 * s)
                rows = pl.ds(pl.multiple_of(step * batch, 8), batch)
                nr = ar * xr - ai * xi + bu_ref[rows, re_sl]
                ni = ar * xi + ai * xr + bu_ref[rows, im_sl]
                bu_ref[rows, re_sl] = nr
                bu_ref[rows, im_sl] = ni
                return nr, ni

            xr, xi = lax.fori_loop(0, steps, body, (st_ref[:, re_sl], st_ref[:, im_sl]))
            st_ref[:, re_sl] = xr
            st_ref[:, im_sl] = xi
    else:
        ar, ai = a_ref[:, :ns], a_ref[:, ns:]
        half = lax.broadcasted_iota(jnp.int32, (8, ns), 0) // 4
        first = half == d

        def body(s, carry):
            xr, xi = carry
            pair = s + d * (steps // 2 - 1 - 2 * s)
            rows = pl.ds(pl.multiple_of(pair * 8, 8), 8)
            br, bi = bu_ref[rows, :ns], bu_ref[rows, ns:]
            r1 = ar * xr - ai * xi + br
            i1 = ar * xi + ai * xr + bi
            r1s, i1s = pltpu.roll(r1, 4, axis=0), pltpu.roll(i1, 4, axis=0)
            r2 = ar * r1s - ai * i1s + br
            i2 = ar * i1s + ai * r1s + bi
            bu_ref[rows, :ns] = jnp.where(first, r1, r2)
            bu_ref[rows, ns:] = jnp.where(first, i1, i2)
            return pltpu.roll(r2, 4, axis=0), pltpu.roll(i2, 4, axis=0)

        xr, xi = lax.fori_loop(0, steps // 2, body, (st_ref[:, :ns], st_ref[:, ns:]))
        st_ref[:, :ns] = xr
        st_ref[:, ns:] = xi

    y = _dot(bu_ref[...].astype(BF16), wc_ref[...])
    for k in range(halves):
        tm_ref[k] = y[:, k * 128:(k + 1) * 128]
    for b in range(batch):
        for k in range(halves):
            y_ref[b, :, k * 128:(k + 1) * 128] = tm_ref[k, pl.ds(b, steps, stride=batch), :]

    @pl.when(i == pl.num_programs(1) - 1)
    def _():
        fin_ref[...] = st_ref[...]


def _s5_scan(u, wb, wc, a_bar, x0, l, seq_len, batch):
    rows_blk = 512
    steps = rows_blk // batch
    nblk = seq_len * batch // rows_blk
    rows = max(batch, 8)
    blk = lambda d, i: i + d * (nblk - 1 - 2 * i)
    return pl.pallas_call(
        functools.partial(_s5_kernel, batch=batch, steps=steps),
        grid=(2, nblk),
        in_specs=[
            pl.BlockSpec((batch, steps, S5_WIDTH), lambda d, i: (0, blk(d, i), 0)),
            pl.BlockSpec((None, None, S5_WIDTH, 2 * S5_NS), lambda d, i: (l, d, 0, 0)),
            pl.BlockSpec((None, None, 2 * S5_NS, S5_WIDTH), lambda d, i: (l, d, 0, 0)),
            pl.BlockSpec((None, None, 1, 2 * S5_NS), lambda d, i: (l, d, 0, 0)),
            pl.BlockSpec((None, rows, 2 * S5_NS), lambda d, i: (d, 0, 0)),
        ],
        out_specs=[
            pl.BlockSpec((None, batch, steps, S5_WIDTH), lambda d, i: (d, 0, blk(d, i), 0)),
            pl.BlockSpec((None, rows, 2 * S5_NS), lambda d, i: (d, 0, 0)),
        ],
        out_shape=[jax.ShapeDtypeStruct((2, batch, seq_len, S5_WIDTH), F32),
                   jax.ShapeDtypeStruct((2, rows, 2 * S5_NS), F32)],
        scratch_shapes=[pltpu.VMEM((rows_blk, 2 * S5_NS), F32), pltpu.VMEM((rows, 2 * S5_NS), F32),
                        pltpu.VMEM((S5_WIDTH // 128, rows_blk, 128), F32)],
        compiler_params=_cparams(("parallel", "arbitrary")),
        name="s5_scan",
    )(u, wb, wc, a_bar, x0)


def _gelu_tanh(x):
    return 0.5 * x * (1.0 + jnp.tanh(math.sqrt(2.0 / math.pi) * (x + 0.044715 * x * x * x)))


def _merge_kernel(x_ref, gate_ref, yhy_ref, of_ref, ob_ref, bonus_ref, g_ref, u5_ref, y5f_ref, y5b_ref,
                  gt1_ref, sh2_ref, sc2_ref,
                  hm_ref, gnw_ref, gnb_ref, s5d_ref, gluw_ref, glub_ref,
                  wbhy_ref, wbrw_ref, wbs5_ref, wout_ref, n2g_ref, rw_ref, rb_ref,
                  h_ref, hn_ref, gates_ref):
    o = of_ref[...] + ob_ref[...]
    head_mean = hm_ref[...]
    mu = _dot_exact_rhs(o, head_mean)
    oc = o - mu
    var = _dot_exact_rhs(oc * oc, head_mean)
    y_rw = (oc * lax.rsqrt(var + RW_GN_EPS) * gnw_ref[...] + gnb_ref[...] + bonus_ref[...]) * g_ref[...]
    y5 = _gelu_tanh(u5_ref[...] * s5d_ref[...] + y5f_ref[...] + y5b_ref[...])
    y5 = y5 * jax.nn.sigmoid(_dot(y5.astype(BF16), gluw_ref[...]) + glub_ref[...])
    gate = gate_ref[...].astype(F32)
    merged = (gate[:, :D_MODEL] * _dot(yhy_ref[...].astype(BF16), wbhy_ref[...])
              + gate[:, D_MODEL:2 * D_MODEL] * _dot(y_rw.astype(BF16), wbrw_ref[...])
              + gate[:, 2 * D_MODEL:] * _dot(y5.astype(BF16), wbs5_ref[...]))
    h = x_ref[...] + gt1_ref[...] * _dot(merged.astype(BF16), wout_ref[...])
    h_ref[...] = h
    hn = h * lax.rsqrt(jnp.mean(h * h, axis=-1, keepdims=True) + NORM_EPS) * n2g_ref[...]
    hn = hn * (1.0 + sc2_ref[...]) + sh2_ref[...]
    hn_ref[...] = hn.astype(BF16)

    scores = jax.nn.sigmoid(_dot3(hn, rw_ref[...]))
    sel = scores + rb_ref[...]
    scores_t, sel_t = scores.T, sel.T
    s_col = [sel_t[e:e + 1, :] for e in range(N_EXPERTS)]
    best_val = None
    best_grp = None
    for grp in range(N_EXPERT_GROUPS):
        a, b, c, dd = s_col[4 * grp:4 * grp + 4]
        hi1, lo1 = jnp.maximum(a, b), jnp.minimum(a, b)
        hi2, lo2 = jnp.maximum(c, dd), jnp.minimum(c, dd)
        m1 = jnp.maximum(hi1, hi2)
        m2 = jnp.maximum(jnp.minimum(hi1, hi2), jnp.where(hi1 >= hi2, lo1, lo2))
        val = m1 + m2
        if grp == 0:
            best_val, best_grp = val, jnp.zeros_like(val, dtype=jnp.int32)
        else:
            better = val > best_val
            best_val = jnp.where(better, val, best_val)
            best_grp = jnp.where(better, grp, best_grp)
    picked = []
    for e in range(N_EXPERTS):
        grp, pos = divmod(e, EXPERTS_PER_GROUP)
        rank = jnp.zeros_like(best_grp)
        for other in range(EXPERTS_PER_GROUP):
            if other == pos:
                continue
            so = s_col[4 * grp + other]
            ahead = (so > s_col[e]) if other > pos else (so >= s_col[e])
            rank = rank + ahead.astype(jnp.int32)
        take = jnp.logical_and(rank < 2, best_grp == grp)
        picked.append(jnp.where(take, scores_t[e:e + 1, :], 0.0))
    total = picked[0]
    for e in range(1, N_EXPERTS):
        total = total + picked[e]
    tm = gates_ref.shape[0]
    row = lax.broadcasted_iota(jnp.int32, (N_EXPERTS, tm), 0)
    gates_t = jnp.zeros((N_EXPERTS, tm), F32)
    for e in range(N_EXPERTS):
        gates_t = jnp.where(row == e, picked[e] / total, gates_t)
    gates_t = jnp.concatenate([gates_t, jnp.zeros((128 - N_EXPERTS, tm), F32)], axis=0)
    gates_ref[...] = gates_t.T


def _merge(x, gates_in, y_hy, o_f, o_b, bonus, g, u5, y5f, y5b, mod, P, l, seq_len):
    n_tok = x.shape[0]
    tm = TM_TOK
    tok = lambda w: pl.BlockSpec((tm, w), lambda i: (i, 0))
    n_mod = mod.shape[0]
    par = lambda shape: pl.BlockSpec((None,) + shape, lambda i: (l,) + tuple(0 for _ in shape))
    shared = lambda shape: pl.BlockSpec(shape, lambda i: tuple(0 for _ in shape))
    return pl.pallas_call(
        _merge_kernel,
        grid=(n_tok // tm,),
        in_specs=[
            tok(D_MODEL), tok(C_GATE), tok(HY_WIDTH), tok(RW_WIDTH), tok(RW_WIDTH), tok(RW_WIDTH),
            tok(RW_WIDTH), tok(S5_WIDTH), tok(S5_WIDTH), tok(S5_WIDTH),
            _mod_spec(2, tm, seq_len, n_mod), _mod_spec(3, tm, seq_len, n_mod), _mod_spec(4, tm, seq_len, n_mod),
            shared((RW_WIDTH, RW_WIDTH)), par((1, RW_WIDTH)), par((1, RW_WIDTH)),
            par((1, S5_WIDTH)), par((S5_WIDTH, S5_WIDTH)), par((1, S5_WIDTH)),
            par((HY_WIDTH, D_MODEL)), par((RW_WIDTH, D_MODEL)), par((S5_WIDTH, D_MODEL)),
            par((D_MODEL, D_MODEL)), par((1, D_MODEL)),
            shared((D_MODEL, 128)), shared((1, 128)),
        ],
        out_specs=[tok(D_MODEL), tok(D_MODEL), tok(128)],
        out_shape=[jax.ShapeDtypeStruct((n_tok, D_MODEL), F32),
                   jax.ShapeDtypeStruct((n_tok, D_MODEL), BF16),
                   jax.ShapeDtypeStruct((n_tok, 128), F32)],
        compiler_params=_cparams(("parallel",)),
        name="merge_router",
    )(x, gates_in, y_hy, o_f, o_b, bonus, g, u5, y5f, y5b, mod, mod, mod,
      P["head_mean"], P["rw_gn_w3"], P["rw_gn_b3"], P["s5_d3"], P["s5_glu_w_bf"], P["s5_glu_b3"],
      P["wb_hy_bf"], P["wb_rw_bf"], P["wb_s5_bf"], P["w_out_bf"], P["norm2_g3"],
      P["router_w_pad"], P["router_b_pad"])


def _moe_kernel(hn_ref, gates_ref, h_ref, gt2_ref, wg_ref, wu_ref, wd_ref, fg_ref, x_ref, acc_ref,
                *, final):
    e = pl.program_id(1)

    @pl.when(e == 0)
    def _():
        acc_ref[...] = jnp.zeros_like(acc_ref)

    hn = hn_ref[...]
    a = _dot(hn, wg_ref[...].astype(BF16))
    he = (a * jax.nn.sigmoid(a)) * _dot(hn, wu_ref[...].astype(BF16))
    lane = lax.broadcasted_iota(jnp.int32, gates_ref.shape, 1)
    gate = jnp.sum(jnp.where(lane == e, gates_ref[...], 0.0), axis=1, keepdims=True)
    acc_ref[...] += gate * _dot(he.astype(BF16), wd_ref[...].astype(BF16))

    @pl.when(e == pl.num_programs(1) - 1)
    def _():
        x = h_ref[...] + gt2_ref[...] * acc_ref[...]
        if final:
            x = x * lax.rsqrt(jnp.mean(x * x, axis=-1, keepdims=True) + NORM_EPS) * fg_ref[...]
        x_ref[...] = x


def _moe(hn, gates, h, mod, wg, wu, wd, final_g, l, seq_len, final):
    n_tok = hn.shape[0]
    tm = 1024
    tok = lambda w: pl.BlockSpec((tm, w), lambda i, e: (i, 0))
    return pl.pallas_call(
        functools.partial(_moe_kernel, final=final),
        grid=(n_tok // tm, N_EXPERTS),
        in_specs=[
            tok(D_MODEL), tok(128), tok(D_MODEL),
            _mod_spec(5, tm, seq_len, mod.shape[0]),
            pl.BlockSpec((None, None, D_MODEL, EXPERT_FF), lambda i, e: (l, e, 0, 0)),
            pl.BlockSpec((None, None, D_MODEL, EXPERT_FF), lambda i, e: (l, e, 0, 0)),
            pl.BlockSpec((None, None, EXPERT_FF, D_MODEL), lambda i, e: (l, e, 0, 0)),
            pl.BlockSpec((1, D_MODEL), lambda i, e: (0, 0)),
        ],
        out_specs=tok(D_MODEL),
        out_shape=jax.ShapeDtypeStruct((n_tok, D_MODEL), F32),
        scratch_shapes=[pltpu.VMEM((tm, D_MODEL), F32)],
        compiler_params=_cparams(("parallel", "arbitrary")),
        name="moe",
    )(hn, gates, h, mod, wg, wu, wd, final_g.reshape(1, D_MODEL))


def _block_diag(blocks):
    g, r, c = blocks.shape[-3:]
    eye = jnp.eye(g, dtype=blocks.dtype)
    out = blocks[..., :, :, None, :] * eye[:, None, :, None]
    return out.reshape(blocks.shape[:-3] + (g * r, g * c))


def _s5_params(lam_re, lam_im, log_dt, b_re, b_im, c_re, c_im):
    lr, li = lam_re.astype(F32), lam_im.astype(F32)
    dt = jnp.exp(log_dt.astype(F32))[..., None]
    mag = jnp.exp(lr * dt)
    ar, ai = mag * jnp.cos(li * dt), mag * jnp.sin(li * dt)
    den = lr * lr + li * li
    qr = ((ar - 1.0) * lr + ai * li) / den
    qi = (ai * lr - (ar - 1.0) * li) / den
    bbr = qr[..., None] * b_re - qi[..., None] * b_im
    bbi = qr[..., None] * b_im + qi[..., None] * b_re
    tr = lambda m: jnp.swapaxes(m, -1, -2)
    wb = jnp.concatenate([_block_diag(tr(bbr)), _block_diag(tr(bbi))], axis=-1)
    wc = jnp.concatenate([_block_diag(tr(c_re)), -_block_diag(tr(c_im))], axis=-2)
    a_bar = jnp.concatenate([ar.reshape(DEPTH, 2, 1, S5_NS), ai.reshape(DEPTH, 2, 1, S5_NS)], axis=-1)
    return wb.astype(BF16), wc.astype(BF16), a_bar


def kernel(x_prompt, x_sample, state_rwkv, state_s5_re, state_s5_im, c, c_ctx, ada_w, ada_b, norm1_g, norm2_g, final_g, w_in, hy_conv_w, hy_conv_b, hy_f_w1, hy_f_b1, hy_f_freq1, hy_f_w2, hy_f_b2, hy_f_freq2, hy_f_w3, hy_bias, rw_conv_w, rw_conv_b, rw_w0, rw_w_up, rw_a0, rw_a_up, rw_g_up, rw_k_k, rw_k_a, rw_r_k, rw_gn_w, rw_gn_b, s5_lam_re, s5_lam_im, s5_log_dt, s5_b_re, s5_b_im, s5_c_re, s5_c_im, s5_d, s5_glu_w, s5_glu_b, wb_hy, wb_rw, wb_s5, w_out, router_w, router_b, exp_wg, exp_wu, exp_wd):
    bc, lc = x_prompt.shape[0], x_prompt.shape[1]
    bl, ll = x_sample.shape[0], x_sample.shape[1]
    n_ctx, n_lat = bc * lc, bl * ll
    assert n_ctx == n_lat and n_ctx % 1024 == 0 and bl == 4 and bc % 8 == 0
    streams = ((lc, bc, lc), (ll, bl, GRID_W))

    head_id = np.arange(RW_WIDTH) // RW_HEAD_DIM
    head_sum = (head_id[:, None] == head_id[None, :]).astype(np.float32)
    P = dict(
        rw_conv_w=rw_conv_w, rw_conv_b3=rw_conv_b.reshape(DEPTH, 1, C_RKV),
        rw_w0c=rw_w0.reshape(DEPTH, 1, 2 * RW_WIDTH),
        rw_wupc=_block_diag(rw_w_up), rw_a0c=rw_a0.reshape(DEPTH, 1, 2 * RW_WIDTH),
        rw_aupc=_block_diag(rw_a_up), rw_g_up=rw_g_up,
        rw_k_k3=rw_k_k.reshape(DEPTH, 1, RW_WIDTH), rw_k_a3=rw_k_a.reshape(DEPTH, 1, RW_WIDTH),
        rw_r_k3=rw_r_k.reshape(DEPTH, 1, RW_WIDTH),
        head_sum=jnp.asarray(head_sum), head_mean=jnp.asarray(head_sum / RW_HEAD_DIM),
        rw_gn_w3=rw_gn_w.reshape(DEPTH, 1, RW_WIDTH), rw_gn_b3=rw_gn_b.reshape(DEPTH, 1, RW_WIDTH),
        s5_d3=s5_d.reshape(DEPTH, 1, S5_WIDTH), s5_glu_w_bf=s5_glu_w.astype(BF16),
        s5_glu_b3=s5_glu_b.reshape(DEPTH, 1, S5_WIDTH),
        wb_hy_bf=wb_hy.astype(BF16), wb_rw_bf=wb_rw.astype(BF16), wb_s5_bf=wb_s5.astype(BF16),
        w_out_bf=w_out.astype(BF16), norm2_g3=norm2_g.reshape(DEPTH, 1, D_MODEL),
        router_w_pad=jnp.pad(router_w, ((0, 0), (0, 128 - N_EXPERTS))),
        router_b_pad=jnp.pad(router_b, (0, 128 - N_EXPERTS)).reshape(1, 128),
    )
    w_in_bf = w_in.astype(BF16)
    norm1_g3 = norm1_g.reshape(DEPTH, 1, D_MODEL)
    hy_conv_b3, hy_bias3 = hy_conv_b.reshape(DEPTH, 1, C_HY), hy_bias.reshape(DEPTH, 1, HY_WIDTH)
    row3 = lambda a: a.reshape(DEPTH, 1, a.shape[-1])
    hy_filter_params = (jnp.pad(hy_f_w1, ((0, 0), (0, 128 - HY_POS_DIM), (0, 0))), row3(hy_f_b1),
                        row3(hy_f_freq1), hy_f_w2, row3(hy_f_b2), row3(hy_f_freq2), hy_f_w3)
    s5_wb, s5_wc, s5_abar = _s5_params(s5_lam_re, s5_lam_im, s5_log_dt, s5_b_re, s5_b_im, s5_c_re, s5_c_im)
    dft = {seq_len: _dft_matrices(seq_len) for seq_len in (lc, ll)}

    xs = [x_prompt.astype(F32).reshape(n_ctx, D_MODEL), x_sample.astype(F32).reshape(n_lat, D_MODEL)]
    cond8 = jnp.zeros((8, D_MODEL), F32).at[:bl].set(c.astype(F32)).at[bl].set(c_ctx.astype(F32))

    rw_s0 = (lambda l: jnp.zeros((2, bc, RW_PAIRS, RW_HEAD_DIM, 128), F32),
             lambda l: _rw_pack_state(state_rwkv[:, l].astype(F32)))
    s5_lat = jnp.concatenate([state_s5_re.reshape(bl, DEPTH, 2, S5_NS),
                              state_s5_im.reshape(bl, DEPTH, 2, S5_NS)], axis=-1).astype(F32)
    s5_lat = jnp.tile(jnp.moveaxis(s5_lat, 0, 2), (1, 1, 2, 1))
    s5_x0 = (jnp.zeros((DEPTH, 2, bc, 2 * S5_NS), F32), s5_lat)

    rw_new, s5_new = [], []
    for l in range(DEPTH):
        m = _modulation(cond8, ada_w, ada_b, l).reshape(8, 6, 1, D_MODEL)
        mods = (m[bl:bl + 1], m[:bl])
        for si, (seq_len, batch, period) in enumerate(streams):
            n_tok = seq_len * batch
            x, mod = xs[si], mods[si]
            g_in, vq, e, x1, rkv_in, lo_in, s5_in = _inproj(x, mod, norm1_g3, w_in_bf, hy_conv_w, hy_conv_b3,
                                                            hy_bias3, l, seq_len, period)
            seq = lambda a: a.reshape(batch, seq_len, a.shape[-1])
            tok = lambda a: a.reshape(n_tok, a.shape[-1])
            hs, hd, ny = _hy_filter(seq_len, l, *hy_filter_params)
            fw = dft[seq_len]
            tk = min(seq_len, 512)
            sp, sq = _hy_spec(hs, hd, fw, seq_len, tk)
            y_hy = _hy_conv(seq(vq), sp, sq, ny, fw, seq(e), seq(x1), seq_len, batch, tk)
            v_rw, bon, g_rw, at, rh, bt, kt, wt = _rw_pre(seq(rkv_in), seq(lo_in), P, l, seq_len, batch, period)
            whi, wlo, rt, ot = _rw_prep(at, rh, v_rw, bt, kt, seq_len, batch)
            o_f, o_b, s_fin = _rw_seq(whi, wlo, rt, ot, wt, rw_s0[si](l), seq_len, batch)
            y5d, fin5 = _s5_scan(seq(s5_in), s5_wb, s5_wc, s5_abar, s5_x0[si][l], l, seq_len, batch)
            y5d = y5d.reshape(2, n_tok, S5_WIDTH)
            if si == 0:
                rw_new.append(s_fin)
                s5_new.append(fin5)
            h, hn, gates = _merge(x, g_in, tok(y_hy), tok(o_f), tok(o_b), tok(bon), tok(g_rw), s5_in,
                                  y5d[0], y5d[1], mod, P, l, seq_len)
            xs[si] = _moe(hn, gates, h, mod, exp_wg, exp_wu, exp_wd, final_g, l, seq_len, final=(l == DEPTH - 1))

    y_prompt = xs[0].reshape(bc, lc, D_MODEL).astype(x_prompt.dtype)
    y_sample = xs[1].reshape(bl, ll, D_MODEL).astype(x_sample.dtype)
    new_state_rwkv = jnp.stack(rw_new, axis=1)
    s5_fin = jnp.stack(s5_new, axis=0)
    s5_fin = jnp.moveaxis(s5_fin, 2, 0)
    new_re = s5_fin[..., :S5_NS].reshape(bc, DEPTH, 2, S5_GROUPS, S5_STATE)
    new_im = s5_fin[..., S5_NS:].reshape(bc, DEPTH, 2, S5_GROUPS, S5_STATE)
    return (y_prompt, y_sample, new_state_rwkv, new_re, new_im)
```

```python
import functools
import math

import jax
import jax.numpy as jnp
import numpy as np
from jax import lax
from jax.experimental import pallas as pl
from jax.experimental.pallas import tpu as pltpu

F32 = jnp.float32
BF16 = jnp.bfloat16
HI = lax.Precision.HIGHEST

D_MODEL = 1024
DEPTH = 2
GRID_W = 64
NORM_EPS = 1e-6

HY_WIDTH = 384
HY_FILTER_HIDDEN = 64
HY_N_BANDS = 8
HY_POS_DIM = 1 + 2 * HY_N_BANDS
HY_FAST_DECAY_PCT = 0.3
HY_SLOW_DECAY_PCT = 1.5
HY_DECAY_TARGET = 1e-2

RW_HEAD_DIM = 64
RW_HEADS = 6
RW_WIDTH = RW_HEADS * RW_HEAD_DIM
RW_GN_EPS = 64e-5
RW_CHUNK = 64

S5_GROUPS = 16
S5_GROUP_CH = 16
S5_WIDTH = S5_GROUPS * S5_GROUP_CH
S5_STATE = 64
S5_NS = S5_GROUPS * S5_STATE

N_EXPERTS = 16
N_EXPERT_GROUPS = 4
EXPERTS_PER_GROUP = N_EXPERTS // N_EXPERT_GROUPS
EXPERT_FF = 512

C_GATE = 3 * D_MODEL
C_HY = 3 * HY_WIDTH
C_RKV = 3 * RW_WIDTH
C_LORA = 384
C_S5 = S5_WIDTH
IN_COLS = C_GATE + C_HY + C_RKV + C_LORA + C_S5

V7X_VMEM_LIMIT = 56 * 1024 * 1024

TM_TOK = 256
TL_SEQ = 256


def _cparams(sem):
    return pltpu.CompilerParams(dimension_semantics=sem, vmem_limit_bytes=V7X_VMEM_LIMIT)


def _dot(a, b, precision=None):
    return jnp.dot(a, b, preferred_element_type=F32, precision=precision)


def _mod_kernel(c_ref, w_ref, b_ref, o_ref):
    c = c_ref[...]
    s = c * jax.nn.sigmoid(c)
    o_ref[...] = _dot(s, w_ref[...], HI) + b_ref[...]


def _modulation(cond8, ada_w, ada_b, l):
    tn = 1536
    return pl.pallas_call(
        _mod_kernel,
        grid=(6 * D_MODEL // tn,),
        in_specs=[
            pl.BlockSpec((8, D_MODEL), lambda j: (0, 0)),
            pl.BlockSpec((None, D_MODEL, tn), lambda j: (l, 0, j)),
            pl.BlockSpec((None, 1, tn), lambda j: (l, 0, j)),
        ],
        out_specs=pl.BlockSpec((8, tn), lambda j: (0, j)),
        out_shape=jax.ShapeDtypeStruct((8, 6 * D_MODEL), F32),
        compiler_params=_cparams(("arbitrary",)),
        name="modulation",
    )(cond8, ada_w, ada_b.reshape(DEPTH, 1, 6 * D_MODEL))


def _inproj_kernel(x_ref, sh_ref, sc_ref, g_ref, w_ref, cw_ref, cb_ref, bias_ref,
                   og, ovq, oe, ox1, orkv, olo, os5, *, period):
    x = x_ref[...]
    xn = x * lax.rsqrt(jnp.mean(x * x, axis=-1, keepdims=True) + NORM_EPS) * g_ref[...]
    xn = (xn * (1.0 + sc_ref[...]) + sh_ref[...]).astype(BF16)
    off = 0
    for o_ref, width in ((og, C_GATE), (None, C_HY), (orkv, C_RKV), (olo, C_LORA), (os5, C_S5)):
        y = _dot(xn, w_ref[:, off:off + width])
        off += width
        if o_ref is og:
            o_ref[...] = jax.nn.sigmoid(y).astype(BF16)
            continue
        if o_ref is not None:
            o_ref[...] = y
            continue
        u = _conv3(y, cw_ref, cb_ref, period)
        v = u[:, 2 * HY_WIDTH:] * u[:, :HY_WIDTH]
        ovq[...] = v.astype(BF16)
        oe[...] = v * bias_ref[...]
        ox1[...] = u[:, HY_WIDTH:2 * HY_WIDTH]


def _mod_spec(which, tm, seq_len, n_mod):
    if n_mod == 1:
        return pl.BlockSpec((None, None, 1, D_MODEL), lambda i, *_: (0, which, 0, 0))
    return pl.BlockSpec((None, None, 1, D_MODEL), lambda i, *_: (i * tm // seq_len, which, 0, 0))


def _inproj(x, mod, norm_g3, w_in_bf, hy_conv_w, hy_conv_b3, hy_bias3, l, seq_len, period):
    n_tok = x.shape[0]
    tm = 2 * TM_TOK
    assert tm % period == 0 and (seq_len % tm == 0 or (tm % seq_len == 0 and mod.shape[0] == 1))
    outs = ((C_GATE, BF16), (HY_WIDTH, BF16), (HY_WIDTH, F32), (HY_WIDTH, F32),
            (C_RKV, F32), (C_LORA, F32), (C_S5, F32))
    par = lambda shape: pl.BlockSpec((None,) + shape, lambda i: (l,) + tuple(0 for _ in shape))
    return pl.pallas_call(
        functools.partial(_inproj_kernel, period=period),
        grid=(n_tok // tm,),
        in_specs=[
            pl.BlockSpec((tm, D_MODEL), lambda i: (i, 0)),
            _mod_spec(0, tm, seq_len, mod.shape[0]),
            _mod_spec(1, tm, seq_len, mod.shape[0]),
            par((1, D_MODEL)),
            pl.BlockSpec((None, D_MODEL, IN_COLS), lambda i: (l, 0, 0), pipeline_mode=pl.Buffered(1)),
            par((3, C_HY)), par((1, C_HY)), par((1, HY_WIDTH)),
        ],
        out_specs=[pl.BlockSpec((tm, w), lambda i: (i, 0)) for w, _ in outs],
        out_shape=[jax.ShapeDtypeStruct((n_tok, w), dt) for w, dt in outs],
        compiler_params=_cparams(("parallel",)),
        name="inproj",
    )(x, mod, mod, norm_g3, w_in_bf, hy_conv_w, hy_conv_b3, hy_bias3)


def _conv3(x, w_ref, b_ref, period):
    n = x.shape[0]
    t = lax.broadcasted_iota(jnp.int32, x.shape, 0) % period
    prev = jnp.where(t == 0, 0.0, pltpu.roll(x, 1, axis=0))
    nxt = jnp.where(t == period - 1, 0.0, pltpu.roll(x, n - 1, axis=0))
    return prev * w_ref[0:1, :] + x * w_ref[1:2, :] + nxt * w_ref[2:3, :] + b_ref[...]


def _hy_filter_kernel(feat_ref, w1_ref, b1_ref, f1_ref, w2_ref, b2_ref, f2_ref, w3_ref, dl_ref,
                      hs_ref, hd_ref, ny_ref):
    feats = feat_ref[...]
    h = jnp.sin(f1_ref[...] * (_dot(feats, w1_ref[...], HI) + b1_ref[...]))
    h = jnp.sin(f2_ref[...] * (_dot(h, w2_ref[...], HI) + b2_ref[...]))
    h = _dot(h, w3_ref[...], HI)
    h = h * jnp.exp(-feats[:, 0:1] * dl_ref[...])
    hf, hb = h[:, :HY_WIDTH], h[:, HY_WIDTH:]
    l1 = (jnp.sum(jnp.abs(hf), axis=0, keepdims=True)
          + jnp.sum(jnp.abs(hb), axis=0, keepdims=True) + 1e-6)
    hf = hf / l1
    hb = hb / l1
    row = lax.broadcasted_iota(jnp.int32, hb.shape, 0)
    hb0 = jnp.where(row == 0, 0.0, hb)
    hs = hf + hb0
    hs_ref[...] = hs.astype(BF16)
    hd_ref[...] = (hb0 - hf).astype(BF16)
    ny_ref[...] = jnp.sum(hs * _alternating(hs.shape), axis=0, keepdims=True) * (0.5 / hs.shape[0])


def _hy_positions(seq_len):
    t = np.arange(seq_len, dtype=np.float32)
    t01 = t / np.float32(max(seq_len - 1, 1))
    bands = np.linspace(1e-4, HY_N_BANDS - 1, HY_N_BANDS, dtype=np.float32)
    ang = np.float32(2.0 * math.pi / seq_len) * t[:, None] * bands[None, :]
    feats = np.concatenate([t01[:, None], np.cos(ang), -np.sin(ang)], axis=-1).astype(np.float32)
    feats = np.pad(feats, ((0, 0), (0, 128 - HY_POS_DIM)))
    max_decay = math.log(HY_DECAY_TARGET) / HY_FAST_DECAY_PCT
    min_decay = math.log(HY_DECAY_TARGET) / HY_SLOW_DECAY_PCT
    deltas = np.abs(np.linspace(min_decay, max_decay, HY_WIDTH, dtype=np.float32))
    return jnp.asarray(feats), jnp.asarray(np.concatenate([deltas, deltas])[None, :])


def _hy_filter(seq_len, l, w1p, b1, f1, w2, b2, f2, w3):
    feats, dl = _hy_positions(seq_len)
    hid = HY_FILTER_HIDDEN
    full = lambda shape: pl.BlockSpec(shape, lambda i: tuple(0 for _ in shape))
    par = lambda shape: pl.BlockSpec((None,) + shape, lambda i: (l,) + tuple(0 for _ in shape))
    return pl.pallas_call(
        _hy_filter_kernel,
        grid=(1,),
        in_specs=[full((seq_len, 128)), par((128, hid)), par((1, hid)), par((1, hid)),
                  par((hid, hid)), par((1, hid)), par((1, hid)), par((hid, 2 * HY_WIDTH)),
                  full((1, 2 * HY_WIDTH))],
        out_specs=[full((seq_len, HY_WIDTH)), full((seq_len, HY_WIDTH)), full((1, HY_WIDTH))],
        out_shape=[jax.ShapeDtypeStruct((seq_len, HY_WIDTH), BF16)] * 2
        + [jax.ShapeDtypeStruct((1, HY_WIDTH), F32)],
        compiler_params=pltpu.CompilerParams(vmem_limit_bytes=V7X_VMEM_LIMIT),
        name="hyena_filter",
    )(feats, w1p, b1, f1, w2, b2, f2, w3, dl)


def _dft_matrices(seq_len):
    n2 = 2 * seq_len
    k = np.arange(seq_len, dtype=np.int64)
    ang = ((k[:, None] * k[None, :]) % n2).astype(np.float64) * (2.0 * math.pi / n2)
    return jnp.asarray(np.stack([np.cos(ang), np.sin(ang)]).astype(np.float32)).astype(BF16)


def _alternating(shape):
    t = lax.broadcasted_iota(jnp.int32, shape, 0)
    return (1 - 2 * (t % 2)).astype(F32)


def _hy_spec_kernel(hs_ref, hd_ref, f_ref, p_ref, q_ref, *, tk, seq_len):
    kre = _dot(f_ref[0], hs_ref[...])
    kim = _dot(f_ref[1], hd_ref[...])
    row = lax.broadcasted_iota(jnp.int32, kre.shape, 0) + pl.program_id(0) * tk
    w = jnp.where(row == 0, 0.5 / seq_len, 1.0 / seq_len)
    p_ref[...] = kre * w
    q_ref[...] = kim * w


def _hy_spec(hs, hd, fw, seq_len, tk):
    nf = seq_len // tk
    full = pl.BlockSpec((seq_len, HY_WIDTH), lambda f: (0, 0))
    tile = pl.BlockSpec((tk, HY_WIDTH), lambda f: (f, 0))
    return pl.pallas_call(
        functools.partial(_hy_spec_kernel, tk=tk, seq_len=seq_len),
        grid=(nf,),
        in_specs=[full, full, pl.BlockSpec((2, tk, seq_len), lambda f: (0, f, 0))],
        out_specs=[tile, tile],
        out_shape=[jax.ShapeDtypeStruct((seq_len, HY_WIDTH), F32)] * 2,
        compiler_params=_cparams(("parallel",)),
        name="hyena_spectrum",
    )(hs, hd, fw)


def _hy_conv_kernel(v_ref, p_ref, q_ref, ny_ref, f_ref, g_ref, e_ref, x1_ref, o_ref, acc_ref):
    f = pl.program_id(1)
    v = v_ref[...]

    @pl.when(f == 0)
    def _():
        alt = _alternating(acc_ref.shape)
        v_nyq = jnp.sum(v.astype(F32) * alt, axis=0, keepdims=True)
        acc_ref[...] = alt * (v_nyq * ny_ref[...])

    c = _dot(f_ref[0], v)
    s = _dot(f_ref[1], v)
    p, q = p_ref[...], q_ref[...]
    yre = (c * p + s * q).astype(BF16)
    yim_neg = (s * p - c * q).astype(BF16)
    acc_ref[...] += _dot(g_ref[0], yre) + _dot(g_ref[1], yim_neg)

    @pl.when(f == pl.num_programs(1) - 1)
    def _():
        o_ref[...] = (acc_ref[...] + e_ref[...]) * x1_ref[...]


def _hy_conv(vq, p, q, ny, fw, e, x1, seq_len, batch, tk):
    nf = seq_len // tk
    seq = pl.BlockSpec((None, seq_len, HY_WIDTH), lambda b, f: (b, 0, 0))
    tile = pl.BlockSpec((tk, HY_WIDTH), lambda b, f: (f, 0))
    return pl.pallas_call(
        _hy_conv_kernel,
        grid=(batch, nf),
        in_specs=[seq, tile, tile, pl.BlockSpec((1, HY_WIDTH), lambda b, f: (0, 0)),
                  pl.BlockSpec((2, tk, seq_len), lambda b, f: (0, f, 0)),
                  pl.BlockSpec((2, seq_len, tk), lambda b, f: (0, 0, f)),
                  seq, seq],
        out_specs=seq,
        out_shape=jax.ShapeDtypeStruct((batch, seq_len, HY_WIDTH), F32),
        scratch_shapes=[pltpu.VMEM((seq_len, HY_WIDTH), F32)],
        compiler_params=_cparams(("parallel", "arbitrary")),
        name="hyena_longconv",
    )(vq, p, q, ny, fw, fw, e, x1)


def _split_bf16(x):
    hi = x.astype(BF16)
    return hi, (x - hi.astype(F32)).astype(BF16)


def _dot3(a, b):
    ah, al = _split_bf16(a)
    bh, bl = _split_bf16(b)
    n = a.shape[0]
    both = _dot(jnp.concatenate([ah, al], axis=0), bh)
    return both[:n] + both[n:] + _dot(ah, bl)


def _dot_exact_rhs(a, b):
    ah, al = _split_bf16(a)
    n = a.shape[0]
    both = _dot(jnp.concatenate([ah, al], axis=0), b.astype(BF16))
    return both[:n] + both[n:]


def _dot_exact_lhs(a, b):
    bh, bl = _split_bf16(b)
    a16 = a.astype(BF16)
    return _dot(a16, bh) + _dot(a16, bl)


def _dot_nt(a, b):
    return lax.dot_general(a, b, (((1,), (1,)), ((), ())), preferred_element_type=F32)


def _rw_pre_kernel(rkv_ref, lo_ref, cw_ref, cb_ref, w0_ref, wup_ref, a0_ref, aup_ref, gup_ref,
                   kk_ref, ka_ref, rk_ref, hs_ref, tri_ref,
                   v_out, bonus_out, g_out, at_out, rh_out, bt_out, kt_out, wt_out,
                   *, period):
    T = RW_CHUNK
    u = _conv3(rkv_ref[...], cw_ref, cb_ref, period)
    r, k, v = u[:, :RW_WIDTH], u[:, RW_WIDTH:2 * RW_WIDTH], u[:, 2 * RW_WIDTH:]
    lo = lo_ref[...]
    wraw = w0_ref[...] + _dot3(jnp.tanh(lo[:, 0:128]), wup_ref[...])
    z = -wraw
    softplus = jnp.maximum(z, 0.0) + jnp.log(1.0 + jnp.exp(-jnp.abs(z)))
    logw = -jnp.exp(-softplus - 0.5)
    a = jax.nn.sigmoid(a0_ref[...] + _dot3(lo[:, 128:256], aup_ref[...]))
    g_out[...] = _dot3(jax.nn.sigmoid(lo[:, 256:384]), gup_ref[...])
    head_sum = hs_ref[...]
    kk = k * kk_ref[...]
    kk = kk * lax.rsqrt(jnp.maximum(_dot_exact_rhs(kk * kk, head_sum), 1e-24))
    v_out[...] = v
    n_chunk = r.shape[0] // T
    lane = lax.broadcasted_iota(jnp.int32, (RW_WIDTH, 128), 1)
    kt_sum = jnp.zeros_like(k)
    for d in range(2):
        a_d = a[:, d * RW_WIDTH:(d + 1) * RW_WIDTH]
        lw_d = logw[:, d * RW_WIDTH:(d + 1) * RW_WIDTH]
        kt_d = k * (1.0 + (a_d - 1.0) * ka_ref[...])
        kt_sum = kt_sum + kt_d
        cum = _dot_exact_lhs(tri_ref[d], lw_d)
        w_inv = jnp.exp(-cum)
        at_out[d] = -kk * jnp.exp(cum - lw_d)
        rh_out[d] = r * jnp.exp(cum)
        bt = (kk * a_d * w_inv).T
        ktt = (kt_d * w_inv).T
        for j in range(0, n_chunk, 2):
            for src, dst in ((bt, bt_out), (ktt, kt_out)):
                pair = src[:, j * T:(j + 2) * T]
                swapped = pltpu.roll(pair, T, axis=1)
                dst[d, j] = jnp.where(lane < T, pair, swapped)
                dst[d, j + 1] = jnp.where(lane < T, swapped, pair)
        for j in range(n_chunk):
            last = j * T + (T - 1 if d == 0 else 0)
            wt_out[d, j] = jnp.exp(cum[last:last + 1, :])
    bonus_out[...] = _dot_exact_rhs(r * kt_sum * rk_ref[...], head_sum) * v


def _chunk_tri(n_rows):
    i = np.arange(n_rows)
    same = (i[:, None] // RW_CHUNK) == (i[None, :] // RW_CHUNK)
    fwd = same & (i[None, :] <= i[:, None])
    bwd = same & (i[None, :] >= i[:, None])
    return jnp.asarray(np.stack([fwd, bwd]).astype(np.float32))


def _rw_pre(rkv2d, lora2d, P, l, seq_len, batch, period):
    nt = seq_len // TL_SEQ
    cpt = TL_SEQ // RW_CHUNK
    nchunk = seq_len // RW_CHUNK
    blk = lambda w: pl.BlockSpec((None, TL_SEQ, w), lambda b, i: (b, i, 0))
    par = lambda shape: pl.BlockSpec((None,) + shape, lambda b, i: (l,) + tuple(0 for _ in shape))
    row_shape = jax.ShapeDtypeStruct((batch, seq_len, RW_WIDTH), F32)
    dir_shape = jax.ShapeDtypeStruct((2, batch, seq_len, RW_WIDTH), F32)
    dir_blk = pl.BlockSpec((2, None, TL_SEQ, RW_WIDTH), lambda b, i: (0, b, i, 0))
    tr_shape = jax.ShapeDtypeStruct((2, batch, nchunk, RW_WIDTH, 128), F32)
    tr_blk = pl.BlockSpec((2, None, cpt, RW_WIDTH, 128), lambda b, i: (0, b, i, 0, 0))
    wt_shape = jax.ShapeDtypeStruct((2, batch, nchunk, 1, RW_WIDTH), F32)
    wt_blk = pl.BlockSpec((2, None, cpt, 1, RW_WIDTH), lambda b, i: (0, b, i, 0, 0))
    return pl.pallas_call(
        functools.partial(_rw_pre_kernel, period=period),
        grid=(batch, nt),
        in_specs=[
            blk(C_RKV), blk(C_LORA),
            par((3, C_RKV)), par((1, C_RKV)),
            par((1, 2 * RW_WIDTH)), par((128, 2 * RW_WIDTH)),
            par((1, 2 * RW_WIDTH)), par((128, 2 * RW_WIDTH)),
            par((128, RW_WIDTH)),
            par((1, RW_WIDTH)), par((1, RW_WIDTH)), par((1, RW_WIDTH)),
            pl.BlockSpec((RW_WIDTH, RW_WIDTH), lambda b, i: (0, 0)),
            pl.BlockSpec((2, TL_SEQ, TL_SEQ), lambda b, i: (0, 0, 0)),
        ],
        out_specs=[blk(RW_WIDTH)] * 3 + [dir_blk, dir_blk, tr_blk, tr_blk, wt_blk],
        out_shape=[row_shape] * 3 + [dir_shape, dir_shape, tr_shape, tr_shape, wt_shape],
        compiler_params=_cparams(("parallel", "parallel")),
        name="rwkv_pre",
    )(rkv2d, lora2d, P["rw_conv_w"], P["rw_conv_b3"], P["rw_w0c"], P["rw_wupc"], P["rw_a0c"],
      P["rw_aupc"], P["rw_g_up"], P["rw_k_k3"], P["rw_k_a3"], P["rw_r_k3"], P["head_sum"],
      _chunk_tri(TL_SEQ))


RW_PAIRS = RW_HEADS // 2
RW_PREP_CHUNKS = 4


def _rw_prep_kernel(at_ref, rh_ref, v_ref, bt_ref, kt_ref, whi_ref, wlo_ref, rt_ref, ot_ref):
    d = pl.program_id(0)
    T = RW_CHUNK
    sgn = 1 - 2 * d
    ti = lax.broadcasted_iota(jnp.int32, (T, 128), 0)
    tj = lax.broadcasted_iota(jnp.int32, (T, 128), 1) % T
    before = (tj - ti) * sgn < 0
    before_eq = (tj - ti) * sgn <= 0
    eye2 = (ti == tj).astype(F32)
    bi = lax.broadcasted_iota(jnp.int32, (128, 128), 0)
    bj = lax.broadcasted_iota(jnp.int32, (128, 128), 1)
    same_head = (bi // T) == (bj // T)
    eye128 = (bi == bj).astype(F32)

    def bdiag(x):
        return jnp.where(same_head, jnp.concatenate([x, x], axis=0), 0.0)

    probs = [(j, p) for j in range(RW_PREP_CHUNKS) for p in range(RW_PAIRS)]
    rows = lambda j: slice(j * T, (j + 1) * T)
    cols = lambda p: slice(p * 128, (p + 1) * 128)
    a_l = [at_ref[rows(j), cols(p)] for j, p in probs]
    r_l = [rh_ref[rows(j), cols(p)] for j, p in probs]
    bd_b = [jnp.where(same_head, bt_ref[j, cols(p), :], 0.0) for j, p in probs]
    bd_k = [jnp.where(same_head, kt_ref[j, cols(p), :], 0.0) for j, p in probs]
    gram = [_dot3(jnp.concatenate([a, r], axis=0), jnp.concatenate([b, k], axis=1))
            for a, r, b, k in zip(a_l, r_l, bd_b, bd_k)]
    l_ab = [jnp.where(before, g[:T, :128], 0.0) for g in gram]
    l_ak = [jnp.where(before, g[:T, 128:], 0.0) for g in gram]
    g_rb = [jnp.where(before_eq, g[T:, :128], 0.0) for g in gram]
    g_rk = [jnp.where(before_eq, g[T:, 128:], 0.0) for g in gram]
    m = [eye2 + l for l in l_ab]
    lp = [_dot3(l, bdiag(l)) for l in l_ab]
    n_sq = int(math.log2(T)) - 1
    for s in range(n_sq):
        if s < n_sq - 1:
            out = [_dot3(jnp.concatenate([mm, ll], axis=0), bdiag(ll)) for mm, ll in zip(m, lp)]
            m = [mm + o[:T] for mm, o in zip(m, out)]
            lp = [o[T:] for o in out]
        else:
            m = [mm + _dot3(mm, bdiag(ll)) for mm, ll in zip(m, lp)]
    x = [_dot3(mm, jnp.concatenate([bdiag(a), bdiag(l)], axis=1))
         for mm, a, l in zip(m, a_l, l_ak)]
    y = [_dot3(bt_ref[j, cols(p), 0:T], xx) for (j, p), xx in zip(probs, x)]
    bd_v = [bdiag(v_ref[rows(j), cols(p)]) for j, p in probs]
    psi = [_dot3(jnp.where(same_head, yy[:, 128:], 0.0) + k, v)
           for yy, k, v in zip(y, bd_k, bd_v)]
    for (j, p), yy, ps in zip(probs, y, psi):
        phi = jnp.where(same_head, eye128 + yy[:, :128], 0.0)
        w_bd = jnp.concatenate([phi, ps], axis=1)
        w_hi, w_lo = _split_bf16(w_bd[:T] + w_bd[T:])
        whi_ref[j, p] = w_hi
        wlo_ref[j, p] = w_lo
    z = [_dot(g.astype(BF16), jnp.concatenate([bdiag(xx[:, :128]), bdiag(xx[:, 128:])], axis=1).astype(BF16))
         for g, xx in zip(g_rb, x)]
    for (j, p), zz, r, g, v in zip(probs, z, r_l, g_rk, bd_v):
        rt_ref[rows(j), cols(p)] = (r + zz[:, :128]).astype(BF16)
        ot_ref[rows(j), cols(p)] = _dot((zz[:, 128:] + g).astype(BF16), v.astype(BF16))


def _rw_prep(at, rh, v, bt, kt, seq_len, batch):
    nchunk = seq_len // RW_CHUNK
    cb = RW_PREP_CHUNKS
    rows = cb * RW_CHUNK
    drow = pl.BlockSpec((None, None, rows, RW_WIDTH), lambda d, b, i: (d, b, i, 0))
    tr = pl.BlockSpec((None, None, cb, RW_WIDTH, 128), lambda d, b, i: (d, b, i, 0, 0))
    wsp = pl.BlockSpec((None, None, cb, RW_PAIRS, RW_CHUNK, 256), lambda d, b, i: (d, b, i, 0, 0, 0))
    w_shape = jax.ShapeDtypeStruct((2, batch, nchunk, RW_PAIRS, RW_CHUNK, 256), BF16)
    return pl.pallas_call(
        _rw_prep_kernel,
        grid=(2, batch, nchunk // cb),
        in_specs=[drow, drow, pl.BlockSpec((None, rows, RW_WIDTH), lambda d, b, i: (b, i, 0)), tr, tr],
        out_specs=[wsp, wsp, drow, drow],
        out_shape=[w_shape, w_shape,
                   jax.ShapeDtypeStruct((2, batch, seq_len, RW_WIDTH), BF16),
                   jax.ShapeDtypeStruct((2, batch, seq_len, RW_WIDTH), F32)],
        compiler_params=_cparams(("parallel", "parallel", "parallel")),
        name="rwkv_chunk_ops",
    )(at, rh, v, bt, kt)


def _rw_seq_kernel(whi_f, wlo_f, whi_b, wlo_b, rt_f, rt_b, ot_f, ot_b, wt_f, wt_b, s0_ref,
                   o_f, o_b, sfin_ref, st_ref, *, bb):
    c = pl.program_id(1)
    T = RW_CHUNK

    @pl.when(c == 0)
    def _():
        st_ref[...] = s0_ref[...]

    bi_ = lax.broadcasted_iota(jnp.int32, (128, 128), 0)
    bj_ = lax.broadcasted_iota(jnp.int32, (128, 128), 1)
    same_head = (bi_ // T) == (bj_ // T)
    ji = lax.broadcasted_iota(jnp.int32, (T, 128), 0)
    jj = lax.broadcasted_iota(jnp.int32, (T, 128), 1) % T
    eye2 = (ji == jj).astype(BF16)

    dirs = ((whi_f, wlo_f, rt_f, ot_f, wt_f, o_f), (whi_b, wlo_b, rt_b, ot_b, wt_b, o_b))
    probs = [(d, b, p) for d in range(2) for b in range(bb) for p in range(RW_PAIRS)]
    cols = lambda p: slice(p * 128, (p + 1) * 128)
    s_l =[st_ref[d, b, p] for d, b, p in probs]
    split = [_split_bf16(s) for s in s_l]
    zero = jnp.zeros((T, 128), BF16)
    lhs = [jnp.concatenate([jnp.concatenate([hi, eye2], axis=1), jnp.concatenate([lo, zero], axis=1)], axis=0)
           for hi, lo in split]
    ci = lax.broadcasted_iota(jnp.int32, (128, 256), 0)
    cj = lax.broadcasted_iota(jnp.int32, (128, 256), 1)
    own_block = (ci // T) == ((cj % 128) // T)

    def unpack(w):
        return jnp.where(own_block, jnp.concatenate([w, w], axis=0), 0)

    both = [_dot_nt(lh, unpack(dirs[d][0][b, p])) for (d, b, p), lh in zip(probs, lhs)]
    new = [bt[:T] + bt[T:] + _dot_nt(lh[:T], unpack(dirs[d][1][b, p]))
           for (d, b, p), lh, bt in zip(probs, lhs, both)]
    for (d, b, p), nw in zip(probs, new):
        st_ref[d, b, p] = nw * dirs[d][4][b, :, p * 128:(p + 1) * 128]
    for (d, b, p), s in zip(probs, s_l):
        _, _, rt, ot, _, o = dirs[d]
        bd_s = jnp.where(same_head, jnp.concatenate([s, s], axis=0), 0.0).astype(BF16)
        o[b, :, cols(p)] = _dot_nt(rt[b, :, cols(p)], bd_s) + ot[b, :, cols(p)]

    @pl.when(c == pl.num_programs(1) - 1)
    def _():
        for d, b, p in probs:
            s = st_ref[d, b, p]
            for hh in range(2):
                sfin_ref[b, d, 2 * p + hh] = s[:, hh * RW_HEAD_DIM:(hh + 1) * RW_HEAD_DIM]


def _rw_seq(whi, wlo, rt, ot, wt, s0p, seq_len, batch):
    nc = seq_len // RW_CHUNK
    bb = min(batch, 8)
    rev = lambda c: nc - 1 - c
    fwd = lambda c: c
    wsp = lambda d, ch: pl.BlockSpec((None, bb, None, RW_PAIRS, RW_CHUNK, 256),
                                     lambda g, c: (d, g, ch(c), 0, 0, 0))
    row = lambda d, ch: pl.BlockSpec((None, bb, RW_CHUNK, RW_WIDTH), lambda g, c: (d, g, ch(c), 0))
    wts = lambda d, ch: pl.BlockSpec((None, bb, None, 1, RW_WIDTH), lambda g, c: (d, g, ch(c), 0, 0))
    st = pl.BlockSpec((2, bb, RW_PAIRS, RW_HEAD_DIM, 128), lambda g, c: (0, g, 0, 0, 0))
    o_spec = lambda ch: pl.BlockSpec((bb, RW_CHUNK, RW_WIDTH), lambda g, c: (g, ch(c), 0))
    o_shape = jax.ShapeDtypeStruct((batch, seq_len, RW_WIDTH), F32)
    return pl.pallas_call(
        functools.partial(_rw_seq_kernel, bb=bb),
        grid=(batch // bb, nc),
        in_specs=[wsp(0, fwd), wsp(0, fwd), wsp(1, rev), wsp(1, rev),
                  row(0, fwd), row(1, rev), row(0, fwd), row(1, rev),
                  wts(0, fwd), wts(1, rev), st],
        out_specs=[o_spec(fwd), o_spec(rev),
                   pl.BlockSpec((bb, 2, RW_HEADS, RW_HEAD_DIM, RW_HEAD_DIM), lambda g, c: (g, 0, 0, 0, 0))],
        out_shape=[o_shape, o_shape,
                   jax.ShapeDtypeStruct((batch, 2, RW_HEADS, RW_HEAD_DIM, RW_HEAD_DIM), F32)],
        scratch_shapes=[pltpu.VMEM((2, bb, RW_PAIRS, RW_HEAD_DIM, 128), F32)],
        compiler_params=_cparams(("parallel", "arbitrary")),
        name="rwkv_state_scan",
    )(whi, wlo, whi, wlo, rt, rt, ot, ot, wt, wt, s0p)


def _rw_pack_state(s):
    b = s.shape[0]
    s = s.reshape(b, 2, RW_PAIRS, 2, RW_HEAD_DIM, RW_HEAD_DIM)
    return jnp.transpose(s, (1, 0, 2, 4, 3, 5)).reshape(2, b, RW_PAIRS, RW_HEAD_DIM, 128)


def _s5_kernel(u_ref, wb_ref, wc_ref, a_ref, x0_ref, y_ref, fin_ref, bu_ref, st_ref, tm_ref, *, batch, steps):
    d = pl.program_id(0)
    i = pl.program_id(1)
    ns = S5_NS
    halves = S5_WIDTH // 128

    @pl.when(i == 0)
    def _():
        st_ref[...] = x0_ref[...]

    for b in range(batch):
        for k in range(halves):
            tm_ref[k, pl.ds(b, steps, stride=batch), :] = u_ref[b, :, k * 128:(k + 1) * 128]
    u_tm = jnp.concatenate([tm_ref[k] for k in range(halves)], axis=1)
    bu_ref[...] = _dot(u_tm.astype(BF16), wb_ref[...])

    if batch % 8 == 0:
        lanes = 256
        for lc in range(ns // lanes):
            re_sl = slice(lc * lanes, (lc + 1) * lanes)
            im_sl = slice(ns + lc * lanes, ns + (lc + 1) * lanes)
            ar, ai = a_ref[:, re_sl], a_ref[:, im_sl]

            def body(s, carry, re_sl=re_sl, im_sl=im_sl, ar=ar, ai=ai):
                xr, xi = carry
                step = s + d * (steps - 1 - 2 * s)
                rows = pl.ds(pl.multiple_of(step * batch, 8), batch)
                nr = ar * xr - ai * xi + bu_ref[rows, re_sl]
                ni = ar * xi + ai * xr + bu_ref[rows, im_sl]
                bu_ref[rows, re_sl] = nr
                bu_ref[rows, im_sl] = ni
                return nr, ni

            xr, xi = lax.fori_loop(0, steps, body, (st_ref[:, re_sl], st_ref[:, im_sl]))
            st_ref[:, re_sl] = xr
            st_ref[:, im_sl] = xi
    else:
        ar, ai = a_ref[:, :ns], a_ref[:, ns:]
        half = lax.broadcasted_iota(jnp.int32, (8, ns), 0) // 4
        first = half == d

        def body(s, carry):
            xr, xi = carry
            pair = s + d * (steps // 2 - 1 - 2 * s)
            rows = pl.ds(pl.multiple_of(pair * 8, 8), 8)
            br, bi = bu_ref[rows, :ns], bu_ref[rows, ns:]
            r1 = ar * xr - ai * xi + br
            i1 = ar * xi + ai * xr + bi
            r1s, i1s = pltpu.roll(r1, 4, axis=0), pltpu.roll(i1, 4, axis=0)
            r2 = ar * r1s - ai * i1s + br
            i2 = ar * i1s + ai * r1s + bi
            bu_ref[rows, :ns] = jnp.where(first, r1, r2)
            bu_ref[rows, ns:] = jnp.where(first, i1, i2)
            return pltpu.roll(r2, 4, axis=0), pltpu.roll(i2, 4, axis=0)

        xr, xi = lax.fori_loop(0, steps // 2, body, (st_ref[:, :ns], st_ref[:, ns:]))
        st_ref[:, :ns] = xr
        st_ref[:, ns:] = xi

    y = _dot(bu_ref[...].astype(BF16), wc_ref[...])
    for k in range(halves):
        tm_ref[k] = y[:, k * 128:(k + 1) * 128]
    for b in range(batch):
        for k in range(halves):
            y_ref[b, :, k * 128:(k + 1) * 128] = tm_ref[k, pl.ds(b, steps, stride=batch), :]

    @pl.when(i == pl.num_programs(1) - 1)
    def _():
        fin_ref[...] = st_ref[...]


def _s5_scan(u, wb, wc, a_bar, x0, l, seq_len, batch):
    rows_blk = 1024
    steps = rows_blk // batch
    nblk = seq_len * batch // rows_blk
    rows = max(batch, 8)
    blk = lambda d, i: i + d * (nblk - 1 - 2 * i)
    return pl.pallas_call(
        functools.partial(_s5_kernel, batch=batch, steps=steps),
        grid=(2, nblk),
        in_specs=[
            pl.BlockSpec((batch, steps, S5_WIDTH), lambda d, i: (0, blk(d, i), 0)),
            pl.BlockSpec((None, None, S5_WIDTH, 2 * S5_NS), lambda d, i: (l, d, 0, 0)),
            pl.BlockSpec((None, None, 2 * S5_NS, S5_WIDTH), lambda d, i: (l, d, 0, 0)),
            pl.BlockSpec((None, None, 1, 2 * S5_NS), lambda d, i: (l, d, 0, 0)),
            pl.BlockSpec((None, rows, 2 * S5_NS), lambda d, i: (d, 0, 0)),
        ],
        out_specs=[
            pl.BlockSpec((None, batch, steps, S5_WIDTH), lambda d, i: (d, 0, blk(d, i), 0)),
            pl.BlockSpec((None, rows, 2 * S5_NS), lambda d, i: (d, 0, 0)),
        ],
        out_shape=[jax.ShapeDtypeStruct((2, batch, seq_len, S5_WIDTH), F32),
                   jax.ShapeDtypeStruct((2, rows, 2 * S5_NS), F32)],
        scratch_shapes=[pltpu.VMEM((rows_blk, 2 * S5_NS), F32), pltpu.VMEM((rows, 2 * S5_NS), F32),
                        pltpu.VMEM((S5_WIDTH // 128, rows_blk, 128), F32)],
        compiler_params=_cparams(("parallel", "arbitrary")),
        name="s5_scan",
    )(u, wb, wc, a_bar, x0)


def _gelu_tanh(x):
    return 0.5 * x * (1.0 + jnp.tanh(math.sqrt(2.0 / math.pi) * (x + 0.044715 * x * x * x)))


def _merge_kernel(x_ref, gate_ref, yhy_ref, of_ref, ob_ref, bonus_ref, g_ref, u5_ref, y5f_ref, y5b_ref,
                  gt1_ref, sh2_ref, sc2_ref,
                  hm_ref, gnw_ref, gnb_ref, s5d_ref, gluw_ref, glub_ref,
                  wbhy_ref, wbrw_ref, wbs5_ref, wout_ref, n2g_ref, rw_ref, rb_ref,
                  h_ref, hn_ref, gates_ref):
    o = of_ref[...] + ob_ref[...]
    head_mean = hm_ref[...]
    mu = _dot_exact_rhs(o, head_mean)
    oc = o - mu
    var = _dot_exact_rhs(oc * oc, head_mean)
    y_rw = (oc * lax.rsqrt(var + RW_GN_EPS) * gnw_ref[...] + gnb_ref[...] + bonus_ref[...]) * g_ref[...]
    y5 = _gelu_tanh(u5_ref[...] * s5d_ref[...] + y5f_ref[...] + y5b_ref[...])
    y5 = y5 * jax.nn.sigmoid(_dot(y5.astype(BF16), gluw_ref[...]) + glub_ref[...])
    gate = gate_ref[...].astype(F32)
    merged = (gate[:, :D_MODEL] * _dot(yhy_ref[...].astype(BF16), wbhy_ref[...])
              + gate[:, D_MODEL:2 * D_MODEL] * _dot(y_rw.astype(BF16), wbrw_ref[...])
              + gate[:, 2 * D_MODEL:] * _dot(y5.astype(BF16), wbs5_ref[...]))
    h = x_ref[...] + gt1_ref[...] * _dot(merged.astype(BF16), wout_ref[...])
    h_ref[...] = h
    hn = h * lax.rsqrt(jnp.mean(h * h, axis=-1, keepdims=True) + NORM_EPS) * n2g_ref[...]
    hn = hn * (1.0 + sc2_ref[...]) + sh2_ref[...]
    hn_ref[...] = hn.astype(BF16)

    scores = jax.nn.sigmoid(_dot3(hn, rw_ref[...]))
    sel = scores + rb_ref[...]
    scores_t, sel_t = scores.T, sel.T
    s_col = [sel_t[e:e + 1, :] for e in range(N_EXPERTS)]
    best_val = None
    best_grp = None
    for grp in range(N_EXPERT_GROUPS):
        a, b, c, dd = s_col[4 * grp:4 * grp + 4]
        hi1, lo1 = jnp.maximum(a, b), jnp.minimum(a, b)
        hi2, lo2 = jnp.maximum(c, dd), jnp.minimum(c, dd)
        m1 = jnp.maximum(hi1, hi2)
        m2 = jnp.maximum(jnp.minimum(hi1, hi2), jnp.where(hi1 >= hi2, lo1, lo2))
        val = m1 + m2
        if grp == 0:
            best_val, best_grp = val, jnp.zeros_like(val, dtype=jnp.int32)
        else:
            better = val > best_val
            best_val = jnp.where(better, val, best_val)
            best_grp = jnp.where(better, grp, best_grp)
    picked = []
    for e in range(N_EXPERTS):
        grp, pos = divmod(e, EXPERTS_PER_GROUP)
        rank = jnp.zeros_like(best_grp)
        for other in range(EXPERTS_PER_GROUP):
            if other == pos:
                continue
            so = s_col[4 * grp + other]
            ahead = (so > s_col[e]) if other > pos else (so >= s_col[e])
            rank = rank + ahead.astype(jnp.int32)
        take = jnp.logical_and(rank < 2, best_grp == grp)
        picked.append(jnp.where(take, scores_t[e:e + 1, :], 0.0))
    total = picked[0]
    for e in range(1, N_EXPERTS):
        total = total + picked[e]
    tm = gates_ref.shape[0]
    row = lax.broadcasted_iota(jnp.int32, (N_EXPERTS, tm), 0)
    gates_t = jnp.zeros((N_EXPERTS, tm), F32)
    for e in range(N_EXPERTS):
        gates_t = jnp.where(row == e, picked[e] / total, gates_t)
    gates_t = jnp.concatenate([gates_t, jnp.zeros((128 - N_EXPERTS, tm), F32)], axis=0)
    gates_ref[...] = gates_t.T


def _merge(x, gates_in, y_hy, o_f, o_b, bonus, g, u5, y5f, y5b, mod, P, l, seq_len):
    n_tok = x.shape[0]
    tm = TM_TOK
    tok = lambda w: pl.BlockSpec((tm, w), lambda i: (i, 0))
    n_mod = mod.shape[0]
    par = lambda shape: pl.BlockSpec((None,) + shape, lambda i: (l,) + tuple(0 for _ in shape))
    shared = lambda shape: pl.BlockSpec(shape, lambda i: tuple(0 for _ in shape))
    return pl.pallas_call(
        _merge_kernel,
        grid=(n_tok // tm,),
        in_specs=[
            tok(D_MODEL), tok(C_GATE), tok(HY_WIDTH), tok(RW_WIDTH), tok(RW_WIDTH), tok(RW_WIDTH),
            tok(RW_WIDTH), tok(S5_WIDTH), tok(S5_WIDTH), tok(S5_WIDTH),
            _mod_spec(2, tm, seq_len, n_mod), _mod_spec(3, tm, seq_len, n_mod), _mod_spec(4, tm, seq_len, n_mod),
            shared((RW_WIDTH, RW_WIDTH)), par((1, RW_WIDTH)), par((1, RW_WIDTH)),
            par((1, S5_WIDTH)), par((S5_WIDTH, S5_WIDTH)), par((1, S5_WIDTH)),
            par((HY_WIDTH, D_MODEL)), par((RW_WIDTH, D_MODEL)), par((S5_WIDTH, D_MODEL)),
            par((D_MODEL, D_MODEL)), par((1, D_MODEL)),
            shared((D_MODEL, 128)), shared((1, 128)),
        ],
        out_specs=[tok(D_MODEL), tok(D_MODEL), tok(128)],
        out_shape=[jax.ShapeDtypeStruct((n_tok, D_MODEL), F32),
                   jax.ShapeDtypeStruct((n_tok, D_MODEL), BF16),
                   jax.ShapeDtypeStruct((n_tok, 128), F32)],
        compiler_params=_cparams(("parallel",)),
        name="merge_router",
    )(x, gates_in, y_hy, o_f, o_b, bonus, g, u5, y5f, y5b, mod, mod, mod,
      P["head_mean"], P["rw_gn_w3"], P["rw_gn_b3"], P["s5_d3"], P["s5_glu_w_bf"], P["s5_glu_b3"],
      P["wb_hy_bf"], P["wb_rw_bf"], P["wb_s5_bf"], P["w_out_bf"], P["norm2_g3"],
      P["router_w_pad"], P["router_b_pad"])


def _moe_kernel(hn_ref, gates_ref, h_ref, gt2_ref, wg_ref, wu_ref, wd_ref, fg_ref, x_ref, acc_ref,
                *, final):
    e = pl.program_id(1)

    @pl.when(e == 0)
    def _():
        acc_ref[...] = jnp.zeros_like(acc_ref)

    hn = hn_ref[...]
    a = _dot(hn, wg_ref[...].astype(BF16))
    he = (a * jax.nn.sigmoid(a)) * _dot(hn, wu_ref[...].astype(BF16))
    lane = lax.broadcasted_iota(jnp.int32, gates_ref.shape, 1)
    gate = jnp.sum(jnp.where(lane == e, gates_ref[...], 0.0), axis=1, keepdims=True)
    acc_ref[...] += gate * _dot(he.astype(BF16), wd_ref[...].astype(BF16))

    @pl.when(e == pl.num_programs(1) - 1)
    def _():
        x = h_ref[...] + gt2_ref[...] * acc_ref[...]
        if final:
            x = x * lax.rsqrt(jnp.mean(x * x, axis=-1, keepdims=True) + NORM_EPS) * fg_ref[...]
        x_ref[...] = x


def _moe(hn, gates, h, mod, wg, wu, wd, final_g, l, seq_len, final):
    n_tok = hn.shape[0]
    tm = 1024
    tok = lambda w: pl.BlockSpec((tm, w), lambda i, e: (i, 0))
    return pl.pallas_call(
        functools.partial(_moe_kernel, final=final),
        grid=(n_tok // tm, N_EXPERTS),
        in_specs=[
            tok(D_MODEL), tok(128), tok(D_MODEL),
            _mod_spec(5, tm, seq_len, mod.shape[0]),
            pl.BlockSpec((None, None, D_MODEL, EXPERT_FF), lambda i, e: (l, e, 0, 0)),
            pl.BlockSpec((None, None, D_MODEL, EXPERT_FF), lambda i, e: (l, e, 0, 0)),
            pl.BlockSpec((None, None, EXPERT_FF, D_MODEL), lambda i, e: (l, e, 0, 0)),
            pl.BlockSpec((1, D_MODEL), lambda i, e: (0, 0)),
        ],
        out_specs=tok(D_MODEL),
        out_shape=jax.ShapeDtypeStruct((n_tok, D_MODEL), F32),
        scratch_shapes=[pltpu.VMEM((tm, D_MODEL), F32)],
        compiler_params=_cparams(("parallel", "arbitrary")),
        name="moe",
    )(hn, gates, h, mod, wg, wu, wd, final_g.reshape(1, D_MODEL))


def _block_diag(blocks):
    g, r, c = blocks.shape[-3:]
    eye = jnp.eye(g, dtype=blocks.dtype)
    out = blocks[..., :, :, None, :] * eye[:, None, :, None]
    return out.reshape(blocks.shape[:-3] + (g * r, g * c))


def _s5_params(lam_re, lam_im, log_dt, b_re, b_im, c_re, c_im):
    lr, li = lam_re.astype(F32), lam_im.astype(F32)
    dt = jnp.exp(log_dt.astype(F32))[..., None]
    mag = jnp.exp(lr * dt)
    ar, ai = mag * jnp.cos(li * dt), mag * jnp.sin(li * dt)
    den = lr * lr + li * li
    qr = ((ar - 1.0) * lr + ai * li) / den
    qi = (ai * lr - (ar - 1.0) * li) / den
    bbr = qr[..., None] * b_re - qi[..., None] * b_im
    bbi = qr[..., None] * b_im + qi[..., None] * b_re
    tr = lambda m: jnp.swapaxes(m, -1, -2)
    wb = jnp.concatenate([_block_diag(tr(bbr)), _block_diag(tr(bbi))], axis=-1)
    wc = jnp.concatenate([_block_diag(tr(c_re)), -_block_diag(tr(c_im))], axis=-2)
    a_bar = jnp.concatenate([ar.reshape(DEPTH, 2, 1, S5_NS), ai.reshape(DEPTH, 2, 1, S5_NS)], axis=-1)
    return wb.astype(BF16), wc.astype(BF16), a_bar


def kernel(x_prompt, x_sample, state_rwkv, state_s5_re, state_s5_im, c, c_ctx, ada_w, ada_b, norm1_g, norm2_g, final_g, w_in, hy_conv_w, hy_conv_b, hy_f_w1, hy_f_b1, hy_f_freq1, hy_f_w2, hy_f_b2, hy_f_freq2, hy_f_w3, hy_bias, rw_conv_w, rw_conv_b, rw_w0, rw_w_up, rw_a0, rw_a_up, rw_g_up, rw_k_k, rw_k_a, rw_r_k, rw_gn_w, rw_gn_b, s5_lam_re, s5_lam_im, s5_log_dt, s5_b_re, s5_b_im, s5_c_re, s5_c_im, s5_d, s5_glu_w, s5_glu_b, wb_hy, wb_rw, wb_s5, w_out, router_w, router_b, exp_wg, exp_wu, exp_wd):
    bc, lc = x_prompt.shape[0], x_prompt.shape[1]
    bl, ll = x_sample.shape[0], x_sample.shape[1]
    n_ctx, n_lat = bc * lc, bl * ll
    assert n_ctx == n_lat and n_ctx % 1024 == 0 and bl == 4 and bc % 8 == 0
    streams = ((lc, bc, lc), (ll, bl, GRID_W))

    head_id = np.arange(RW_WIDTH) // RW_HEAD_DIM
    head_sum = (head_id[:, None] == head_id[None, :]).astype(np.float32)
    P = dict(
        rw_conv_w=rw_conv_w, rw_conv_b3=rw_conv_b.reshape(DEPTH, 1, C_RKV),
        rw_w0c=rw_w0.reshape(DEPTH, 1, 2 * RW_WIDTH),
        rw_wupc=_block_diag(rw_w_up), rw_a0c=rw_a0.reshape(DEPTH, 1, 2 * RW_WIDTH),
        rw_aupc=_block_diag(rw_a_up), rw_g_up=rw_g_up,
        rw_k_k3=rw_k_k.reshape(DEPTH, 1, RW_WIDTH), rw_k_a3=rw_k_a.reshape(DEPTH, 1, RW_WIDTH),
        rw_r_k3=rw_r_k.reshape(DEPTH, 1, RW_WIDTH),
        head_sum=jnp.asarray(head_sum), head_mean=jnp.asarray(head_sum / RW_HEAD_DIM),
        rw_gn_w3=rw_gn_w.reshape(DEPTH, 1, RW_WIDTH), rw_gn_b3=rw_gn_b.reshape(DEPTH, 1, RW_WIDTH),
        s5_d3=s5_d.reshape(DEPTH, 1, S5_WIDTH), s5_glu_w_bf=s5_glu_w.astype(BF16),
        s5_glu_b3=s5_glu_b.reshape(DEPTH, 1, S5_WIDTH),
        wb_hy_bf=wb_hy.astype(BF16), wb_rw_bf=wb_rw.astype(BF16), wb_s5_bf=wb_s5.astype(BF16),
        w_out_bf=w_out.astype(BF16), norm2_g3=norm2_g.reshape(DEPTH, 1, D_MODEL),
        router_w_pad=jnp.pad(router_w, ((0, 0), (0, 128 - N_EXPERTS))),
        router_b_pad=jnp.pad(router_b, (0, 128 - N_EXPERTS)).reshape(1, 128),
    )
    w_in_bf = w_in.astype(BF16)
    norm1_g3 = norm1_g.reshape(DEPTH, 1, D_MODEL)
    hy_conv_b3, hy_bias3 = hy_conv_b.reshape(DEPTH, 1, C_HY), hy_bias.reshape(DEPTH, 1, HY_WIDTH)
    row3 = lambda a: a.reshape(DEPTH, 1, a.shape[-1])
    hy_filter_params = (jnp.pad(hy_f_w1, ((0, 0), (0, 128 - HY_POS_DIM), (0, 0))), row3(hy_f_b1),
                        row3(hy_f_freq1), hy_f_w2, row3(hy_f_b2), row3(hy_f_freq2), hy_f_w3)
    s5_wb, s5_wc, s5_abar = _s5_params(s5_lam_re, s5_lam_im, s5_log_dt, s5_b_re, s5_b_im, s5_c_re, s5_c_im)
    dft = {seq_len: _dft_matrices(seq_len) for seq_len in (lc, ll)}

    xs = [x_prompt.astype(F32).reshape(n_ctx, D_MODEL), x_sample.astype(F32).reshape(n_lat, D_MODEL)]
    cond8 = jnp.zeros((8, D_MODEL), F32).at[:bl].set(c.astype(F32)).at[bl].set(c_ctx.astype(F32))

    rw_s0 = (lambda l: jnp.zeros((2, bc, RW_PAIRS, RW_HEAD_DIM, 128), F32),
             lambda l: _rw_pack_state(state_rwkv[:, l].astype(F32)))
    s5_lat = jnp.concatenate([state_s5_re.reshape(bl, DEPTH, 2, S5_NS),
                              state_s5_im.reshape(bl, DEPTH, 2, S5_NS)], axis=-1).astype(F32)
    s5_lat = jnp.tile(jnp.moveaxis(s5_lat, 0, 2), (1, 1, 2, 1))
    s5_x0 = (jnp.zeros((DEPTH, 2, bc, 2 * S5_NS), F32), s5_lat)

    rw_new, s5_new = [], []
    for l in range(DEPTH):
        m = _modulation(cond8, ada_w, ada_b, l).reshape(8, 6, 1, D_MODEL)
        mods = (m[bl:bl + 1], m[:bl])
        for si, (seq_len, batch, period) in enumerate(streams):
            n_tok = seq_len * batch
            x, mod = xs[si], mods[si]
            g_in, vq, e, x1, rkv_in, lo_in, s5_in = _inproj(x, mod, norm1_g3, w_in_bf, hy_conv_w, hy_conv_b3,
                                                            hy_bias3, l, seq_len, period)
            seq = lambda a: a.reshape(batch, seq_len, a.shape[-1])
            tok = lambda a: a.reshape(n_tok, a.shape[-1])
            hs, hd, ny = _hy_filter(seq_len, l, *hy_filter_params)
            fw = dft[seq_len]
            tk = min(seq_len, 512)
            sp, sq = _hy_spec(hs, hd, fw, seq_len, tk)
            y_hy = _hy_conv(seq(vq), sp, sq, ny, fw, seq(e), seq(x1), seq_len, batch, tk)
            v_rw, bon, g_rw, at, rh, bt, kt, wt = _rw_pre(seq(rkv_in), seq(lo_in), P, l, seq_len, batch, period)
            whi, wlo, rt, ot = _rw_prep(at, rh, v_rw, bt, kt, seq_len, batch)
            o_f, o_b, s_fin = _rw_seq(whi, wlo, rt, ot, wt, rw_s0[si](l), seq_len, batch)
            y5d, fin5 = _s5_scan(seq(s5_in), s5_wb, s5_wc, s5_abar, s5_x0[si][l], l, seq_len, batch)
            y5d = y5d.reshape(2, n_tok, S5_WIDTH)
            if si == 0:
                rw_new.append(s_fin)
                s5_new.append(fin5)
            h, hn, gates = _merge(x, g_in, tok(y_hy), tok(o_f), tok(o_b), tok(bon), tok(g_rw), s5_in,
                                  y5d[0], y5d[1], mod, P, l, seq_len)
            xs[si] = _moe(hn, gates, h, mod, exp_wg, exp_wu, exp_wd, final_g, l, seq_len, final=(l == DEPTH - 1))

    y_prompt = xs[0].reshape(bc, lc, D_MODEL).astype(x_prompt.dtype)
    y_sample = xs[1].reshape(bl, ll, D_MODEL).astype(x_sample.dtype)
    new_state_rwkv = jnp.stack(rw_new, axis=1)
    s5_fin = jnp.stack(s5_new, axis=0)
    s5_fin = jnp.moveaxis(s5_fin, 2, 0)
    new_re = s5_fin[..., :S5_NS].reshape(bc, DEPTH, 2, S5_GROUPS, S5_STATE)
    new_im = s5_fin[..., S5_NS:].reshape(bc, DEPTH, 2, S5_GROUPS, S5_STATE)
    return (y_prompt, y_sample, new_state_rwkv, new_re, new_im)
```

```python
import functools
import math

import jax
import jax.numpy as jnp
import numpy as np
from jax import lax
from jax.experimental import pallas as pl
from jax.experimental.pallas import tpu as pltpu

F32 = jnp.float32
BF16 = jnp.bfloat16
HI = lax.Precision.HIGHEST

D_MODEL = 1024
DEPTH = 2
GRID_W = 64
NORM_EPS = 1e-6

HY_WIDTH = 384
HY_FILTER_HIDDEN = 64
HY_N_BANDS = 8
HY_POS_DIM = 1 + 2 * HY_N_BANDS
HY_FAST_DECAY_PCT = 0.3
HY_SLOW_DECAY_PCT = 1.5
HY_DECAY_TARGET = 1e-2

RW_HEAD_DIM = 64
RW_HEADS = 6
RW_WIDTH = RW_HEADS * RW_HEAD_DIM
RW_GN_EPS = 64e-5
RW_CHUNK = 64

S5_GROUPS = 16
S5_GROUP_CH = 16
S5_WIDTH = S5_GROUPS * S5_GROUP_CH
S5_STATE = 64
S5_NS = S5_GROUPS * S5_STATE

N_EXPERTS = 16
N_EXPERT_GROUPS = 4
EXPERTS_PER_GROUP = N_EXPERTS // N_EXPERT_GROUPS
EXPERT_FF = 512

C_GATE = 3 * D_MODEL
C_HY = 3 * HY_WIDTH
C_RKV = 3 * RW_WIDTH
C_LORA = 384
C_S5 = S5_WIDTH
IN_COLS = C_GATE + C_HY + C_RKV + C_LORA + C_S5

V7X_VMEM_LIMIT = 56 * 1024 * 1024

TM_TOK = 256
TL_SEQ = 256
TM_MOE = 1024
S5_ROWS = 1024


def _cparams(sem):
    return pltpu.CompilerParams(dimension_semantics=sem, vmem_limit_bytes=V7X_VMEM_LIMIT)


def _dot(a, b, precision=None):
    return jnp.dot(a, b, preferred_element_type=F32, precision=precision)


def _mod_kernel(c_ref, w_ref, b_ref, o_ref):
    c = c_ref[...]
    s = c * jax.nn.sigmoid(c)
    o_ref[...] = _dot(s, w_ref[...], HI) + b_ref[...]


def _modulation(cond8, ada_w, ada_b, l):
    tn = 1536
    return pl.pallas_call(
        _mod_kernel,
        grid=(6 * D_MODEL // tn,),
        in_specs=[
            pl.BlockSpec((8, D_MODEL), lambda j: (0, 0)),
            pl.BlockSpec((None, D_MODEL, tn), lambda j: (l, 0, j)),
            pl.BlockSpec((None, 1, tn), lambda j: (l, 0, j)),
        ],
        out_specs=pl.BlockSpec((8, tn), lambda j: (0, j)),
        out_shape=jax.ShapeDtypeStruct((8, 6 * D_MODEL), F32),
        compiler_params=_cparams(("arbitrary",)),
        name="modulation",
    )(cond8, ada_w, ada_b.reshape(DEPTH, 1, 6 * D_MODEL))


def _inproj_kernel(x_ref, sh_ref, sc_ref, g_ref, w_ref, cw_ref, cb_ref, bias_ref,
                   og, ovq, oe, ox1, orkv, olo, os5, *, period):
    x = x_ref[...]
    xn = x * lax.rsqrt(jnp.mean(x * x, axis=-1, keepdims=True) + NORM_EPS) * g_ref[...]
    xn = (xn * (1.0 + sc_ref[...]) + sh_ref[...]).astype(BF16)
    off = 0
    for o_ref, width in ((og, C_GATE), (None, C_HY), (orkv, C_RKV), (olo, C_LORA), (os5, C_S5)):
        y = _dot(xn, w_ref[:, off:off + width])
        off += width
        if o_ref is og:
            o_ref[...] = jax.nn.sigmoid(y).astype(BF16)
            continue
        if o_ref is not None:
            o_ref[...] = y
            continue
        u = _conv3(y, cw_ref, cb_ref, period)
        v = u[:, 2 * HY_WIDTH:] * u[:, :HY_WIDTH]
        ovq[...] = v.astype(BF16)
        oe[...] = v * bias_ref[...]
        ox1[...] = u[:, HY_WIDTH:2 * HY_WIDTH]


def _mod_spec(which, tm, seq_len, n_mod):
    if n_mod == 1:
        return pl.BlockSpec((None, None, 1, D_MODEL), lambda i, *_: (0, which, 0, 0))
    return pl.BlockSpec((None, None, 1, D_MODEL), lambda i, *_: (i * tm // seq_len, which, 0, 0))


def _inproj(x, mod, norm_g3, w_in_bf, hy_conv_w, hy_conv_b3, hy_bias3, l, seq_len, period):
    n_tok = x.shape[0]
    tm = TM_TOK
    assert tm % period == 0 and (seq_len % tm == 0 or (tm % seq_len == 0 and mod.shape[0] == 1))
    outs = ((C_GATE, BF16), (HY_WIDTH, BF16), (HY_WIDTH, F32), (HY_WIDTH, F32),
            (C_RKV, F32), (C_LORA, F32), (C_S5, F32))
    par = lambda shape: pl.BlockSpec((None,) + shape, lambda i: (l,) + tuple(0 for _ in shape))
    return pl.pallas_call(
        functools.partial(_inproj_kernel, period=period),
        grid=(n_tok // tm,),
        in_specs=[
            pl.BlockSpec((tm, D_MODEL), lambda i: (i, 0)),
            _mod_spec(0, tm, seq_len, mod.shape[0]),
            _mod_spec(1, tm, seq_len, mod.shape[0]),
            par((1, D_MODEL)),
            pl.BlockSpec((None, D_MODEL, IN_COLS), lambda i: (l, 0, 0), pipeline_mode=pl.Buffered(1)),
            par((3, C_HY)), par((1, C_HY)), par((1, HY_WIDTH)),
        ],
        out_specs=[pl.BlockSpec((tm, w), lambda i: (i, 0)) for w, _ in outs],
        out_shape=[jax.ShapeDtypeStruct((n_tok, w), dt) for w, dt in outs],
        compiler_params=_cparams(("parallel",)),
        name="inproj",
    )(x, mod, mod, norm_g3, w_in_bf, hy_conv_w, hy_conv_b3, hy_bias3)


def _conv3(x, w_ref, b_ref, period):
    n = x.shape[0]
    t = lax.broadcasted_iota(jnp.int32, x.shape, 0) % period
    prev = jnp.where(t == 0, 0.0, pltpu.roll(x, 1, axis=0))
    nxt = jnp.where(t == period - 1, 0.0, pltpu.roll(x, n - 1, axis=0))
    return prev * w_ref[0:1, :] + x * w_ref[1:2, :] + nxt * w_ref[2:3, :] + b_ref[...]


def _hy_filter_kernel(feat_ref, w1_ref, b1_ref, f1_ref, w2_ref, b2_ref, f2_ref, w3_ref, dl_ref,
                      hs_ref, hd_ref, ny_ref):
    feats = feat_ref[...]
    h = jnp.sin(f1_ref[...] * (_dot(feats, w1_ref[...], HI) + b1_ref[...]))
    h = jnp.sin(f2_ref[...] * (_dot(h, w2_ref[...], HI) + b2_ref[...]))
    h = _dot(h, w3_ref[...], HI)
    h = h * jnp.exp(-feats[:, 0:1] * dl_ref[...])
    hf, hb = h[:, :HY_WIDTH], h[:, HY_WIDTH:]
    l1 = (jnp.sum(jnp.abs(hf), axis=0, keepdims=True)
          + jnp.sum(jnp.abs(hb), axis=0, keepdims=True) + 1e-6)
    hf = hf / l1
    hb = hb / l1
    row = lax.broadcasted_iota(jnp.int32, hb.shape, 0)
    hb0 = jnp.where(row == 0, 0.0, hb)
    hs = hf + hb0
    hs_ref[...] = hs.astype(BF16)
    hd_ref[...] = (hb0 - hf).astype(BF16)
    ny_ref[...] = jnp.sum(hs * _alternating(hs.shape), axis=0, keepdims=True) * (0.5 / hs.shape[0])


def _hy_positions(seq_len):
    t = np.arange(seq_len, dtype=np.float32)
    t01 = t / np.float32(max(seq_len - 1, 1))
    bands = np.linspace(1e-4, HY_N_BANDS - 1, HY_N_BANDS, dtype=np.float32)
    ang = np.float32(2.0 * math.pi / seq_len) * t[:, None] * bands[None, :]
    feats = np.concatenate([t01[:, None], np.cos(ang), -np.sin(ang)], axis=-1).astype(np.float32)
    feats = np.pad(feats, ((0, 0), (0, 128 - HY_POS_DIM)))
    max_decay = math.log(HY_DECAY_TARGET) / HY_FAST_DECAY_PCT
    min_decay = math.log(HY_DECAY_TARGET) / HY_SLOW_DECAY_PCT
    deltas = np.abs(np.linspace(min_decay, max_decay, HY_WIDTH, dtype=np.float32))
    return jnp.asarray(feats), jnp.asarray(np.concatenate([deltas, deltas])[None, :])


def _hy_filter(seq_len, l, w1p, b1, f1, w2, b2, f2, w3):
    feats, dl = _hy_positions(seq_len)
    hid = HY_FILTER_HIDDEN
    full = lambda shape: pl.BlockSpec(shape, lambda i: tuple(0 for _ in shape))
    par = lambda shape: pl.BlockSpec((None,) + shape, lambda i: (l,) + tuple(0 for _ in shape))
    return pl.pallas_call(
        _hy_filter_kernel,
        grid=(1,),
        in_specs=[full((seq_len, 128)), par((128, hid)), par((1, hid)), par((1, hid)),
                  par((hid, hid)), par((1, hid)), par((1, hid)), par((hid, 2 * HY_WIDTH)),
                  full((1, 2 * HY_WIDTH))],
        out_specs=[full((seq_len, HY_WIDTH)), full((seq_len, HY_WIDTH)), full((1, HY_WIDTH))],
        out_shape=[jax.ShapeDtypeStruct((seq_len, HY_WIDTH), BF16)] * 2
        + [jax.ShapeDtypeStruct((1, HY_WIDTH), F32)],
        compiler_params=pltpu.CompilerParams(vmem_limit_bytes=V7X_VMEM_LIMIT),
        name="hyena_filter",
    )(feats, w1p, b1, f1, w2, b2, f2, w3, dl)


def _dft_matrices(seq_len):
    n2 = 2 * seq_len
    k = np.arange(seq_len, dtype=np.int64)
    ang = ((k[:, None] * k[None, :]) % n2).astype(np.float64) * (2.0 * math.pi / n2)
    return jnp.asarray(np.stack([np.cos(ang), np.sin(ang)]).astype(np.float32)).astype(BF16)


def _alternating(shape):
    t = lax.broadcasted_iota(jnp.int32, shape, 0)
    return (1 - 2 * (t % 2)).astype(F32)


def _hy_spec_kernel(hs_ref, hd_ref, f_ref, p_ref, q_ref, *, tk, seq_len):
    kre = _dot(f_ref[0], hs_ref[...])
    kim = _dot(f_ref[1], hd_ref[...])
    row = lax.broadcasted_iota(jnp.int32, kre.shape, 0) + pl.program_id(0) * tk
    w = jnp.where(row == 0, 0.5 / seq_len, 1.0 / seq_len)
    p_ref[...] = kre * w
    q_ref[...] = kim * w


def _hy_spec(hs, hd, fw, seq_len, tk):
    nf = seq_len // tk
    full = pl.BlockSpec((seq_len, HY_WIDTH), lambda f: (0, 0))
    tile = pl.BlockSpec((tk, HY_WIDTH), lambda f: (f, 0))
    return pl.pallas_call(
        functools.partial(_hy_spec_kernel, tk=tk, seq_len=seq_len),
        grid=(nf,),
        in_specs=[full, full, pl.BlockSpec((2, tk, seq_len), lambda f: (0, f, 0))],
        out_specs=[tile, tile],
        out_shape=[jax.ShapeDtypeStruct((seq_len, HY_WIDTH), F32)] * 2,
        compiler_params=_cparams(("parallel",)),
        name="hyena_spectrum",
    )(hs, hd, fw)


def _hy_conv_kernel(v_ref, p_ref, q_ref, ny_ref, f_ref, g_ref, e_ref, x1_ref, o_ref, acc_ref):
    f = pl.program_id(1)
    v = v_ref[...]

    @pl.when(f == 0)
    def _():
        alt = _alternating(acc_ref.shape)
        v_nyq = jnp.sum(v.astype(F32) * alt, axis=0, keepdims=True)
        acc_ref[...] = alt * (v_nyq * ny_ref[...])

    c = _dot(f_ref[0], v)
    s = _dot(f_ref[1], v)
    p, q = p_ref[...], q_ref[...]
    yre = (c * p + s * q).astype(BF16)
    yim_neg = (s * p - c * q).astype(BF16)
    acc_ref[...] += _dot(g_ref[0], yre) + _dot(g_ref[1], yim_neg)

    @pl.when(f == pl.num_programs(1) - 1)
    def _():
        o_ref[...] = (acc_ref[...] + e_ref[...]) * x1_ref[...]


def _hy_conv(vq, p, q, ny, fw, e, x1, seq_len, batch, tk):
    nf = seq_len // tk
    seq = pl.BlockSpec((None, seq_len, HY_WIDTH), lambda b, f: (b, 0, 0))
    tile = pl.BlockSpec((tk, HY_WIDTH), lambda b, f: (f, 0))
    return pl.pallas_call(
        _hy_conv_kernel,
        grid=(batch, nf),
        in_specs=[seq, tile, tile, pl.BlockSpec((1, HY_WIDTH), lambda b, f: (0, 0)),
                  pl.BlockSpec((2, tk, seq_len), lambda b, f: (0, f, 0)),
                  pl.BlockSpec((2, seq_len, tk), lambda b, f: (0, 0, f)),
                  seq, seq],
        out_specs=seq,
        out_shape=jax.ShapeDtypeStruct((batch, seq_len, HY_WIDTH), F32),
        scratch_shapes=[pltpu.VMEM((seq_len, HY_WIDTH), F32)],
        compiler_params=_cparams(("parallel", "arbitrary")),
        name="hyena_longconv",
    )(vq, p, q, ny, fw, fw, e, x1)


def _split_bf16(x):
    hi = x.astype(BF16)
    return hi, (x - hi.astype(F32)).astype(BF16)


def _dot3(a, b):
    ah, al = _split_bf16(a)
    bh, bl = _split_bf16(b)
    n = a.shape[0]
    both = _dot(jnp.concatenate([ah, al], axis=0), bh)
    return both[:n] + both[n:] + _dot(ah, bl)


def _dot_exact_rhs(a, b):
    ah, al = _split_bf16(a)
    n = a.shape[0]
    both = _dot(jnp.concatenate([ah, al], axis=0), b.astype(BF16))
    return both[:n] + both[n:]


def _dot_exact_lhs(a, b):
    bh, bl = _split_bf16(b)
    a16 = a.astype(BF16)
    return _dot(a16, bh) + _dot(a16, bl)


def _dot_nt(a, b):
    return lax.dot_general(a, b, (((1,), (1,)), ((), ())), preferred_element_type=F32)


def _rw_pre_kernel(rkv_ref, lo_ref, cw_ref, cb_ref, w0_ref, wup_ref, a0_ref, aup_ref, gup_ref,
                   kk_ref, ka_ref, rk_ref, hs_ref, tri_ref,
                   v_out, bonus_out, g_out, at_out, rh_out, bt_out, kt_out, wt_out,
                   *, period):
    T = RW_CHUNK
    u = _conv3(rkv_ref[...], cw_ref, cb_ref, period)
    r, k, v = u[:, :RW_WIDTH], u[:, RW_WIDTH:2 * RW_WIDTH], u[:, 2 * RW_WIDTH:]
    lo = lo_ref[...]
    wraw = w0_ref[...] + _dot3(jnp.tanh(lo[:, 0:128]), wup_ref[...])
    z = -wraw
    softplus = jnp.maximum(z, 0.0) + jnp.log(1.0 + jnp.exp(-jnp.abs(z)))
    logw = -jnp.exp(-softplus - 0.5)
    a = jax.nn.sigmoid(a0_ref[...] + _dot3(lo[:, 128:256], aup_ref[...]))
    g_out[...] = _dot3(jax.nn.sigmoid(lo[:, 256:384]), gup_ref[...])
    head_sum = hs_ref[...]
    kk = k * kk_ref[...]
    kk = kk * lax.rsqrt(jnp.maximum(_dot_exact_rhs(kk * kk, head_sum), 1e-24))
    v_out[...] = v
    n_chunk = r.shape[0] // T
    lane = lax.broadcasted_iota(jnp.int32, (RW_WIDTH, 128), 1)
    kt_sum = jnp.zeros_like(k)
    for d in range(2):
        a_d = a[:, d * RW_WIDTH:(d + 1) * RW_WIDTH]
        lw_d = logw[:, d * RW_WIDTH:(d + 1) * RW_WIDTH]
        kt_d = k * (1.0 + (a_d - 1.0) * ka_ref[...])
        kt_sum = kt_sum + kt_d
        cum = _dot_exact_lhs(tri_ref[d], lw_d)
        w_inv = jnp.exp(-cum)
        at_out[d] = -kk * jnp.exp(cum - lw_d)
        rh_out[d] = r * jnp.exp(cum)
        bt = (kk * a_d * w_inv).T
        ktt = (kt_d * w_inv).T
        for j in range(0, n_chunk, 2):
            for src, dst in ((bt, bt_out), (ktt, kt_out)):
                pair = src[:, j * T:(j + 2) * T]
                swapped = pltpu.roll(pair, T, axis=1)
                dst[d, j] = jnp.where(lane < T, pair, swapped)
                dst[d, j + 1] = jnp.where(lane < T, swapped, pair)
        for j in range(n_chunk):
            last = j * T + (T - 1 if d == 0 else 0)
            wt_out[d, j] = jnp.exp(cum[last:last + 1, :])
    bonus_out[...] = _dot_exact_rhs(r * kt_sum * rk_ref[...], head_sum) * v


def _chunk_tri(n_rows):
    i = np.arange(n_rows)
    same = (i[:, None] // RW_CHUNK) == (i[None, :] // RW_CHUNK)
    fwd = same & (i[None, :] <= i[:, None])
    bwd = same & (i[None, :] >= i[:, None])
    return jnp.asarray(np.stack([fwd, bwd]).astype(np.float32))


def _rw_pre(rkv, lora, P, l, seq_len, batch, period):
    nt = seq_len // TL_SEQ
    cpt = TL_SEQ // RW_CHUNK
    nchunk = seq_len // RW_CHUNK
    blk = lambda w: pl.BlockSpec((None, TL_SEQ, w), lambda b, i: (b, i, 0))
    par = lambda shape: pl.BlockSpec((None,) + shape, lambda b, i: (l,) + tuple(0 for _ in shape))
    row_shape = jax.ShapeDtypeStruct((batch, seq_len, RW_WIDTH), F32)
    dir_shape = jax.ShapeDtypeStruct((2, batch, seq_len, RW_WIDTH), F32)
    dir_blk = pl.BlockSpec((2, None, TL_SEQ, RW_WIDTH), lambda b, i: (0, b, i, 0))
    tr_shape = jax.ShapeDtypeStruct((2, batch, nchunk, RW_WIDTH, 128), F32)
    tr_blk = pl.BlockSpec((2, None, cpt, RW_WIDTH, 128), lambda b, i: (0, b, i, 0, 0))
    wt_shape = jax.ShapeDtypeStruct((2, batch, nchunk, 1, RW_WIDTH), F32)
    wt_blk = pl.BlockSpec((2, None, cpt, 1, RW_WIDTH), lambda b, i: (0, b, i, 0, 0))
    return pl.pallas_call(
        functools.partial(_rw_pre_kernel, period=period),
        grid=(batch, nt),
        in_specs=[
            blk(C_RKV), blk(C_LORA),
            par((3, C_RKV)), par((1, C_RKV)),
            par((1, 2 * RW_WIDTH)), par((128, 2 * RW_WIDTH)),
            par((1, 2 * RW_WIDTH)), par((128, 2 * RW_WIDTH)),
            par((128, RW_WIDTH)),
            par((1, RW_WIDTH)), par((1, RW_WIDTH)), par((1, RW_WIDTH)),
            pl.BlockSpec((RW_WIDTH, RW_WIDTH), lambda b, i: (0, 0)),
            pl.BlockSpec((2, TL_SEQ, TL_SEQ), lambda b, i: (0, 0, 0)),
        ],
        out_specs=[blk(RW_WIDTH)] * 3 + [dir_blk, dir_blk, tr_blk, tr_blk, wt_blk],
        out_shape=[row_shape] * 3 + [dir_shape, dir_shape, tr_shape, tr_shape, wt_shape],
        compiler_params=_cparams(("parallel", "parallel")),
        name="rwkv_pre",
    )(rkv, lora, P["rw_conv_w"], P["rw_conv_b3"], P["rw_w0c"], P["rw_wupc"], P["rw_a0c"],
      P["rw_aupc"], P["rw_g_up"], P["rw_k_k3"], P["rw_k_a3"], P["rw_r_k3"], P["head_sum"],
      _chunk_tri(TL_SEQ))


RW_PAIRS = RW_HEADS // 2
RW_PREP_CHUNKS = 4


def _rw_prep_kernel(at_ref, rh_ref, v_ref, bt_ref, kt_ref, whi_ref, wlo_ref, rt_ref, ot_ref):
    d = pl.program_id(0)
    T = RW_CHUNK
    sgn = 1 - 2 * d
    ti = lax.broadcasted_iota(jnp.int32, (T, 128), 0)
    tj = lax.broadcasted_iota(jnp.int32, (T, 128), 1) % T
    before = (tj - ti) * sgn < 0
    before_eq = (tj - ti) * sgn <= 0
    eye2 = (ti == tj).astype(F32)
    bi = lax.broadcasted_iota(jnp.int32, (128, 128), 0)
    bj = lax.broadcasted_iota(jnp.int32, (128, 128), 1)
    same_head = (bi // T) == (bj // T)
    eye128 = (bi == bj).astype(F32)

    def bdiag(x):
        return jnp.where(same_head, jnp.concatenate([x, x], axis=0), 0.0)

    probs = [(j, p) for j in range(RW_PREP_CHUNKS) for p in range(RW_PAIRS)]
    rows = lambda j: slice(j * T, (j + 1) * T)
    cols = lambda p: slice(p * 128, (p + 1) * 128)
    a_l = [at_ref[rows(j), cols(p)] for j, p in probs]
    r_l = [rh_ref[rows(j), cols(p)] for j, p in probs]
    bd_b = [jnp.where(same_head, bt_ref[j, cols(p), :], 0.0) for j, p in probs]
    bd_k = [jnp.where(same_head, kt_ref[j, cols(p), :], 0.0) for j, p in probs]
    bk_l = [jnp.concatenate([b, k], axis=1) for b, k in zip(bd_b, bd_k)]
    gram_a = [_dot3(a, bk) for a, bk in zip(a_l, bk_l)]
    gram_r = [_dot(r.astype(BF16), bk.astype(BF16)) for r, bk in zip(r_l, bk_l)]
    l_ab = [jnp.where(before, g[:, :128], 0.0) for g in gram_a]
    l_ak = [jnp.where(before, g[:, 128:], 0.0) for g in gram_a]
    g_rb = [jnp.where(before_eq, g[:, :128], 0.0) for g in gram_r]
    g_rk = [jnp.where(before_eq, g[:, 128:], 0.0) for g in gram_r]
    m = [eye2 + l for l in l_ab]
    lp = [_dot3(l, bdiag(l)) for l in l_ab]
    n_sq = int(math.log2(T)) - 1
    for s in range(n_sq):
        if s < n_sq - 1:
            out = [_dot3(jnp.concatenate([mm, ll], axis=0), bdiag(ll)) for mm, ll in zip(m, lp)]
            m = [mm + o[:T] for mm, o in zip(m, out)]
            lp = [o[T:] for o in out]
        else:
            m = [mm + _dot3(mm, bdiag(ll)) for mm, ll in zip(m, lp)]
    x = [_dot3(mm, jnp.concatenate([bdiag(a), bdiag(l)], axis=1))
         for mm, a, l in zip(m, a_l, l_ak)]
    y = [_dot3(bt_ref[j, cols(p), 0:T], xx) for (j, p), xx in zip(probs, x)]
    bd_v = [bdiag(v_ref[rows(j), cols(p)]) for j, p in probs]
    psi = [_dot3(jnp.where(same_head, yy[:, 128:], 0.0) + k, v)
           for yy, k, v in zip(y, bd_k, bd_v)]
    for (j, p), yy, ps in zip(probs, y, psi):
        phi = jnp.where(same_head, eye128 + yy[:, :128], 0.0)
        w_bd = jnp.concatenate([phi, ps], axis=1)
        w_hi, w_lo = _split_bf16(w_bd[:T] + w_bd[T:])
        whi_ref[j, p] = w_hi
        wlo_ref[j, p] = w_lo
    z = [_dot(g.astype(BF16), jnp.concatenate([bdiag(xx[:, :128]), bdiag(xx[:, 128:])], axis=1).astype(BF16))
         for g, xx in zip(g_rb, x)]
    for (j, p), zz, r, g, v in zip(probs, z, r_l, g_rk, bd_v):
        rt_ref[rows(j), cols(p)] = (r + zz[:, :128]).astype(BF16)
        ot_ref[rows(j), cols(p)] = _dot((zz[:, 128:] + g).astype(BF16), v.astype(BF16))


def _rw_prep(at, rh, v, bt, kt, seq_len, batch):
    nchunk = seq_len // RW_CHUNK
    cb = RW_PREP_CHUNKS
    rows = cb * RW_CHUNK
    drow = pl.BlockSpec((None, None, rows, RW_WIDTH), lambda d, b, i: (d, b, i, 0))
    tr = pl.BlockSpec((None, None, cb, RW_WIDTH, 128), lambda d, b, i: (d, b, i, 0, 0))
    wsp = pl.BlockSpec((None, None, cb, RW_PAIRS, RW_CHUNK, 256), lambda d, b, i: (d, b, i, 0, 0, 0))
    w_shape = jax.ShapeDtypeStruct((2, batch, nchunk, RW_PAIRS, RW_CHUNK, 256), BF16)
    return pl.pallas_call(
        _rw_prep_kernel,
        grid=(2, batch, nchunk // cb),
        in_specs=[drow, drow, pl.BlockSpec((None, rows, RW_WIDTH), lambda d, b, i: (b, i, 0)), tr, tr],
        out_specs=[wsp, wsp, drow, drow],
        out_shape=[w_shape, w_shape,
                   jax.ShapeDtypeStruct((2, batch, seq_len, RW_WIDTH), BF16),
                   jax.ShapeDtypeStruct((2, batch, seq_len, RW_WIDTH), F32)],
        compiler_params=_cparams(("parallel", "parallel", "parallel")),
        name="rwkv_chunk_ops",
    )(at, rh, v, bt, kt)


def _rw_seq_kernel(whi_f, wlo_f, whi_b, wlo_b, rt_f, rt_b, ot_f, ot_b, wt_f, wt_b, s0_ref,
                   o_f, o_b, sfin_ref, st_ref, *, bb):
    c = pl.program_id(1)
    T = RW_CHUNK

    @pl.when(c == 0)
    def _():
        st_ref[...] = s0_ref[...]

    bi_ = lax.broadcasted_iota(jnp.int32, (128, 128), 0)
    bj_ = lax.broadcasted_iota(jnp.int32, (128, 128), 1)
    same_head = (bi_ // T) == (bj_ // T)
    ji = lax.broadcasted_iota(jnp.int32, (T, 128), 0)
    jj = lax.broadcasted_iota(jnp.int32, (T, 128), 1) % T
    eye2 = (ji == jj).astype(BF16)

    dirs = ((whi_f, wlo_f, rt_f, ot_f, wt_f, o_f), (whi_b, wlo_b, rt_b, ot_b, wt_b, o_b))
    probs = [(d, b, p) for d in range(2) for b in range(bb) for p in range(RW_PAIRS)]
    cols = lambda p: slice(p * 128, (p + 1) * 128)
    s_l =[st_ref[d, b, p] for d, b, p in probs]
    split = [_split_bf16(s) for s in s_l]
    zero = jnp.zeros((T, 128), BF16)
    lhs = [jnp.concatenate([jnp.concatenate([hi, eye2], axis=1), jnp.concatenate([lo, zero], axis=1)], axis=0)
           for hi, lo in split]
    ci = lax.broadcasted_iota(jnp.int32, (128, 256), 0)
    cj = lax.broadcasted_iota(jnp.int32, (128, 256), 1)
    own_block = (ci // T) == ((cj % 128) // T)

    def unpack(w):
        return jnp.where(own_block, jnp.concatenate([w, w], axis=0), 0)

    both = [_dot_nt(lh, unpack(dirs[d][0][b, p])) for (d, b, p), lh in zip(probs, lhs)]
    new = [bt[:T] + bt[T:] + _dot_nt(lh[:T], unpack(dirs[d][1][b, p]))
           for (d, b, p), lh, bt in zip(probs, lhs, both)]
    for (d, b, p), nw in zip(probs, new):
        st_ref[d, b, p] = nw * dirs[d][4][b, :, p * 128:(p + 1) * 128]
    for (d, b, p), s in zip(probs, s_l):
        _, _, rt, ot, _, o = dirs[d]
        bd_s = jnp.where(same_head, jnp.concatenate([s, s], axis=0), 0.0).astype(BF16)
        o[b, :, cols(p)] = _dot_nt(rt[b, :, cols(p)], bd_s) + ot[b, :, cols(p)]

    @pl.when(c == pl.num_programs(1) - 1)
    def _():
        for d, b, p in probs:
            s = st_ref[d, b, p]
            for hh in range(2):
                sfin_ref[b, d, 2 * p + hh] = s[:, hh * RW_HEAD_DIM:(hh + 1) * RW_HEAD_DIM]


def _rw_seq(whi, wlo, rt, ot, wt, s0p, seq_len, batch):
    nc = seq_len // RW_CHUNK
    bb = min(batch, 8)
    rev = lambda c: nc - 1 - c
    fwd = lambda c: c
    wsp = lambda d, ch: pl.BlockSpec((None, bb, None, RW_PAIRS, RW_CHUNK, 256),
                                     lambda g, c: (d, g, ch(c), 0, 0, 0))
    row = lambda d, ch: pl.BlockSpec((None, bb, RW_CHUNK, RW_WIDTH), lambda g, c: (d, g, ch(c), 0))
    wts = lambda d, ch: pl.BlockSpec((None, bb, None, 1, RW_WIDTH), lambda g, c: (d, g, ch(c), 0, 0))
    st = pl.BlockSpec((2, bb, RW_PAIRS, RW_HEAD_DIM, 128), lambda g, c: (0, g, 0, 0, 0))
    o_spec = lambda ch: pl.BlockSpec((bb, RW_CHUNK, RW_WIDTH), lambda g, c: (g, ch(c), 0))
    o_shape = jax.ShapeDtypeStruct((batch, seq_len, RW_WIDTH), F32)
    return pl.pallas_call(
        functools.partial(_rw_seq_kernel, bb=bb),
        grid=(batch // bb, nc),
        in_specs=[wsp(0, fwd), wsp(0, fwd), wsp(1, rev), wsp(1, rev),
                  row(0, fwd), row(1, rev), row(0, fwd), row(1, rev),
                  wts(0, fwd), wts(1, rev), st],
        out_specs=[o_spec(fwd), o_spec(rev),
                   pl.BlockSpec((bb, 2, RW_HEADS, RW_HEAD_DIM, RW_HEAD_DIM), lambda g, c: (g, 0, 0, 0, 0))],
        out_shape=[o_shape, o_shape,
                   jax.ShapeDtypeStruct((batch, 2, RW_HEADS, RW_HEAD_DIM, RW_HEAD_DIM), F32)],
        scratch_shapes=[pltpu.VMEM((2, bb, RW_PAIRS, RW_HEAD_DIM, 128), F32)],
        compiler_params=_cparams(("parallel", "arbitrary")),
        name="rwkv_state_scan",
    )(whi, wlo, whi, wlo, rt, rt, ot, ot, wt, wt, s0p)


def _rw_pack_state(s):
    b = s.shape[0]
    s = s.reshape(b, 2, RW_PAIRS, 2, RW_HEAD_DIM, RW_HEAD_DIM)
    return jnp.transpose(s, (1, 0, 2, 4, 3, 5)).reshape(2, b, RW_PAIRS, RW_HEAD_DIM, 128)


def _s5_kernel(u_ref, wb_ref, wc_ref, a_ref, x0_ref, y_ref, fin_ref, bu_ref, st_ref, tm_ref, *, batch, steps):
    d = pl.program_id(0)
    i = pl.program_id(1)
    ns = S5_NS
    halves = S5_WIDTH // 128

    @pl.when(i == 0)
    def _():
        st_ref[...] = x0_ref[...]

    for b in range(batch):
        for k in range(halves):
            tm_ref[k, pl.ds(b, steps, stride=batch), :] = u_ref[b, :, k * 128:(k + 1) * 128]
    u_tm = jnp.concatenate([tm_ref[k] for k in range(halves)], axis=1)
    bu_ref[...] = _dot(u_tm.astype(BF16), wb_ref[...])

    if batch % 8 == 0:
        lanes = 256
        for lc in range(ns // lanes):
            re_sl = slice(lc * lanes, (lc + 1) * lanes)
            im_sl = slice(ns + lc * lanes, ns + (lc + 1) * lanes)
            ar, ai = a_ref[:, re_sl], a_ref[:, im_sl]

            def body(s, carry, re_sl=re_sl, im_sl=im_sl, ar=ar, ai=ai):
                xr, xi = carry
                step = s + d * (steps - 1 - 2 * s)
                rows = pl.ds(pl.multiple_of(step * batch, 8), batch)
                nr = ar * xr - ai * xi + bu_ref[rows, re_sl]
                ni = ar * xi + ai * xr + bu_ref[rows, im_sl]
                bu_ref[rows, re_sl] = nr
                bu_ref[rows, im_sl] = ni
                return nr, ni

            xr, xi = lax.fori_loop(0, steps, body, (st_ref[:, re_sl], st_ref[:, im_sl]))
            st_ref[:, re_sl] = xr
            st_ref[:, im_sl] = xi
    else:
        ar, ai = a_ref[:, :ns], a_ref[:, ns:]
        half = lax.broadcasted_iota(jnp.int32, (8, ns), 0) // 4
        first = half == d

        def body(s, carry):
            xr, xi = carry
            pair = s + d * (steps // 2 - 1 - 2 * s)
            rows = pl.ds(pl.multiple_of(pair * 8, 8), 8)
            br, bi = bu_ref[rows, :ns], bu_ref[rows, ns:]
            r1 = ar * xr - ai * xi + br
            i1 = ar * xi + ai * xr + bi
            r1s, i1s = pltpu.roll(r1, 4, axis=0), pltpu.roll(i1, 4, axis=0)
            r2 = ar * r1s - ai * i1s + br
            i2 = ar * i1s + ai * r1s + bi
            bu_ref[rows, :ns] = jnp.where(first, r1, r2)
            bu_ref[rows, ns:] = jnp.where(first, i1, i2)
            return pltpu.roll(r2, 4, axis=0), pltpu.roll(i2, 4, axis=0)

        xr, xi = lax.fori_loop(0, steps // 2, body, (st_ref[:, :ns], st_ref[:, ns:]))
        st_ref[:, :ns] = xr
        st_ref[:, ns:] = xi

    y = _dot(bu_ref[...].astype(BF16), wc_ref[...])
    for k in range(halves):
        tm_ref[k] = y[:, k * 128:(k + 1) * 128]
    for b in range(batch):
        for k in range(halves):
            y_ref[b, :, k * 128:(k + 1) * 128] = tm_ref[k, pl.ds(b, steps, stride=batch), :]

    @pl.when(i == pl.num_programs(1) - 1)
    def _():
        fin_ref[...] = st_ref[...]


def _s5_scan(u, wb, wc, a_bar, x0, l, seq_len, batch):
    rows_blk = S5_ROWS
    steps = rows_blk // batch
    nblk = seq_len * batch // rows_blk
    rows = max(batch, 8)
    blk = lambda d, i: i + d * (nblk - 1 - 2 * i)
    return pl.pallas_call(
        functools.partial(_s5_kernel, batch=batch, steps=steps),
        grid=(2, nblk),
        in_specs=[
            pl.BlockSpec((batch, steps, S5_WIDTH), lambda d, i: (0, blk(d, i), 0)),
            pl.BlockSpec((None, None, S5_WIDTH, 2 * S5_NS), lambda d, i: (l, d, 0, 0)),
            pl.BlockSpec((None, None, 2 * S5_NS, S5_WIDTH), lambda d, i: (l, d, 0, 0)),
            pl.BlockSpec((None, None, 1, 2 * S5_NS), lambda d, i: (l, d, 0, 0)),
            pl.BlockSpec((None, rows, 2 * S5_NS), lambda d, i: (d, 0, 0)),
        ],
        out_specs=[
            pl.BlockSpec((None, batch, steps, S5_WIDTH), lambda d, i: (d, 0, blk(d, i), 0)),
            pl.BlockSpec((None, rows, 2 * S5_NS), lambda d, i: (d, 0, 0)),
        ],
        out_shape=[jax.ShapeDtypeStruct((2, batch, seq_len, S5_WIDTH), F32),
                   jax.ShapeDtypeStruct((2, rows, 2 * S5_NS), F32)],
        scratch_shapes=[pltpu.VMEM((rows_blk, 2 * S5_NS), F32), pltpu.VMEM((rows, 2 * S5_NS), F32),
                        pltpu.VMEM((S5_WIDTH // 128, rows_blk, 128), F32)],
        compiler_params=_cparams(("parallel", "arbitrary")),
        name="s5_scan",
    )(u, wb, wc, a_bar, x0)


def _gelu_tanh(x):
    return 0.5 * x * (1.0 + jnp.tanh(math.sqrt(2.0 / math.pi) * (x + 0.044715 * x * x * x)))


def _merge_kernel(x_ref, gate_ref, yhy_ref, of_ref, ob_ref, bonus_ref, g_ref, u5_ref, y5f_ref, y5b_ref,
                  gt1_ref, sh2_ref, sc2_ref,
                  hm_ref, gnw_ref, gnb_ref, s5d_ref, gluw_ref, glub_ref,
                  wbhy_ref, wbrw_ref, wbs5_ref, wout_ref, n2g_ref, rw_ref, rb_ref,
                  h_ref, hn_ref, gates_ref):
    o = of_ref[...] + ob_ref[...]
    head_mean = hm_ref[...]
    mu = _dot_exact_rhs(o, head_mean)
    oc = o - mu
    var = _dot_exact_rhs(oc * oc, head_mean)
    y_rw = (oc * lax.rsqrt(var + RW_GN_EPS) * gnw_ref[...] + gnb_ref[...] + bonus_ref[...]) * g_ref[...]
    y5 = _gelu_tanh(u5_ref[...] * s5d_ref[...] + y5f_ref[...] + y5b_ref[...])
    y5 = y5 * jax.nn.sigmoid(_dot(y5.astype(BF16), gluw_ref[...]) + glub_ref[...])
    gate = gate_ref[...].astype(F32)
    merged = (gate[:, :D_MODEL] * _dot(yhy_ref[...].astype(BF16), wbhy_ref[...])
              + gate[:, D_MODEL:2 * D_MODEL] * _dot(y_rw.astype(BF16), wbrw_ref[...])
              + gate[:, 2 * D_MODEL:] * _dot(y5.astype(BF16), wbs5_ref[...]))
    h = x_ref[...] + gt1_ref[...] * _dot(merged.astype(BF16), wout_ref[...])
    h_ref[...] = h
    hn = h * lax.rsqrt(jnp.mean(h * h, axis=-1, keepdims=True) + NORM_EPS) * n2g_ref[...]
    hn = hn * (1.0 + sc2_ref[...]) + sh2_ref[...]
    hn_ref[...] = hn.astype(BF16)

    scores = jax.nn.sigmoid(_dot3(hn, rw_ref[...]))
    sel = scores + rb_ref[...]
    scores_t, sel_t = scores.T, sel.T
    s_col = [sel_t[e:e + 1, :] for e in range(N_EXPERTS)]
    best_val = None
    best_grp = None
    for grp in range(N_EXPERT_GROUPS):
        a, b, c, dd = s_col[4 * grp:4 * grp + 4]
        hi1, lo1 = jnp.maximum(a, b), jnp.minimum(a, b)
        hi2, lo2 = jnp.maximum(c, dd), jnp.minimum(c, dd)
        m1 = jnp.maximum(hi1, hi2)
        m2 = jnp.maximum(jnp.minimum(hi1, hi2), jnp.where(hi1 >= hi2, lo1, lo2))
        val = m1 + m2
        if grp == 0:
            best_val, best_grp = val, jnp.zeros_like(val, dtype=jnp.int32)
        else:
            better = val > best_val
            best_val = jnp.where(better, val, best_val)
            best_grp = jnp.where(better, grp, best_grp)
    picked = []
    for e in range(N_EXPERTS):
        grp, pos = divmod(e, EXPERTS_PER_GROUP)
        rank = jnp.zeros_like(best_grp)
        for other in range(EXPERTS_PER_GROUP):
            if other == pos:
                continue
            so = s_col[4 * grp + other]
            ahead = (so > s_col[e]) if other > pos else (so >= s_col[e])
            rank = rank + ahead.astype(jnp.int32)
        take = jnp.logical_and(rank < 2, best_grp == grp)
        picked.append(jnp.where(take, scores_t[e:e + 1, :], 0.0))
    total = picked[0]
    for e in range(1, N_EXPERTS):
        total = total + picked[e]
    tm = gates_ref.shape[0]
    row = lax.broadcasted_iota(jnp.int32, (N_EXPERTS, tm), 0)
    gates_t = jnp.zeros((N_EXPERTS, tm), F32)
    for e in range(N_EXPERTS):
        gates_t = jnp.where(row == e, picked[e] / total, gates_t)
    gates_t = jnp.concatenate([gates_t, jnp.zeros((128 - N_EXPERTS, tm), F32)], axis=0)
    gates_ref[...] = gates_t.T


def _merge(x, gates_in, y_hy, o_f, o_b, bonus, g, u5, y5f, y5b, mod, P, l, seq_len):
    n_tok = x.shape[0]
    tm = TM_TOK
    tok = lambda w: pl.BlockSpec((tm, w), lambda i: (i, 0))
    n_mod = mod.shape[0]
    par = lambda shape: pl.BlockSpec((None,) + shape, lambda i: (l,) + tuple(0 for _ in shape))
    shared = lambda shape: pl.BlockSpec(shape, lambda i: tuple(0 for _ in shape))
    return pl.pallas_call(
        _merge_kernel,
        grid=(n_tok // tm,),
        in_specs=[
            tok(D_MODEL), tok(C_GATE), tok(HY_WIDTH), tok(RW_WIDTH), tok(RW_WIDTH), tok(RW_WIDTH),
            tok(RW_WIDTH), tok(S5_WIDTH), tok(S5_WIDTH), tok(S5_WIDTH),
            _mod_spec(2, tm, seq_len, n_mod), _mod_spec(3, tm, seq_len, n_mod), _mod_spec(4, tm, seq_len, n_mod),
            shared((RW_WIDTH, RW_WIDTH)), par((1, RW_WIDTH)), par((1, RW_WIDTH)),
            par((1, S5_WIDTH)), par((S5_WIDTH, S5_WIDTH)), par((1, S5_WIDTH)),
            par((HY_WIDTH, D_MODEL)), par((RW_WIDTH, D_MODEL)), par((S5_WIDTH, D_MODEL)),
            par((D_MODEL, D_MODEL)), par((1, D_MODEL)),
            shared((D_MODEL, 128)), shared((1, 128)),
        ],
        out_specs=[tok(D_MODEL), tok(D_MODEL), tok(128)],
        out_shape=[jax.ShapeDtypeStruct((n_tok, D_MODEL), F32),
                   jax.ShapeDtypeStruct((n_tok, D_MODEL), BF16),
                   jax.ShapeDtypeStruct((n_tok, 128), F32)],
        compiler_params=_cparams(("parallel",)),
        name="merge_router",
    )(x, gates_in, y_hy, o_f, o_b, bonus, g, u5, y5f, y5b, mod, mod, mod,
      P["head_mean"], P["rw_gn_w3"], P["rw_gn_b3"], P["s5_d3"], P["s5_glu_w_bf"], P["s5_glu_b3"],
      P["wb_hy_bf"], P["wb_rw_bf"], P["wb_s5_bf"], P["w_out_bf"], P["norm2_g3"],
      P["router_w_pad"], P["router_b_pad"])


def _moe_kernel(hn_ref, gates_ref, h_ref, gt2_ref, wg_ref, wu_ref, wd_ref, fg_ref, x_ref, acc_ref,
                *, final):
    e = pl.program_id(1)

    @pl.when(e == 0)
    def _():
        acc_ref[...] = jnp.zeros_like(acc_ref)

    hn = hn_ref[...]
    a = _dot(hn, wg_ref[...].astype(BF16))
    he = (a * jax.nn.sigmoid(a)) * _dot(hn, wu_ref[...].astype(BF16))
    lane = lax.broadcasted_iota(jnp.int32, gates_ref.shape, 1)
    gate = jnp.sum(jnp.where(lane == e, gates_ref[...], 0.0), axis=1, keepdims=True)
    acc_ref[...] += gate * _dot(he.astype(BF16), wd_ref[...].astype(BF16))

    @pl.when(e == pl.num_programs(1) - 1)
    def _():
        x = h_ref[...] + gt2_ref[...] * acc_ref[...]
        if final:
            x = x * lax.rsqrt(jnp.mean(x * x, axis=-1, keepdims=True) + NORM_EPS) * fg_ref[...]
        x_ref[...] = x


def _moe(hn, gates, h, mod, wg, wu, wd, final_g, l, seq_len, final):
    n_tok = hn.shape[0]
    tm = TM_MOE
    tok = lambda w: pl.BlockSpec((tm, w), lambda i, e: (i, 0))
    return pl.pallas_call(
        functools.partial(_moe_kernel, final=final),
        grid=(n_tok // tm, N_EXPERTS),
        in_specs=[
            tok(D_MODEL), tok(128), tok(D_MODEL),
            _mod_spec(5, tm, seq_len, mod.shape[0]),
            pl.BlockSpec((None, None, D_MODEL, EXPERT_FF), lambda i, e: (l, e, 0, 0)),
            pl.BlockSpec((None, None, D_MODEL, EXPERT_FF), lambda i, e: (l, e, 0, 0)),
            pl.BlockSpec((None, None, EXPERT_FF, D_MODEL), lambda i, e: (l, e, 0, 0)),
            pl.BlockSpec((1, D_MODEL), lambda i, e: (0, 0)),
        ],
        out_specs=tok(D_MODEL),
        out_shape=jax.ShapeDtypeStruct((n_tok, D_MODEL), F32),
        scratch_shapes=[pltpu.VMEM((tm, D_MODEL), F32)],
        compiler_params=_cparams(("parallel", "arbitrary")),
        name="moe",
    )(hn, gates, h, mod, wg, wu, wd, final_g.reshape(1, D_MODEL))


def _block_diag(blocks):
    g, r, c = blocks.shape[-3:]
    eye = jnp.eye(g, dtype=blocks.dtype)
    out = blocks[..., :, :, None, :] * eye[:, None, :, None]
    return out.reshape(blocks.shape[:-3] + (g * r, g * c))


def _s5_params(lam_re, lam_im, log_dt, b_re, b_im, c_re, c_im):
    lr, li = lam_re.astype(F32), lam_im.astype(F32)
    dt = jnp.exp(log_dt.astype(F32))[..., None]
    mag = jnp.exp(lr * dt)
    ar, ai = mag * jnp.cos(li * dt), mag * jnp.sin(li * dt)
    den = lr * lr + li * li
    qr = ((ar - 1.0) * lr + ai * li) / den
    qi = (ai * lr - (ar - 1.0) * li) / den
    bbr = qr[..., None] * b_re - qi[..., None] * b_im
    bbi = qr[..., None] * b_im + qi[..., None] * b_re
    tr = lambda m: jnp.swapaxes(m, -1, -2)
    wb = jnp.concatenate([_block_diag(tr(bbr)), _block_diag(tr(bbi))], axis=-1)
    wc = jnp.concatenate([_block_diag(tr(c_re)), -_block_diag(tr(c_im))], axis=-2)
    a_bar = jnp.concatenate([ar.reshape(DEPTH, 2, 1, S5_NS), ai.reshape(DEPTH, 2, 1, S5_NS)], axis=-1)
    return wb.astype(BF16), wc.astype(BF16), a_bar


def kernel(x_prompt, x_sample, state_rwkv, state_s5_re, state_s5_im, c, c_ctx, ada_w, ada_b, norm1_g, norm2_g, final_g, w_in, hy_conv_w, hy_conv_b, hy_f_w1, hy_f_b1, hy_f_freq1, hy_f_w2, hy_f_b2, hy_f_freq2, hy_f_w3, hy_bias, rw_conv_w, rw_conv_b, rw_w0, rw_w_up, rw_a0, rw_a_up, rw_g_up, rw_k_k, rw_k_a, rw_r_k, rw_gn_w, rw_gn_b, s5_lam_re, s5_lam_im, s5_log_dt, s5_b_re, s5_b_im, s5_c_re, s5_c_im, s5_d, s5_glu_w, s5_glu_b, wb_hy, wb_rw, wb_s5, w_out, router_w, router_b, exp_wg, exp_wu, exp_wd):
    bc, lc = x_prompt.shape[0], x_prompt.shape[1]
    bl, ll = x_sample.shape[0], x_sample.shape[1]
    n_ctx, n_lat = bc * lc, bl * ll
    assert bl == 4 and bc % 8 == 0 and lc % TL_SEQ == 0 and ll % TL_SEQ == 0
    assert n_ctx % TM_MOE == 0 and n_lat % TM_MOE == 0
    streams = ((lc, bc, lc), (ll, bl, GRID_W))

    head_id = np.arange(RW_WIDTH) // RW_HEAD_DIM
    head_sum = (head_id[:, None] == head_id[None, :]).astype(np.float32)
    P = dict(
        rw_conv_w=rw_conv_w, rw_conv_b3=rw_conv_b.reshape(DEPTH, 1, C_RKV),
        rw_w0c=rw_w0.reshape(DEPTH, 1, 2 * RW_WIDTH),
        rw_wupc=_block_diag(rw_w_up), rw_a0c=rw_a0.reshape(DEPTH, 1, 2 * RW_WIDTH),
        rw_aupc=_block_diag(rw_a_up), rw_g_up=rw_g_up,
        rw_k_k3=rw_k_k.reshape(DEPTH, 1, RW_WIDTH), rw_k_a3=rw_k_a.reshape(DEPTH, 1, RW_WIDTH),
        rw_r_k3=rw_r_k.reshape(DEPTH, 1, RW_WIDTH),
        head_sum=jnp.asarray(head_sum), head_mean=jnp.asarray(head_sum / RW_HEAD_DIM),
        rw_gn_w3=rw_gn_w.reshape(DEPTH, 1, RW_WIDTH), rw_gn_b3=rw_gn_b.reshape(DEPTH, 1, RW_WIDTH),
        s5_d3=s5_d.reshape(DEPTH, 1, S5_WIDTH), s5_glu_w_bf=s5_glu_w.astype(BF16),
        s5_glu_b3=s5_glu_b.reshape(DEPTH, 1, S5_WIDTH),
        wb_hy_bf=wb_hy.astype(BF16), wb_rw_bf=wb_rw.astype(BF16), wb_s5_bf=wb_s5.astype(BF16),
        w_out_bf=w_out.astype(BF16), norm2_g3=norm2_g.reshape(DEPTH, 1, D_MODEL),
        router_w_pad=jnp.pad(router_w, ((0, 0), (0, 128 - N_EXPERTS))),
        router_b_pad=jnp.pad(router_b, (0, 128 - N_EXPERTS)).reshape(1, 128),
    )
    w_in_bf = w_in.astype(BF16)
    norm1_g3 = norm1_g.reshape(DEPTH, 1, D_MODEL)
    hy_conv_b3, hy_bias3 = hy_conv_b.reshape(DEPTH, 1, C_HY), hy_bias.reshape(DEPTH, 1, HY_WIDTH)
    row3 = lambda a: a.reshape(DEPTH, 1, a.shape[-1])
    hy_filter_params = (jnp.pad(hy_f_w1, ((0, 0), (0, 128 - HY_POS_DIM), (0, 0))), row3(hy_f_b1),
                        row3(hy_f_freq1), hy_f_w2, row3(hy_f_b2), row3(hy_f_freq2), hy_f_w3)
    s5_wb, s5_wc, s5_abar = _s5_params(s5_lam_re, s5_lam_im, s5_log_dt, s5_b_re, s5_b_im, s5_c_re, s5_c_im)
    dft = {seq_len: _dft_matrices(seq_len) for seq_len in (lc, ll)}

    xs = [x_prompt.astype(F32).reshape(n_ctx, D_MODEL), x_sample.astype(F32).reshape(n_lat, D_MODEL)]
    cond8 = jnp.zeros((8, D_MODEL), F32).at[:bl].set(c.astype(F32)).at[bl].set(c_ctx.astype(F32))

    rw_s0 = (lambda l: jnp.zeros((2, bc, RW_PAIRS, RW_HEAD_DIM, 128), F32),
             lambda l: _rw_pack_state(state_rwkv[:, l].astype(F32)))
    s5_lat = jnp.concatenate([state_s5_re.reshape(bl, DEPTH, 2, S5_NS),
                              state_s5_im.reshape(bl, DEPTH, 2, S5_NS)], axis=-1).astype(F32)
    s5_lat = jnp.tile(jnp.moveaxis(s5_lat, 0, 2), (1, 1, 2, 1))
    s5_x0 = (jnp.zeros((DEPTH, 2, bc, 2 * S5_NS), F32), s5_lat)

    rw_new, s5_new = [], []
    for l in range(DEPTH):
        m = _modulation(cond8, ada_w, ada_b, l).reshape(8, 6, 1, D_MODEL)
        mods = (m[bl:bl + 1], m[:bl])
        for si, (seq_len, batch, period) in enumerate(streams):
            n_tok = seq_len * batch
            x, mod = xs[si], mods[si]
            g_in, vq, e, x1, rkv_in, lo_in, s5_in = _inproj(x, mod, norm1_g3, w_in_bf, hy_conv_w, hy_conv_b3,
                                                            hy_bias3, l, seq_len, period)
            seq = lambda a: a.reshape(batch, seq_len, a.shape[-1])
            tok = lambda a: a.reshape(n_tok, a.shape[-1])
            hs, hd, ny = _hy_filter(seq_len, l, *hy_filter_params)
            fw = dft[seq_len]
            tk = min(seq_len, 512)
            sp, sq = _hy_spec(hs, hd, fw, seq_len, tk)
            y_hy = _hy_conv(seq(vq), sp, sq, ny, fw, seq(e), seq(x1), seq_len, batch, tk)
            v_rw, bon, g_rw, at, rh, bt, kt, wt = _rw_pre(seq(rkv_in), seq(lo_in), P, l, seq_len, batch, period)
            whi, wlo, rt, ot = _rw_prep(at, rh, v_rw, bt, kt, seq_len, batch)
            o_f, o_b, s_fin = _rw_seq(whi, wlo, rt, ot, wt, rw_s0[si](l), seq_len, batch)
            y5d, fin5 = _s5_scan(seq(s5_in), s5_wb, s5_wc, s5_abar, s5_x0[si][l], l, seq_len, batch)
            y5d = y5d.reshape(2, n_tok, S5_WIDTH)
            if si == 0:
                rw_new.append(s_fin)
                s5_new.append(fin5)
            h, hn, gates = _merge(x, g_in, tok(y_hy), tok(o_f), tok(o_b), tok(bon), tok(g_rw), s5_in,
                                  y5d[0], y5d[1], mod, P, l, seq_len)
            xs[si] = _moe(hn, gates, h, mod, exp_wg, exp_wu, exp_wd, final_g, l, seq_len, final=(l == DEPTH - 1))

    y_prompt = xs[0].reshape(bc, lc, D_MODEL).astype(x_prompt.dtype)
    y_sample = xs[1].reshape(bl, ll, D_MODEL).astype(x_sample.dtype)
    new_state_rwkv = jnp.stack(rw_new, axis=1)
    s5_fin = jnp.stack(s5_new, axis=0)
    s5_fin = jnp.moveaxis(s5_fin, 2, 0)
    new_re = s5_fin[..., :S5_NS].reshape(bc, DEPTH, 2, S5_GROUPS, S5_STATE)
    new_im = s5_fin[..., S5_NS:].reshape(bc, DEPTH, 2, S5_GROUPS, S5_STATE)
    return (y_prompt, y_sample, new_state_rwkv, new_re, new_im)
```

```python
import functools
import math

import jax
import jax.numpy as jnp
import numpy as np
from jax import lax
from jax.experimental import pallas as pl
from jax.experimental.pallas import tpu as pltpu

F32 = jnp.float32
BF16 = jnp.bfloat16
HI = lax.Precision.HIGHEST

D_MODEL = 1024
DEPTH = 2
GRID_W = 64
NORM_EPS = 1e-6

HY_WIDTH = 384
HY_FILTER_HIDDEN = 64
HY_N_BANDS = 8
HY_POS_DIM = 1 + 2 * HY_N_BANDS
HY_FAST_DECAY_PCT = 0.3
HY_SLOW_DECAY_PCT = 1.5
HY_DECAY_TARGET = 1e-2

RW_HEAD_DIM = 64
RW_HEADS = 6
RW_WIDTH = RW_HEADS * RW_HEAD_DIM
RW_GN_EPS = 64e-5
RW_CHUNK = 64

S5_GROUPS = 16
S5_GROUP_CH = 16
S5_WIDTH = S5_GROUPS * S5_GROUP_CH
S5_STATE = 64
S5_NS = S5_GROUPS * S5_STATE

N_EXPERTS = 16
N_EXPERT_GROUPS = 4
EXPERTS_PER_GROUP = N_EXPERTS // N_EXPERT_GROUPS
EXPERT_FF = 512

C_GATE = 3 * D_MODEL
C_HY = 3 * HY_WIDTH
C_RKV = 3 * RW_WIDTH
C_LORA = 384
C_S5 = S5_WIDTH
IN_COLS = C_GATE + C_HY + C_RKV + C_LORA + C_S5

V7X_VMEM_LIMIT = 56 * 1024 * 1024

TM_TOK = 256
TL_SEQ = 256
TM_MOE = 1024
S5_ROWS = 2048


def _cparams(sem):
    return pltpu.CompilerParams(dimension_semantics=sem, vmem_limit_bytes=V7X_VMEM_LIMIT)


def _dot(a, b, precision=None):
    return jnp.dot(a, b, preferred_element_type=F32, precision=precision)


def _mod_kernel(c_ref, w_ref, b_ref, o_ref):
    c = c_ref[...]
    s = c * jax.nn.sigmoid(c)
    sh, sl = _split_bf16(s)
    wh, wl = _split_bf16(w_ref[...])
    o_ref[...] = _dot(sh, wh) + _dot(sh, wl) + _dot(sl, wh) + b_ref[...]


def _modulation(cond8, ada_w, ada_b, l):
    tn = 1536
    return pl.pallas_call(
        _mod_kernel,
        grid=(6 * D_MODEL // tn,),
        in_specs=[
            pl.BlockSpec((8, D_MODEL), lambda j: (0, 0)),
            pl.BlockSpec((None, D_MODEL, tn), lambda j: (l, 0, j)),
            pl.BlockSpec((None, 1, tn), lambda j: (l, 0, j)),
        ],
        out_specs=pl.BlockSpec((8, tn), lambda j: (0, j)),
        out_shape=jax.ShapeDtypeStruct((8, 6 * D_MODEL), F32),
        compiler_params=_cparams(("arbitrary",)),
        name="modulation",
    )(cond8, ada_w, ada_b.reshape(DEPTH, 1, 6 * D_MODEL))


def _inproj_kernel(x_ref, sh_ref, sc_ref, g_ref, w_ref, cw_ref, cb_ref, bias_ref,
                   og, ovq, oe, ox1, orkv, olo, os5, *, period):
    x = x_ref[...]
    xn = x * lax.rsqrt(jnp.mean(x * x, axis=-1, keepdims=True) + NORM_EPS) * g_ref[...]
    xn = (xn * (1.0 + sc_ref[...]) + sh_ref[...]).astype(BF16)
    off = 0
    for o_ref, width in ((og, C_GATE), (None, C_HY), (orkv, C_RKV), (olo, C_LORA), (os5, C_S5)):
        y = _dot(xn, w_ref[:, off:off + width])
        off += width
        if o_ref is og:
            o_ref[...] = jax.nn.sigmoid(y).astype(BF16)
            continue
        if o_ref is not None:
            o_ref[...] = y
            continue
        u = _conv3(y, cw_ref, cb_ref, period)
        v = u[:, 2 * HY_WIDTH:] * u[:, :HY_WIDTH]
        ovq[...] = v.astype(BF16)
        oe[...] = v * bias_ref[...]
        ox1[...] = u[:, HY_WIDTH:2 * HY_WIDTH]


def _mod_spec(which, tm, seq_len, n_mod):
    if n_mod == 1:
        return pl.BlockSpec((None, None, 1, D_MODEL), lambda i, *_: (0, which, 0, 0))
    return pl.BlockSpec((None, None, 1, D_MODEL), lambda i, *_: (i * tm // seq_len, which, 0, 0))


def _inproj(x, mod, norm_g3, w_in_bf, hy_conv_w, hy_conv_b3, hy_bias3, l, seq_len, period):
    n_tok = x.shape[0]
    tm = TM_TOK
    assert tm % period == 0 and (seq_len % tm == 0 or (tm % seq_len == 0 and mod.shape[0] == 1))
    outs = ((C_GATE, BF16), (HY_WIDTH, BF16), (HY_WIDTH, F32), (HY_WIDTH, F32),
            (C_RKV, F32), (C_LORA, F32), (C_S5, F32))
    par = lambda shape: pl.BlockSpec((None,) + shape, lambda i: (l,) + tuple(0 for _ in shape))
    return pl.pallas_call(
        functools.partial(_inproj_kernel, period=period),
        grid=(n_tok // tm,),
        in_specs=[
            pl.BlockSpec((tm, D_MODEL), lambda i: (i, 0)),
            _mod_spec(0, tm, seq_len, mod.shape[0]),
            _mod_spec(1, tm, seq_len, mod.shape[0]),
            par((1, D_MODEL)),
            pl.BlockSpec((None, D_MODEL, IN_COLS), lambda i: (l, 0, 0), pipeline_mode=pl.Buffered(1)),
            par((3, C_HY)), par((1, C_HY)), par((1, HY_WIDTH)),
        ],
        out_specs=[pl.BlockSpec((tm, w), lambda i: (i, 0)) for w, _ in outs],
        out_shape=[jax.ShapeDtypeStruct((n_tok, w), dt) for w, dt in outs],
        compiler_params=_cparams(("parallel",)),
        name="inproj",
    )(x, mod, mod, norm_g3, w_in_bf, hy_conv_w, hy_conv_b3, hy_bias3)


def _conv3(x, w_ref, b_ref, period):
    n = x.shape[0]
    t = lax.broadcasted_iota(jnp.int32, x.shape, 0) % period
    prev = jnp.where(t == 0, 0.0, pltpu.roll(x, 1, axis=0))
    nxt = jnp.where(t == period - 1, 0.0, pltpu.roll(x, n - 1, axis=0))
    return prev * w_ref[0:1, :] + x * w_ref[1:2, :] + nxt * w_ref[2:3, :] + b_ref[...]


def _hy_filter_kernel(feat_ref, w1_ref, b1_ref, f1_ref, w2_ref, b2_ref, f2_ref, w3_ref, dl_ref,
                      hs_ref, hd_ref, ny_ref):
    feats = feat_ref[...]
    h = jnp.sin(f1_ref[...] * (_dot(feats, w1_ref[...], HI) + b1_ref[...]))
    h = jnp.sin(f2_ref[...] * (_dot(h, w2_ref[...], HI) + b2_ref[...]))
    h = _dot(h, w3_ref[...], HI)
    h = h * jnp.exp(-feats[:, 0:1] * dl_ref[...])
    hf, hb = h[:, :HY_WIDTH], h[:, HY_WIDTH:]
    l1 = (jnp.sum(jnp.abs(hf), axis=0, keepdims=True)
          + jnp.sum(jnp.abs(hb), axis=0, keepdims=True) + 1e-6)
    hf = hf / l1
    hb = hb / l1
    row = lax.broadcasted_iota(jnp.int32, hb.shape, 0)
    hb0 = jnp.where(row == 0, 0.0, hb)
    hs = hf + hb0
    hs_ref[...] = hs.astype(BF16)
    hd_ref[...] = (hb0 - hf).astype(BF16)
    ny_ref[...] = jnp.sum(hs * _alternating(hs.shape), axis=0, keepdims=True) * (0.5 / hs.shape[0])


def _hy_positions(seq_len):
    t = np.arange(seq_len, dtype=np.float32)
    t01 = t / np.float32(max(seq_len - 1, 1))
    bands = np.linspace(1e-4, HY_N_BANDS - 1, HY_N_BANDS, dtype=np.float32)
    ang = np.float32(2.0 * math.pi / seq_len) * t[:, None] * bands[None, :]
    feats = np.concatenate([t01[:, None], np.cos(ang), -np.sin(ang)], axis=-1).astype(np.float32)
    feats = np.pad(feats, ((0, 0), (0, 128 - HY_POS_DIM)))
    max_decay = math.log(HY_DECAY_TARGET) / HY_FAST_DECAY_PCT
    min_decay = math.log(HY_DECAY_TARGET) / HY_SLOW_DECAY_PCT
    deltas = np.abs(np.linspace(min_decay, max_decay, HY_WIDTH, dtype=np.float32))
    return jnp.asarray(feats), jnp.asarray(np.concatenate([deltas, deltas])[None, :])


def _hy_filter(seq_len, l, w1p, b1, f1, w2, b2, f2, w3):
    feats, dl = _hy_positions(seq_len)
    hid = HY_FILTER_HIDDEN
    full = lambda shape: pl.BlockSpec(shape, lambda i: tuple(0 for _ in shape))
    par = lambda shape: pl.BlockSpec((None,) + shape, lambda i: (l,) + tuple(0 for _ in shape))
    return pl.pallas_call(
        _hy_filter_kernel,
        grid=(1,),
        in_specs=[full((seq_len, 128)), par((128, hid)), par((1, hid)), par((1, hid)),
                  par((hid, hid)), par((1, hid)), par((1, hid)), par((hid, 2 * HY_WIDTH)),
                  full((1, 2 * HY_WIDTH))],
        out_specs=[full((seq_len, HY_WIDTH)), full((seq_len, HY_WIDTH)), full((1, HY_WIDTH))],
        out_shape=[jax.ShapeDtypeStruct((seq_len, HY_WIDTH), BF16)] * 2
        + [jax.ShapeDtypeStruct((1, HY_WIDTH), F32)],
        compiler_params=pltpu.CompilerParams(vmem_limit_bytes=V7X_VMEM_LIMIT),
        name="hyena_filter",
    )(feats, w1p, b1, f1, w2, b2, f2, w3, dl)


def _dft_matrices(seq_len):
    n2 = 2 * seq_len
    k = np.arange(seq_len, dtype=np.int64)
    ang = ((k[:, None] * k[None, :]) % n2).astype(np.float64) * (2.0 * math.pi / n2)
    return jnp.asarray(np.stack([np.cos(ang), np.sin(ang)]).astype(np.float32)).astype(BF16)


def _alternating(shape):
    t = lax.broadcasted_iota(jnp.int32, shape, 0)
    return (1 - 2 * (t % 2)).astype(F32)


def _hy_spec_kernel(hs_ref, hd_ref, f_ref, p_ref, q_ref, *, tk, seq_len):
    kre = _dot(f_ref[0], hs_ref[...])
    kim = _dot(f_ref[1], hd_ref[...])
    row = lax.broadcasted_iota(jnp.int32, kre.shape, 0) + pl.program_id(0) * tk
    w = jnp.where(row == 0, 0.5 / seq_len, 1.0 / seq_len)
    p_ref[...] = kre * w
    q_ref[...] = kim * w


def _hy_spec(hs, hd, fw, seq_len, tk):
    nf = seq_len // tk
    full = pl.BlockSpec((seq_len, HY_WIDTH), lambda f: (0, 0))
    tile = pl.BlockSpec((tk, HY_WIDTH), lambda f: (f, 0))
    return pl.pallas_call(
        functools.partial(_hy_spec_kernel, tk=tk, seq_len=seq_len),
        grid=(nf,),
        in_specs=[full, full, pl.BlockSpec((2, tk, seq_len), lambda f: (0, f, 0))],
        out_specs=[tile, tile],
        out_shape=[jax.ShapeDtypeStruct((seq_len, HY_WIDTH), F32)] * 2,
        compiler_params=_cparams(("parallel",)),
        name="hyena_spectrum",
    )(hs, hd, fw)


def _hy_conv_kernel(v_ref, p_ref, q_ref, ny_ref, f_ref, g_ref, e_ref, x1_ref, o_ref, acc_ref):
    f = pl.program_id(1)
    v = v_ref[...]

    @pl.when(f == 0)
    def _():
        alt = _alternating(acc_ref.shape)
        v_nyq = jnp.sum(v.astype(F32) * alt, axis=0, keepdims=True)
        acc_ref[...] = alt * (v_nyq * ny_ref[...])

    c = _dot(f_ref[0], v)
    s = _dot(f_ref[1], v)
    p, q = p_ref[...], q_ref[...]
    yre = (c * p + s * q).astype(BF16)
    yim_neg = (s * p - c * q).astype(BF16)
    acc_ref[...] += _dot(g_ref[0], yre) + _dot(g_ref[1], yim_neg)

    @pl.when(f == pl.num_programs(1) - 1)
    def _():
        o_ref[...] = (acc_ref[...] + e_ref[...]) * x1_ref[...]


def _hy_conv(vq, p, q, ny, fw, e, x1, seq_len, batch, tk):
    nf = seq_len // tk
    seq = pl.BlockSpec((None, seq_len, HY_WIDTH), lambda b, f: (b, 0, 0))
    tile = pl.BlockSpec((tk, HY_WIDTH), lambda b, f: (f, 0))
    return pl.pallas_call(
        _hy_conv_kernel,
        grid=(batch, nf),
        in_specs=[seq, tile, tile, pl.BlockSpec((1, HY_WIDTH), lambda b, f: (0, 0)),
                  pl.BlockSpec((2, tk, seq_len), lambda b, f: (0, f, 0)),
                  pl.BlockSpec((2, seq_len, tk), lambda b, f: (0, 0, f)),
                  seq, seq],
        out_specs=seq,
        out_shape=jax.ShapeDtypeStruct((batch, seq_len, HY_WIDTH), F32),
        scratch_shapes=[pltpu.VMEM((seq_len, HY_WIDTH), F32)],
        compiler_params=_cparams(("parallel", "arbitrary")),
        name="hyena_longconv",
    )(vq, p, q, ny, fw, fw, e, x1)


def _split_bf16(x):
    hi = x.astype(BF16)
    return hi, (x - hi.astype(F32)).astype(BF16)


def _dot3(a, b):
    ah, al = _split_bf16(a)
    bh, bl = _split_bf16(b)
    n = a.shape[0]
    both = _dot(jnp.concatenate([ah, al], axis=0), bh)
    return both[:n] + both[n:] + _dot(ah, bl)


def _dot_exact_rhs(a, b):
    ah, al = _split_bf16(a)
    n = a.shape[0]
    both = _dot(jnp.concatenate([ah, al], axis=0), b.astype(BF16))
    return both[:n] + both[n:]


def _dot_exact_lhs(a, b):
    bh, bl = _split_bf16(b)
    a16 = a.astype(BF16)
    return _dot(a16, bh) + _dot(a16, bl)


def _dot_nt(a, b):
    return lax.dot_general(a, b, (((1,), (1,)), ((), ())), preferred_element_type=F32)


def _rw_pre_kernel(rkv_ref, lo_ref, cw_ref, cb_ref, w0_ref, wup_ref, a0_ref, aup_ref, gup_ref,
                   kk_ref, ka_ref, rk_ref, hs_ref, tri_ref,
                   v_out, bonus_out, g_out, at_out, rh_out, bt_out, kt_out, wt_out,
                   *, period):
    T = RW_CHUNK
    u = _conv3(rkv_ref[...], cw_ref, cb_ref, period)
    r, k, v = u[:, :RW_WIDTH], u[:, RW_WIDTH:2 * RW_WIDTH], u[:, 2 * RW_WIDTH:]
    lo = lo_ref[...]
    wraw = w0_ref[...] + _dot3(jnp.tanh(lo[:, 0:128]), wup_ref[...])
    z = -wraw
    softplus = jnp.maximum(z, 0.0) + jnp.log(1.0 + jnp.exp(-jnp.abs(z)))
    logw = -jnp.exp(-softplus - 0.5)
    a = jax.nn.sigmoid(a0_ref[...] + _dot3(lo[:, 128:256], aup_ref[...]))
    g_out[...] = _dot3(jax.nn.sigmoid(lo[:, 256:384]), gup_ref[...])
    head_sum = hs_ref[...]
    kk = k * kk_ref[...]
    kk = kk * lax.rsqrt(jnp.maximum(_dot_exact_rhs(kk * kk, head_sum), 1e-24))
    v_out[...] = v
    n_chunk = r.shape[0] // T
    lane = lax.broadcasted_iota(jnp.int32, (RW_WIDTH, 128), 1)
    kt_sum = jnp.zeros_like(k)
    for d in range(2):
        a_d = a[:, d * RW_WIDTH:(d + 1) * RW_WIDTH]
        lw_d = logw[:, d * RW_WIDTH:(d + 1) * RW_WIDTH]
        kt_d = k * (1.0 + (a_d - 1.0) * ka_ref[...])
        kt_sum = kt_sum + kt_d
        cum = _dot_exact_lhs(tri_ref[d], lw_d)
        w_inv = jnp.exp(-cum)
        at_out[d] = -kk * jnp.exp(cum - lw_d)
        rh_out[d] = r * jnp.exp(cum)
        bt = (kk * a_d * w_inv).T
        ktt = (kt_d * w_inv).T
        for j in range(0, n_chunk, 2):
            for src, dst in ((bt, bt_out), (ktt, kt_out)):
                pair = src[:, j * T:(j + 2) * T]
                swapped = pltpu.roll(pair, T, axis=1)
                dst[d, j] = jnp.where(lane < T, pair, swapped)
                dst[d, j + 1] = jnp.where(lane < T, swapped, pair)
        for j in range(n_chunk):
            last = j * T + (T - 1 if d == 0 else 0)
            wt_out[d, j] = jnp.exp(cum[last:last + 1, :])
    bonus_out[...] = _dot_exact_rhs(r * kt_sum * rk_ref[...], head_sum) * v


def _chunk_tri(n_rows):
    i = np.arange(n_rows)
    same = (i[:, None] // RW_CHUNK) == (i[None, :] // RW_CHUNK)
    fwd = same & (i[None, :] <= i[:, None])
    bwd = same & (i[None, :] >= i[:, None])
    return jnp.asarray(np.stack([fwd, bwd]).astype(np.float32))


def _rw_pre(rkv, lora, P, l, seq_len, batch, period):
    nt = seq_len // TL_SEQ
    cpt = TL_SEQ // RW_CHUNK
    nchunk = seq_len // RW_CHUNK
    blk = lambda w: pl.BlockSpec((None, TL_SEQ, w), lambda b, i: (b, i, 0))
    par = lambda shape: pl.BlockSpec((None,) + shape, lambda b, i: (l,) + tuple(0 for _ in shape))
    row_shape = jax.ShapeDtypeStruct((batch, seq_len, RW_WIDTH), F32)
    dir_shape = jax.ShapeDtypeStruct((2, batch, seq_len, RW_WIDTH), F32)
    dir_blk = pl.BlockSpec((2, None, TL_SEQ, RW_WIDTH), lambda b, i: (0, b, i, 0))
    tr_shape = jax.ShapeDtypeStruct((2, batch, nchunk, RW_WIDTH, 128), F32)
    tr_blk = pl.BlockSpec((2, None, cpt, RW_WIDTH, 128), lambda b, i: (0, b, i, 0, 0))
    wt_shape = jax.ShapeDtypeStruct((2, batch, nchunk, 1, RW_WIDTH), F32)
    wt_blk = pl.BlockSpec((2, None, cpt, 1, RW_WIDTH), lambda b, i: (0, b, i, 0, 0))
    return pl.pallas_call(
        functools.partial(_rw_pre_kernel, period=period),
        grid=(batch, nt),
        in_specs=[
            blk(C_RKV), blk(C_LORA),
            par((3, C_RKV)), par((1, C_RKV)),
            par((1, 2 * RW_WIDTH)), par((128, 2 * RW_WIDTH)),
            par((1, 2 * RW_WIDTH)), par((128, 2 * RW_WIDTH)),
            par((128, RW_WIDTH)),
            par((1, RW_WIDTH)), par((1, RW_WIDTH)), par((1, RW_WIDTH)),
            pl.BlockSpec((RW_WIDTH, RW_WIDTH), lambda b, i: (0, 0)),
            pl.BlockSpec((2, TL_SEQ, TL_SEQ), lambda b, i: (0, 0, 0)),
        ],
        out_specs=[blk(RW_WIDTH)] * 3 + [dir_blk, dir_blk, tr_blk, tr_blk, wt_blk],
        out_shape=[row_shape] * 3 + [dir_shape, dir_shape, tr_shape, tr_shape, wt_shape],
        compiler_params=_cparams(("parallel", "parallel")),
        name="rwkv_pre",
    )(rkv, lora, P["rw_conv_w"], P["rw_conv_b3"], P["rw_w0c"], P["rw_wupc"], P["rw_a0c"],
      P["rw_aupc"], P["rw_g_up"], P["rw_k_k3"], P["rw_k_a3"], P["rw_r_k3"], P["head_sum"],
      _chunk_tri(TL_SEQ))


RW_PAIRS = RW_HEADS // 2
RW_PREP_CHUNKS = 4


def _rw_prep_kernel(at_ref, rh_ref, v_ref, bt_ref, kt_ref, whi_ref, wlo_ref, rt_ref, ot_ref):
    d = pl.program_id(0)
    T = RW_CHUNK
    sgn = 1 - 2 * d
    ti = lax.broadcasted_iota(jnp.int32, (T, 128), 0)
    tj = lax.broadcasted_iota(jnp.int32, (T, 128), 1) % T
    before = (tj - ti) * sgn < 0
    before_eq = (tj - ti) * sgn <= 0
    eye2 = (ti == tj).astype(F32)
    bi = lax.broadcasted_iota(jnp.int32, (128, 128), 0)
    bj = lax.broadcasted_iota(jnp.int32, (128, 128), 1)
    same_head = (bi // T) == (bj // T)
    eye128 = (bi == bj).astype(F32)

    def bdiag(x):
        return jnp.where(same_head, jnp.concatenate([x, x], axis=0), 0.0)

    probs = [(j, p) for j in range(RW_PREP_CHUNKS) for p in range(RW_PAIRS)]
    rows = lambda j: slice(j * T, (j + 1) * T)
    cols = lambda p: slice(p * 128, (p + 1) * 128)
    a_l = [at_ref[rows(j), cols(p)] for j, p in probs]
    r_l = [rh_ref[rows(j), cols(p)] for j, p in probs]
    bd_b = [jnp.where(same_head, bt_ref[j, cols(p), :], 0.0) for j, p in probs]
    bd_k = [jnp.where(same_head, kt_ref[j, cols(p), :], 0.0) for j, p in probs]
    bk_l = [jnp.concatenate([b, k], axis=1) for b, k in zip(bd_b, bd_k)]
    gram_a = [_dot3(a, bk) for a, bk in zip(a_l, bk_l)]
    gram_r = [_dot(r.astype(BF16), bk.astype(BF16)) for r, bk in zip(r_l, bk_l)]
    l_ab = [jnp.where(before, g[:, :128], 0.0) for g in gram_a]
    l_ak = [jnp.where(before, g[:, 128:], 0.0) for g in gram_a]
    g_rb = [jnp.where(before_eq, g[:, :128], 0.0) for g in gram_r]
    g_rk = [jnp.where(before_eq, g[:, 128:], 0.0) for g in gram_r]
    m = [eye2 + l for l in l_ab]
    lp = [_dot3(l, bdiag(l)) for l in l_ab]
    n_sq = int(math.log2(T)) - 1
    for s in range(n_sq):
        if s < n_sq - 1:
            out = [_dot3(jnp.concatenate([mm, ll], axis=0), bdiag(ll)) for mm, ll in zip(m, lp)]
            m = [mm + o[:T] for mm, o in zip(m, out)]
            lp = [o[T:] for o in out]
        else:
            m = [mm + _dot3(mm, bdiag(ll)) for mm, ll in zip(m, lp)]
    x = [_dot3(mm, jnp.concatenate([bdiag(a), bdiag(l)], axis=1))
         for mm, a, l in zip(m, a_l, l_ak)]
    y = [_dot3(bt_ref[j, cols(p), 0:T], xx) for (j, p), xx in zip(probs, x)]
    bd_v = [bdiag(v_ref[rows(j), cols(p)]) for j, p in probs]
    psi = [_dot3(jnp.where(same_head, yy[:, 128:], 0.0) + k, v)
           for yy, k, v in zip(y, bd_k, bd_v)]
    for (j, p), yy, ps in zip(probs, y, psi):
        phi = jnp.where(same_head, eye128 + yy[:, :128], 0.0)
        w_bd = jnp.concatenate([phi, ps], axis=1)
        w_hi, w_lo = _split_bf16(w_bd[:T] + w_bd[T:])
        whi_ref[j, p] = w_hi
        wlo_ref[j, p] = w_lo
    z = [_dot(g.astype(BF16), jnp.concatenate([bdiag(xx[:, :128]), bdiag(xx[:, 128:])], axis=1).astype(BF16))
         for g, xx in zip(g_rb, x)]
    for (j, p), zz, r, g, v in zip(probs, z, r_l, g_rk, bd_v):
        rt_ref[rows(j), cols(p)] = (r + zz[:, :128]).astype(BF16)
        ot_ref[rows(j), cols(p)] = _dot((zz[:, 128:] + g).astype(BF16), v.astype(BF16))


def _rw_prep(at, rh, v, bt, kt, seq_len, batch):
    nchunk = seq_len // RW_CHUNK
    cb = RW_PREP_CHUNKS
    rows = cb * RW_CHUNK
    drow = pl.BlockSpec((None, None, rows, RW_WIDTH), lambda d, b, i: (d, b, i, 0))
    tr = pl.BlockSpec((None, None, cb, RW_WIDTH, 128), lambda d, b, i: (d, b, i, 0, 0))
    wsp = pl.BlockSpec((None, None, cb, RW_PAIRS, RW_CHUNK, 256), lambda d, b, i: (d, b, i, 0, 0, 0))
    w_shape = jax.ShapeDtypeStruct((2, batch, nchunk, RW_PAIRS, RW_CHUNK, 256), BF16)
    return pl.pallas_call(
        _rw_prep_kernel,
        grid=(2, batch, nchunk // cb),
        in_specs=[drow, drow, pl.BlockSpec((None, rows, RW_WIDTH), lambda d, b, i: (b, i, 0)), tr, tr],
        out_specs=[wsp, wsp, drow, drow],
        out_shape=[w_shape, w_shape,
                   jax.ShapeDtypeStruct((2, batch, seq_len, RW_WIDTH), BF16),
                   jax.ShapeDtypeStruct((2, batch, seq_len, RW_WIDTH), F32)],
        compiler_params=_cparams(("parallel", "parallel", "parallel")),
        name="rwkv_chunk_ops",
    )(at, rh, v, bt, kt)


def _rw_seq_kernel(whi_f, wlo_f, whi_b, wlo_b, rt_f, rt_b, ot_f, ot_b, wt_f, wt_b, s0_ref,
                   o_f, o_b, sfin_ref, st_ref, *, bb):
    c = pl.program_id(1)
    T = RW_CHUNK

    @pl.when(c == 0)
    def _():
        st_ref[...] = s0_ref[...]

    bi_ = lax.broadcasted_iota(jnp.int32, (128, 128), 0)
    bj_ = lax.broadcasted_iota(jnp.int32, (128, 128), 1)
    same_head = (bi_ // T) == (bj_ // T)
    ji = lax.broadcasted_iota(jnp.int32, (T, 128), 0)
    jj = lax.broadcasted_iota(jnp.int32, (T, 128), 1) % T
    eye2 = (ji == jj).astype(BF16)

    dirs = ((whi_f, wlo_f, rt_f, ot_f, wt_f, o_f), (whi_b, wlo_b, rt_b, ot_b, wt_b, o_b))
    probs = [(d, b, p) for d in range(2) for b in range(bb) for p in range(RW_PAIRS)]
    cols = lambda p: slice(p * 128, (p + 1) * 128)
    s_l =[st_ref[d, b, p] for d, b, p in probs]
    split = [_split_bf16(s) for s in s_l]
    zero = jnp.zeros((T, 128), BF16)
    lhs = [jnp.concatenate([jnp.concatenate([hi, eye2], axis=1), jnp.concatenate([lo, zero], axis=1)], axis=0)
           for hi, lo in split]
    ci = lax.broadcasted_iota(jnp.int32, (128, 256), 0)
    cj = lax.broadcasted_iota(jnp.int32, (128, 256), 1)
    own_block = (ci // T) == ((cj % 128) // T)

    def unpack(w):
        return jnp.where(own_block, jnp.concatenate([w, w], axis=0), 0)

    both = [_dot_nt(lh, unpack(dirs[d][0][b, p])) for (d, b, p), lh in zip(probs, lhs)]
    new = [bt[:T] + bt[T:] + _dot_nt(lh[:T], unpack(dirs[d][1][b, p]))
           for (d, b, p), lh, bt in zip(probs, lhs, both)]
    for (d, b, p), nw in zip(probs, new):
        st_ref[d, b, p] = nw * dirs[d][4][b, :, p * 128:(p + 1) * 128]
    for (d, b, p), s in zip(probs, s_l):
        _, _, rt, ot, _, o = dirs[d]
        bd_s = jnp.where(same_head, jnp.concatenate([s, s], axis=0), 0.0).astype(BF16)
        o[b, :, cols(p)] = _dot_nt(rt[b, :, cols(p)], bd_s) + ot[b, :, cols(p)]

    @pl.when(c == pl.num_programs(1) - 1)
    def _():
        for d, b, p in probs:
            s = st_ref[d, b, p]
            for hh in range(2):
                sfin_ref[b, d, 2 * p + hh] = s[:, hh * RW_HEAD_DIM:(hh + 1) * RW_HEAD_DIM]


def _rw_seq(whi, wlo, rt, ot, wt, s0p, seq_len, batch):
    nc = seq_len // RW_CHUNK
    bb = min(batch, 8)
    rev = lambda c: nc - 1 - c
    fwd = lambda c: c
    wsp = lambda d, ch: pl.BlockSpec((None, bb, None, RW_PAIRS, RW_CHUNK, 256),
                                     lambda g, c: (d, g, ch(c), 0, 0, 0))
    row = lambda d, ch: pl.BlockSpec((None, bb, RW_CHUNK, RW_WIDTH), lambda g, c: (d, g, ch(c), 0))
    wts = lambda d, ch: pl.BlockSpec((None, bb, None, 1, RW_WIDTH), lambda g, c: (d, g, ch(c), 0, 0))
    st = pl.BlockSpec((2, bb, RW_PAIRS, RW_HEAD_DIM, 128), lambda g, c: (0, g, 0, 0, 0))
    o_spec = lambda ch: pl.BlockSpec((bb, RW_CHUNK, RW_WIDTH), lambda g, c: (g, ch(c), 0))
    o_shape = jax.ShapeDtypeStruct((batch, seq_len, RW_WIDTH), F32)
    return pl.pallas_call(
        functools.partial(_rw_seq_kernel, bb=bb),
        grid=(batch // bb, nc),
        in_specs=[wsp(0, fwd), wsp(0, fwd), wsp(1, rev), wsp(1, rev),
                  row(0, fwd), row(1, rev), row(0, fwd), row(1, rev),
                  wts(0, fwd), wts(1, rev), st],
        out_specs=[o_spec(fwd), o_spec(rev),
                   pl.BlockSpec((bb, 2, RW_HEADS, RW_HEAD_DIM, RW_HEAD_DIM), lambda g, c: (g, 0, 0, 0, 0))],
        out_shape=[o_shape, o_shape,
                   jax.ShapeDtypeStruct((batch, 2, RW_HEADS, RW_HEAD_DIM, RW_HEAD_DIM), F32)],
        scratch_shapes=[pltpu.VMEM((2, bb, RW_PAIRS, RW_HEAD_DIM, 128), F32)],
        compiler_params=_cparams(("parallel", "arbitrary")),
        name="rwkv_state_scan",
    )(whi, wlo, whi, wlo, rt, rt, ot, ot, wt, wt, s0p)


def _rw_pack_state(s):
    b = s.shape[0]
    s = s.reshape(b, 2, RW_PAIRS, 2, RW_HEAD_DIM, RW_HEAD_DIM)
    return jnp.transpose(s, (1, 0, 2, 4, 3, 5)).reshape(2, b, RW_PAIRS, RW_HEAD_DIM, 128)


def _s5_kernel(u_ref, wb_ref, wc_ref, a_ref, x0_ref, y_ref, fin_ref, bu_ref, st_ref, tm_ref, *, batch, steps):
    d = pl.program_id(0)
    i = pl.program_id(1)
    ns = S5_NS
    halves = S5_WIDTH // 128

    @pl.when(i == 0)
    def _():
        st_ref[...] = x0_ref[...]

    for b in range(batch):
        for k in range(halves):
            tm_ref[k, pl.ds(b, steps, stride=batch), :] = u_ref[b, :, k * 128:(k + 1) * 128]
    u_tm = jnp.concatenate([tm_ref[k] for k in range(halves)], axis=1)
    bu_ref[...] = _dot(u_tm.astype(BF16), wb_ref[...])

    if batch % 8 == 0:
        lanes = 256
        for lc in range(ns // lanes):
            re_sl = slice(lc * lanes, (lc + 1) * lanes)
            im_sl = slice(ns + lc * lanes, ns + (lc + 1) * lanes)
            ar, ai = a_ref[:, re_sl], a_ref[:, im_sl]

            def body(s, carry, re_sl=re_sl, im_sl=im_sl, ar=ar, ai=ai):
                xr, xi = carry
                step = s + d * (steps - 1 - 2 * s)
                rows = pl.ds(pl.multiple_of(step * batch, 8), batch)
                nr = ar * xr - ai * xi + bu_ref[rows, re_sl]
                ni = ar * xi + ai * xr + bu_ref[rows, im_sl]
                bu_ref[rows, re_sl] = nr
                bu_ref[rows, im_sl] = ni
                return nr, ni

            xr, xi = lax.fori_loop(0, steps, body, (st_ref[:, re_sl], st_ref[:, im_sl]))
            st_ref[:, re_sl] = xr
            st_ref[:, im_sl] = xi
    else:
        ar, ai = a_ref[:, :ns], a_ref[:, ns:]
        half = lax.broadcasted_iota(jnp.int32, (8, ns), 0) // 4
        first = half == d

        def body(s, carry):
            xr, xi = carry
            pair = s + d * (steps // 2 - 1 - 2 * s)
            rows = pl.ds(pl.multiple_of(pair * 8, 8), 8)
            br, bi = bu_ref[rows, :ns], bu_ref[rows, ns:]
            r1 = ar * xr - ai * xi + br
            i1 = ar * xi + ai * xr + bi
            r1s, i1s = pltpu.roll(r1, 4, axis=0), pltpu.roll(i1, 4, axis=0)
            r2 = ar * r1s - ai * i1s + br
            i2 = ar * i1s + ai * r1s + bi
            bu_ref[rows, :ns] = jnp.where(first, r1, r2)
            bu_ref[rows, ns:] = jnp.where(first, i1, i2)
            return pltpu.roll(r2, 4, axis=0), pltpu.roll(i2, 4, axis=0)

        xr, xi = lax.fori_loop(0, steps // 2, body, (st_ref[:, :ns], st_ref[:, ns:]))
        st_ref[:, :ns] = xr
        st_ref[:, ns:] = xi

    y = _dot(bu_ref[...].astype(BF16), wc_ref[...])
    for k in range(halves):
        tm_ref[k] = y[:, k * 128:(k + 1) * 128]
    for b in range(batch):
        for k in range(halves):
            y_ref[b, :, k * 128:(k + 1) * 128] = tm_ref[k, pl.ds(b, steps, stride=batch), :]

    @pl.when(i == pl.num_programs(1) - 1)
    def _():
        fin_ref[...] = st_ref[...]


def _s5_scan(u, wb, wc, a_bar, x0, l, seq_len, batch):
    rows_blk = S5_ROWS
    steps = rows_blk // batch
    nblk = seq_len * batch // rows_blk
    rows = max(batch, 8)
    blk = lambda d, i: i + d * (nblk - 1 - 2 * i)
    return pl.pallas_call(
        functools.partial(_s5_kernel, batch=batch, steps=steps),
        grid=(2, nblk),
        in_specs=[
            pl.BlockSpec((batch, steps, S5_WIDTH), lambda d, i: (0, blk(d, i), 0)),
            pl.BlockSpec((None, None, S5_WIDTH, 2 * S5_NS), lambda d, i: (l, d, 0, 0)),
            pl.BlockSpec((None, None, 2 * S5_NS, S5_WIDTH), lambda d, i: (l, d, 0, 0)),
            pl.BlockSpec((None, None, 1, 2 * S5_NS), lambda d, i: (l, d, 0, 0)),
            pl.BlockSpec((None, rows, 2 * S5_NS), lambda d, i: (d, 0, 0)),
        ],
        out_specs=[
            pl.BlockSpec((None, batch, steps, S5_WIDTH), lambda d, i: (d, 0, blk(d, i), 0)),
            pl.BlockSpec((None, rows, 2 * S5_NS), lambda d, i: (d, 0, 0)),
        ],
        out_shape=[jax.ShapeDtypeStruct((2, batch, seq_len, S5_WIDTH), F32),
                   jax.ShapeDtypeStruct((2, rows, 2 * S5_NS), F32)],
        scratch_shapes=[pltpu.VMEM((rows_blk, 2 * S5_NS), F32), pltpu.VMEM((rows, 2 * S5_NS), F32),
                        pltpu.VMEM((S5_WIDTH // 128, rows_blk, 128), F32)],
        compiler_params=_cparams(("parallel", "arbitrary")),
        name="s5_scan",
    )(u, wb, wc, a_bar, x0)


def _gelu_tanh(x):
    return 0.5 * x * (1.0 + jnp.tanh(math.sqrt(2.0 / math.pi) * (x + 0.044715 * x * x * x)))


def _merge_kernel(x_ref, gate_ref, yhy_ref, of_ref, ob_ref, bonus_ref, g_ref, u5_ref, y5f_ref, y5b_ref,
                  gt1_ref, sh2_ref, sc2_ref,
                  hm_ref, gnw_ref, gnb_ref, s5d_ref, gluw_ref, glub_ref,
                  wbhy_ref, wbrw_ref, wbs5_ref, wout_ref, n2g_ref, rw_ref, rb_ref,
                  h_ref, hn_ref, gates_ref):
    o = of_ref[...] + ob_ref[...]
    head_mean = hm_ref[...]
    mu = _dot_exact_rhs(o, head_mean)
    oc = o - mu
    var = _dot_exact_rhs(oc * oc, head_mean)
    y_rw = (oc * lax.rsqrt(var + RW_GN_EPS) * gnw_ref[...] + gnb_ref[...] + bonus_ref[...]) * g_ref[...]
    y5 = _gelu_tanh(u5_ref[...] * s5d_ref[...] + y5f_ref[...] + y5b_ref[...])
    y5 = y5 * jax.nn.sigmoid(_dot(y5.astype(BF16), gluw_ref[...]) + glub_ref[...])
    gate = gate_ref[...].astype(F32)
    merged = (gate[:, :D_MODEL] * _dot(yhy_ref[...].astype(BF16), wbhy_ref[...])
              + gate[:, D_MODEL:2 * D_MODEL] * _dot(y_rw.astype(BF16), wbrw_ref[...])
              + gate[:, 2 * D_MODEL:] * _dot(y5.astype(BF16), wbs5_ref[...]))
    h = x_ref[...] + gt1_ref[...] * _dot(merged.astype(BF16), wout_ref[...])
    h_ref[...] = h
    hn = h * lax.rsqrt(jnp.mean(h * h, axis=-1, keepdims=True) + NORM_EPS) * n2g_ref[...]
    hn = hn * (1.0 + sc2_ref[...]) + sh2_ref[...]
    hn_ref[...] = hn.astype(BF16)

    scores = jax.nn.sigmoid(_dot3(hn, rw_ref[...]))
    sel = scores + rb_ref[...]
    scores_t, sel_t = scores.T, sel.T
    s_col = [sel_t[e:e + 1, :] for e in range(N_EXPERTS)]
    best_val = None
    best_grp = None
    for grp in range(N_EXPERT_GROUPS):
        a, b, c, dd = s_col[4 * grp:4 * grp + 4]
        hi1, lo1 = jnp.maximum(a, b), jnp.minimum(a, b)
        hi2, lo2 = jnp.maximum(c, dd), jnp.minimum(c, dd)
        m1 = jnp.maximum(hi1, hi2)
        m2 = jnp.maximum(jnp.minimum(hi1, hi2), jnp.where(hi1 >= hi2, lo1, lo2))
        val = m1 + m2
        if grp == 0:
            best_val, best_grp = val, jnp.zeros_like(val, dtype=jnp.int32)
        else:
            better = val > best_val
            best_val = jnp.where(better, val, best_val)
            best_grp = jnp.where(better, grp, best_grp)
    picked = []
    for e in range(N_EXPERTS):
        grp, pos = divmod(e, EXPERTS_PER_GROUP)
        rank = jnp.zeros_like(best_grp)
        for other in range(EXPERTS_PER_GROUP):
            if other == pos:
                continue
            so = s_col[4 * grp + other]
            ahead = (so > s_col[e]) if other > pos else (so >= s_col[e])
            rank = rank + ahead.astype(jnp.int32)
        take = jnp.logical_and(rank < 2, best_grp == grp)
        picked.append(jnp.where(take, scores_t[e:e + 1, :], 0.0))
    total = picked[0]
    for e in range(1, N_EXPERTS):
        total = total + picked[e]
    tm = gates_ref.shape[0]
    row = lax.broadcasted_iota(jnp.int32, (N_EXPERTS, tm), 0)
    gates_t = jnp.zeros((N_EXPERTS, tm), F32)
    for e in range(N_EXPERTS):
        gates_t = jnp.where(row == e, picked[e] / total, gates_t)
    gates_t = jnp.concatenate([gates_t, jnp.zeros((128 - N_EXPERTS, tm), F32)], axis=0)
    gates_ref[...] = gates_t.T


def _merge(x, gates_in, y_hy, o_f, o_b, bonus, g, u5, y5f, y5b, mod, P, l, seq_len):
    n_tok = x.shape[0]
    tm = TM_TOK
    tok = lambda w: pl.BlockSpec((tm, w), lambda i: (i, 0))
    n_mod = mod.shape[0]
    par = lambda shape: pl.BlockSpec((None,) + shape, lambda i: (l,) + tuple(0 for _ in shape))
    shared = lambda shape: pl.BlockSpec(shape, lambda i: tuple(0 for _ in shape))
    return pl.pallas_call(
        _merge_kernel,
        grid=(n_tok // tm,),
        in_specs=[
            tok(D_MODEL), tok(C_GATE), tok(HY_WIDTH), tok(RW_WIDTH), tok(RW_WIDTH), tok(RW_WIDTH),
            tok(RW_WIDTH), tok(S5_WIDTH), tok(S5_WIDTH), tok(S5_WIDTH),
            _mod_spec(2, tm, seq_len, n_mod), _mod_spec(3, tm, seq_len, n_mod), _mod_spec(4, tm, seq_len, n_mod),
            shared((RW_WIDTH, RW_WIDTH)), par((1, RW_WIDTH)), par((1, RW_WIDTH)),
            par((1, S5_WIDTH)), par((S5_WIDTH, S5_WIDTH)), par((1, S5_WIDTH)),
            par((HY_WIDTH, D_MODEL)), par((RW_WIDTH, D_MODEL)), par((S5_WIDTH, D_MODEL)),
            par((D_MODEL, D_MODEL)), par((1, D_MODEL)),
            shared((D_MODEL, 128)), shared((1, 128)),
        ],
        out_specs=[tok(D_MODEL), tok(D_MODEL), tok(128)],
        out_shape=[jax.ShapeDtypeStruct((n_tok, D_MODEL), F32),
                   jax.ShapeDtypeStruct((n_tok, D_MODEL), BF16),
                   jax.ShapeDtypeStruct((n_tok, 128), F32)],
        compiler_params=_cparams(("parallel",)),
        name="merge_router",
    )(x, gates_in, y_hy, o_f, o_b, bonus, g, u5, y5f, y5b, mod, mod, mod,
      P["head_mean"], P["rw_gn_w3"], P["rw_gn_b3"], P["s5_d3"], P["s5_glu_w_bf"], P["s5_glu_b3"],
      P["wb_hy_bf"], P["wb_rw_bf"], P["wb_s5_bf"], P["w_out_bf"], P["norm2_g3"],
      P["router_w_pad"], P["router_b_pad"])


def _moe_kernel(hn_ref, gates_ref, h_ref, gt2_ref, wg_ref, wu_ref, wd_ref, fg_ref, x_ref, acc_ref,
                *, final):
    e = pl.program_id(1)

    @pl.when(e == 0)
    def _():
        acc_ref[...] = jnp.zeros_like(acc_ref)

    hn = hn_ref[...]
    a = _dot(hn, wg_ref[...].astype(BF16))
    he = (a * jax.nn.sigmoid(a)) * _dot(hn, wu_ref[...].astype(BF16))
    lane = lax.broadcasted_iota(jnp.int32, gates_ref.shape, 1)
    gate = jnp.sum(jnp.where(lane == e, gates_ref[...], 0.0), axis=1, keepdims=True)
    acc_ref[...] += gate * _dot(he.astype(BF16), wd_ref[...].astype(BF16))

    @pl.when(e == pl.num_programs(1) - 1)
    def _():
        x = h_ref[...] + gt2_ref[...] * acc_ref[...]
        if final:
            x = x * lax.rsqrt(jnp.mean(x * x, axis=-1, keepdims=True) + NORM_EPS) * fg_ref[...]
        x_ref[...] = x


def _moe(hn, gates, h, mod, wg, wu, wd, final_g, l, seq_len, final):
    n_tok = hn.shape[0]
    tm = TM_MOE
    tok = lambda w: pl.BlockSpec((tm, w), lambda i, e: (i, 0))
    return pl.pallas_call(
        functools.partial(_moe_kernel, final=final),
        grid=(n_tok // tm, N_EXPERTS),
        in_specs=[
            tok(D_MODEL), tok(128), tok(D_MODEL),
            _mod_spec(5, tm, seq_len, mod.shape[0]),
            pl.BlockSpec((None, None, D_MODEL, EXPERT_FF), lambda i, e: (l, e, 0, 0)),
            pl.BlockSpec((None, None, D_MODEL, EXPERT_FF), lambda i, e: (l, e, 0, 0)),
            pl.BlockSpec((None, None, EXPERT_FF, D_MODEL), lambda i, e: (l, e, 0, 0)),
            pl.BlockSpec((1, D_MODEL), lambda i, e: (0, 0)),
        ],
        out_specs=tok(D_MODEL),
        out_shape=jax.ShapeDtypeStruct((n_tok, D_MODEL), F32),
        scratch_shapes=[pltpu.VMEM((tm, D_MODEL), F32)],
        compiler_params=_cparams(("parallel", "arbitrary")),
        name="moe",
    )(hn, gates, h, mod, wg, wu, wd, final_g.reshape(1, D_MODEL))


def _block_diag(blocks):
    g, r, c = blocks.shape[-3:]
    eye = jnp.eye(g, dtype=blocks.dtype)
    out = blocks[..., :, :, None, :] * eye[:, None, :, None]
    return out.reshape(blocks.shape[:-3] + (g * r, g * c))


def _s5_params(lam_re, lam_im, log_dt, b_re, b_im, c_re, c_im):
    lr, li = lam_re.astype(F32), lam_im.astype(F32)
    dt = jnp.exp(log_dt.astype(F32))[..., None]
    mag = jnp.exp(lr * dt)
    ar, ai = mag * jnp.cos(li * dt), mag * jnp.sin(li * dt)
    den = lr * lr + li * li
    qr = ((ar - 1.0) * lr + ai * li) / den
    qi = (ai * lr - (ar - 1.0) * li) / den
    bbr = qr[..., None] * b_re - qi[..., None] * b_im
    bbi = qr[..., None] * b_im + qi[..., None] * b_re
    tr = lambda m: jnp.swapaxes(m, -1, -2)
    wb = jnp.concatenate([_block_diag(tr(bbr)), _block_diag(tr(bbi))], axis=-1)
    wc = jnp.concatenate([_block_diag(tr(c_re)), -_block_diag(tr(c_im))], axis=-2)
    a_bar = jnp.concatenate([ar.reshape(DEPTH, 2, 1, S5_NS), ai.reshape(DEPTH, 2, 1, S5_NS)], axis=-1)
    return wb.astype(BF16), wc.astype(BF16), a_bar


def kernel(x_prompt, x_sample, state_rwkv, state_s5_re, state_s5_im, c, c_ctx, ada_w, ada_b, norm1_g, norm2_g, final_g, w_in, hy_conv_w, hy_conv_b, hy_f_w1, hy_f_b1, hy_f_freq1, hy_f_w2, hy_f_b2, hy_f_freq2, hy_f_w3, hy_bias, rw_conv_w, rw_conv_b, rw_w0, rw_w_up, rw_a0, rw_a_up, rw_g_up, rw_k_k, rw_k_a, rw_r_k, rw_gn_w, rw_gn_b, s5_lam_re, s5_lam_im, s5_log_dt, s5_b_re, s5_b_im, s5_c_re, s5_c_im, s5_d, s5_glu_w, s5_glu_b, wb_hy, wb_rw, wb_s5, w_out, router_w, router_b, exp_wg, exp_wu, exp_wd):
    bc, lc = x_prompt.shape[0], x_prompt.shape[1]
    bl, ll = x_sample.shape[0], x_sample.shape[1]
    n_ctx, n_lat = bc * lc, bl * ll
    assert bl == 4 and bc % 8 == 0 and lc % TL_SEQ == 0 and ll % TL_SEQ == 0
    assert n_ctx % TM_MOE == 0 and n_lat % TM_MOE == 0
    streams = ((lc, bc, lc), (ll, bl, GRID_W))

    head_id = np.arange(RW_WIDTH) // RW_HEAD_DIM
    head_sum = (head_id[:, None] == head_id[None, :]).astype(np.float32)
    P = dict(
        rw_conv_w=rw_conv_w, rw_conv_b3=rw_conv_b.reshape(DEPTH, 1, C_RKV),
        rw_w0c=rw_w0.reshape(DEPTH, 1, 2 * RW_WIDTH),
        rw_wupc=_block_diag(rw_w_up), rw_a0c=rw_a0.reshape(DEPTH, 1, 2 * RW_WIDTH),
        rw_aupc=_block_diag(rw_a_up), rw_g_up=rw_g_up,
        rw_k_k3=rw_k_k.reshape(DEPTH, 1, RW_WIDTH), rw_k_a3=rw_k_a.reshape(DEPTH, 1, RW_WIDTH),
        rw_r_k3=rw_r_k.reshape(DEPTH, 1, RW_WIDTH),
        head_sum=jnp.asarray(head_sum), head_mean=jnp.asarray(head_sum / RW_HEAD_DIM),
        rw_gn_w3=rw_gn_w.reshape(DEPTH, 1, RW_WIDTH), rw_gn_b3=rw_gn_b.reshape(DEPTH, 1, RW_WIDTH),
        s5_d3=s5_d.reshape(DEPTH, 1, S5_WIDTH), s5_glu_w_bf=s5_glu_w.astype(BF16),
        s5_glu_b3=s5_glu_b.reshape(DEPTH, 1, S5_WIDTH),
        wb_hy_bf=wb_hy.astype(BF16), wb_rw_bf=wb_rw.astype(BF16), wb_s5_bf=wb_s5.astype(BF16),
        w_out_bf=w_out.astype(BF16), norm2_g3=norm2_g.reshape(DEPTH, 1, D_MODEL),
        router_w_pad=jnp.pad(router_w, ((0, 0), (0, 128 - N_EXPERTS))),
        router_b_pad=jnp.pad(router_b, (0, 128 - N_EXPERTS)).reshape(1, 128),
    )
    w_in_bf = w_in.astype(BF16)
    norm1_g3 = norm1_g.reshape(DEPTH, 1, D_MODEL)
    hy_conv_b3, hy_bias3 = hy_conv_b.reshape(DEPTH, 1, C_HY), hy_bias.reshape(DEPTH, 1, HY_WIDTH)
    row3 = lambda a: a.reshape(DEPTH, 1, a.shape[-1])
    hy_filter_params = (jnp.pad(hy_f_w1, ((0, 0), (0, 128 - HY_POS_DIM), (0, 0))), row3(hy_f_b1),
                        row3(hy_f_freq1), hy_f_w2, row3(hy_f_b2), row3(hy_f_freq2), hy_f_w3)
    s5_wb, s5_wc, s5_abar = _s5_params(s5_lam_re, s5_lam_im, s5_log_dt, s5_b_re, s5_b_im, s5_c_re, s5_c_im)
    dft = {seq_len: _dft_matrices(seq_len) for seq_len in (lc, ll)}

    xs = [x_prompt.astype(F32).reshape(n_ctx, D_MODEL), x_sample.astype(F32).reshape(n_lat, D_MODEL)]
    cond8 = jnp.zeros((8, D_MODEL), F32).at[:bl].set(c.astype(F32)).at[bl].set(c_ctx.astype(F32))

    rw_s0 = (lambda l: jnp.zeros((2, bc, RW_PAIRS, RW_HEAD_DIM, 128), F32),
             lambda l: _rw_pack_state(state_rwkv[:, l].astype(F32)))
    s5_lat = jnp.concatenate([state_s5_re.reshape(bl, DEPTH, 2, S5_NS),
                              state_s5_im.reshape(bl, DEPTH, 2, S5_NS)], axis=-1).astype(F32)
    s5_lat = jnp.tile(jnp.moveaxis(s5_lat, 0, 2), (1, 1, 2, 1))
    s5_x0 = (jnp.zeros((DEPTH, 2, bc, 2 * S5_NS), F32), s5_lat)

    rw_new, s5_new = [], []
    for l in range(DEPTH):
        m = _modulation(cond8, ada_w, ada_b, l).reshape(8, 6, 1, D_MODEL)
        mods = (m[bl:bl + 1], m[:bl])
        for si, (seq_len, batch, period) in enumerate(streams):
            n_tok = seq_len * batch
            x, mod = xs[si], mods[si]
            g_in, vq, e, x1, rkv_in, lo_in, s5_in = _inproj(x, mod, norm1_g3, w_in_bf, hy_conv_w, hy_conv_b3,
                                                            hy_bias3, l, seq_len, period)
            seq = lambda a: a.reshape(batch, seq_len, a.shape[-1])
            tok = lambda a: a.reshape(n_tok, a.shape[-1])
            hs, hd, ny = _hy_filter(seq_len, l, *hy_filter_params)
            fw = dft[seq_len]
            tk = min(seq_len, 512)
            sp, sq = _hy_spec(hs, hd, fw, seq_len, tk)
            y_hy = _hy_conv(seq(vq), sp, sq, ny, fw, seq(e), seq(x1), seq_len, batch, tk)
            v_rw, bon, g_rw, at, rh, bt, kt, wt = _rw_pre(seq(rkv_in), seq(lo_in), P, l, seq_len, batch, period)
            whi, wlo, rt, ot = _rw_prep(at, rh, v_rw, bt, kt, seq_len, batch)
            o_f, o_b, s_fin = _rw_seq(whi, wlo, rt, ot, wt, rw_s0[si](l), seq_len, batch)
            y5d, fin5 = _s5_scan(seq(s5_in), s5_wb, s5_wc, s5_abar, s5_x0[si][l], l, seq_len, batch)
            y5d = y5d.reshape(2, n_tok, S5_WIDTH)
            if si == 0:
                rw_new.append(s_fin)
                s5_new.append(fin5)
            h, hn, gates = _merge(x, g_in, tok(y_hy), tok(o_f), tok(o_b), tok(bon), tok(g_rw), s5_in,
                                  y5d[0], y5d[1], mod, P, l, seq_len)
            xs[si] = _moe(hn, gates, h, mod, exp_wg, exp_wu, exp_wd, final_g, l, seq_len, final=(l == DEPTH - 1))

    y_prompt = xs[0].reshape(bc, lc, D_MODEL).astype(x_prompt.dtype)
    y_sample = xs[1].reshape(bl, ll, D_MODEL).astype(x_sample.dtype)
    new_state_rwkv = jnp.stack(rw_new, axis=1)
    s5_fin = jnp.stack(s5_new, axis=0)
    s5_fin = jnp.moveaxis(s5_fin, 2, 0)
    new_re = s5_fin[..., :S5_NS].reshape(bc, DEPTH, 2, S5_GROUPS, S5_STATE)
    new_im = s5_fin[..., S5_NS:].reshape(bc, DEPTH, 2, S5_GROUPS, S5_STATE)
    return (y_prompt, y_sample, new_state_rwkv, new_re, new_im)
```
